```python
import math
import jax
import jax.numpy as jnp
from jax import lax
import numpy as np

D_MODEL = 1024
BATCH = 2
SEQ = 8192
DEPTH = 2

MIX_W = D_MODEL
GROUP_W = MIX_W // 2
N_EVEN = (DEPTH + 1) // 2
N_ODD = DEPTH // 2
RMS_EPS = 1e-6
D_FF = 2816

HG_HEADS = 4
HG_DK = GROUP_W // HG_HEADS
HG_DV = GROUP_W // HG_HEADS
HG_CHUNK = 32
SC_CH = GROUP_W
CONV_W = 3
ML_HEADS = 4
ML_DH = GROUP_W // ML_HEADS
ML_CHUNK = 64
MB_HEADS = 4
MB_DH = GROUP_W // MB_HEADS
MB_BLOCK = 256
MB_TOPK = 3
MB_QCHUNK = 64
ROPE_THETA = 500000.0
ROPE_DIM = MB_DH // 4

AB_IN = 4 * GROUP_W + 3 * SC_CH
CD_IN = 4 * GROUP_W + 2 * ML_HEADS + 3 * GROUP_W

kernel_name = "hybrid_hgrn2_shortconv_mlstm_moba_macaron"


def rms_norm(x, gain):
    xf = x.astype(jnp.float32)
    y = xf * lax.rsqrt(jnp.mean(xf * xf, axis=-1, keepdims=True) + RMS_EPS)
    return (y * gain.astype(jnp.float32)).astype(x.dtype)


def swiglu(x, w_in, w_out):
    gate, up = jnp.split(x @ w_in, 2, axis=-1)
    return (jax.nn.silu(gate) * up) @ w_out


def split_heads(t, n_heads):
    return t.reshape(t.shape[0], t.shape[1], n_heads, -1)


def to_chunks(t, chunk):
    b, s = t.shape[0], t.shape[1]
    return jnp.moveaxis(t.reshape(b, s // chunk, chunk, *t.shape[2:]), 3, 1)


def from_chunks(t):
    b, h, nc, l, d = t.shape
    return jnp.moveaxis(t, 1, 3).reshape(b, nc * l, h, d)


def hgrn2_heads(q, f_pre, i_in, lb):
    f = lb + (1.0 - lb) * jax.nn.sigmoid(f_pre)
    qc = to_chunks(jax.nn.silu(q), HG_CHUNK)
    kc = to_chunks(1.0 - f, HG_CHUNK)
    vc = to_chunks(i_in, HG_CHUNK)
    b = jnp.cumsum(to_chunks(jnp.log(f), HG_CHUNK), axis=3)
    b_end = b[:, :, :, -1:, :]
    q_dec = qc * jnp.exp(b)
    k_dec = kc * jnp.exp(-b)
    causal = jnp.tril(jnp.ones((HG_CHUNK, HG_CHUNK), dtype=bool))
    attn = jnp.where(causal, jnp.einsum('bhcid,bhcjd->bhcij', q_dec, k_dec), 0.0)
    o_intra = jnp.einsum('bhcij,bhcjv->bhciv', attn, vc)
    d_state = jnp.einsum('bhcjd,bhcjv->bhcdv', kc * jnp.exp(b_end - b), vc)
    decay = jnp.exp(b_end[:, :, :, 0, :])

    def step(state, xs):
        dec, ds = xs
        return dec[..., None] * state + ds, state

    s0 = jnp.zeros(d_state.shape[:2] + d_state.shape[3:], d_state.dtype)
    _, s_prev = lax.scan(step, s0, (jnp.moveaxis(decay, 2, 0), jnp.moveaxis(d_state, 2, 0)))
    s_prev = jnp.moveaxis(s_prev, 0, 2)
    o = o_intra + jnp.einsum('bhcid,bhcdv->bhciv', q_dec, s_prev)
    return from_chunks(o)


def short_conv_channels(b_gate, c_gate, u, conv_w, conv_b):
    z = c_gate * u
    s = z.shape[1]
    zp = jnp.pad(z, ((0, 0), (CONV_W - 1, 0), (0, 0)))
    y = conv_b + sum(conv_w[j] * zp[:, j:j + s] for j in range(CONV_W))
    return b_gate * y


def mlstm_heads(q, k, v, log_i, log_f):
    qc, kc, vc = (to_chunks(t, ML_CHUNK) for t in (q, k, v))
    ic = to_chunks(log_i, ML_CHUNK)
    b = jnp.cumsum(to_chunks(log_f, ML_CHUNK), axis=-1)
    b_end = b[..., -1]
    causal = jnp.tril(jnp.ones((ML_CHUNK, ML_CHUNK), dtype=bool))
    d_log = jnp.where(causal, b[..., :, None] - b[..., None, :] + ic[..., None, :], -jnp.inf)
    w_end = b_end[..., None] - b + ic
    m_end = jnp.max(w_end, axis=-1)

    def step(carry, xs):
        c_mat, n_vec, m = carry
        k_c, v_c, w_c, m_c, be = xs
        m_new = jnp.maximum(be + m, m_c)
        a = jnp.exp(be + m - m_new)
        wk = jnp.exp(w_c - m_new[..., None])[..., None] * k_c
        c_new = a[..., None, None] * c_mat + jnp.einsum('bhld,bhle->bhde', wk, v_c)
        n_new = a[..., None] * n_vec + wk.sum(axis=-2)
        return (c_new, n_new, m_new), (c_mat, n_vec, m)

    bsz, n_h, dh = q.shape[0], q.shape[2], q.shape[3]
    init = (jnp.zeros((bsz, n_h, dh, dh), q.dtype), jnp.zeros((bsz, n_h, dh), q.dtype),
            jnp.zeros((bsz, n_h), q.dtype))
    xs = tuple(jnp.moveaxis(t, 2, 0) for t in (kc, vc, w_end, m_end, b_end))
    _, (c_prev, n_prev, m_prev) = lax.scan(step, init, xs)
    c_prev = jnp.moveaxis(c_prev, 0, 2)
    n_prev = jnp.moveaxis(n_prev, 0, 2)
    m_prev = jnp.moveaxis(m_prev, 0, 2)
    log_inter = b + m_prev[..., None]
    m_t = jnp.maximum(log_inter, jnp.max(d_log, axis=-1))
    weights = jnp.exp(d_log - m_t[..., None]) * jnp.einsum('bhcid,bhcjd->bhcij', qc, kc)
    a_in = jnp.exp(log_inter - m_t)
    num = jnp.einsum('bhcij,bhcje->bhcie', weights, vc) + a_in[..., None] * jnp.einsum('bhcid,bhcde->bhcie', qc, c_prev)
    den = weights.sum(axis=-1) + a_in * jnp.einsum('bhcid,bhcd->bhci', qc, n_prev)
    h = num / jnp.maximum(jnp.abs(den), jnp.exp(-m_t))[..., None]
    return from_chunks(h)


def rope_partial(x, pos):
    half = ROPE_DIM // 2
    inv_freq = jnp.float32(ROPE_THETA) ** (-jnp.arange(half, dtype=jnp.float32) * 2.0 / ROPE_DIM)
    ang = pos.astype(jnp.float32)[:, None] * inv_freq[None, :]
    cos = jnp.cos(ang)[None, :, None, :]
    sin = jnp.sin(ang)[None, :, None, :]
    x1, x2, x_pass = x[..., :half], x[..., half:ROPE_DIM], x[..., ROPE_DIM:]
    return jnp.concatenate([x1 * cos - x2 * sin, x2 * cos + x1 * sin, x_pass], axis=-1)


def moba_heads(q, k, v):
    bsz, s, n_h, dh = q.shape
    s_pad = -(-s // MB_BLOCK) * MB_BLOCK
    n_blk = s_pad // MB_BLOCK
    n_sel = min(MB_TOPK, n_blk)
    padw = ((0, 0), (0, s_pad - s), (0, 0), (0, 0))
    q, k, v = (jnp.pad(t, padw).transpose(0, 2, 1, 3) for t in (q, k, v))
    k_blk = k.reshape(bsz, n_h, n_blk, MB_BLOCK, dh)
    v_blk = v.reshape(bsz, n_h, n_blk, MB_BLOCK, dh)
    k_mean = k_blk.mean(axis=3)
    n_qc = s_pad // MB_QCHUNK
    q_chunks = jnp.moveaxis(q.reshape(bsz, n_h, n_qc, MB_QCHUNK, dh), 2, 0)
    b_idx = jnp.arange(bsz)[:, None, None, None]
    h_idx = jnp.arange(n_h)[None, :, None, None]
    blk_ids = jnp.arange(n_blk)

    def attend(args):
        c, q_c = args
        q_pos = c * MB_QCHUNK + jnp.arange(MB_QCHUNK)
        own = (c * MB_QCHUNK) // MB_BLOCK
        gate = jnp.einsum('bhqd,bhnd->bhqn', q_c, k_mean)
        gate = jnp.where(blk_ids < own, gate, -jnp.inf)
        g_val, sel = lax.top_k(gate, n_sel)
        sel_ok = jnp.isfinite(g_val)
        k_sel = k_blk[b_idx, h_idx, sel]
        v_sel = v_blk[b_idx, h_idx, sel]
        s_past = jnp.einsum('bhqd,bhqnld->bhqnl', q_c, k_sel)
        s_past = jnp.where(sel_ok[..., None], s_past, -jnp.inf).reshape(bsz, n_h, MB_QCHUNK, n_sel * MB_BLOCK)
        k_own = lax.dynamic_index_in_dim(k_blk, own, axis=2, keepdims=False)
        v_own = lax.dynamic_index_in_dim(v_blk, own, axis=2, keepdims=False)
        k_pos = own * MB_BLOCK + jnp.arange(MB_BLOCK)
        s_own = jnp.where(k_pos[None, :] <= q_pos[:, None], jnp.einsum('bhqd,bhld->bhql', q_c, k_own), -jnp.inf)
        p = jax.nn.softmax(jnp.concatenate([s_past, s_own], axis=-1), axis=-1)
        p_past = p[..., :n_sel * MB_BLOCK].reshape(bsz, n_h, MB_QCHUNK, n_sel, MB_BLOCK)
        p_own = p[..., n_sel * MB_BLOCK:]
        return jnp.einsum('bhqnl,bhqnld->bhqd', p_past, v_sel) + jnp.einsum('bhql,bhld->bhqd', p_own, v_own)

    out = lax.map(attend, (jnp.arange(n_qc), q_chunks))
    out = jnp.moveaxis(out, 0, 2).reshape(bsz, n_h, s_pad, dh)[:, :, :s]
    return out.transpose(0, 2, 1, 3)


def mixer_ab(h, w_in, w_out, lb, hg_norm, conv_w, conv_b):
    proj = (h @ w_in).astype(jnp.float32)
    q, f_pre, i_in, g, sc_b, sc_c, sc_u = jnp.split(proj, 7, axis=-1)
    o = hgrn2_heads(split_heads(q, HG_HEADS), split_heads(f_pre, HG_HEADS),
                    split_heads(i_in, HG_HEADS), lb.reshape(HG_HEADS, HG_DK))
    o = rms_norm(o, hg_norm) * jax.nn.silu(split_heads(g, HG_HEADS))
    y_hg = o.reshape(h.shape[0], h.shape[1], GROUP_W)
    y_sc = short_conv_channels(sc_b, sc_c, sc_u, conv_w, conv_b)
    return jnp.concatenate([y_hg, y_sc], axis=-1).astype(h.dtype) @ w_out


def mixer_cd(h, w_in, w_out, gate_bias, ml_norm):
    bsz, s = h.shape[0], h.shape[1]
    proj = (h @ w_in).astype(jnp.float32)
    ml_q, ml_k, ml_v, ml_o = jnp.split(proj[..., :4 * GROUP_W], 4, axis=-1)
    gates = proj[..., 4 * GROUP_W:4 * GROUP_W + 2 * ML_HEADS]
    mb_q, mb_k, mb_v = jnp.split(proj[..., 4 * GROUP_W + 2 * ML_HEADS:], 3, axis=-1)
    log_i = gates[..., :ML_HEADS] + gate_bias[0]
    log_f = jax.nn.log_sigmoid(gates[..., ML_HEADS:] + gate_bias[1])
    h_ml = mlstm_heads(split_heads(ml_q, ML_HEADS) * ML_DH ** -0.5, split_heads(ml_k, ML_HEADS),
                       split_heads(ml_v, ML_HEADS), log_i, log_f)
    h_ml = rms_norm(h_ml, ml_norm) * jax.nn.sigmoid(split_heads(ml_o, ML_HEADS))
    pos = jnp.arange(s)
    y_mb = moba_heads(rope_partial(split_heads(mb_q, MB_HEADS), pos) * MB_DH ** -0.5,
                      rope_partial(split_heads(mb_k, MB_HEADS), pos), split_heads(mb_v, MB_HEADS))
    y = jnp.concatenate([h_ml.reshape(bsz, s, GROUP_W), y_mb.reshape(bsz, s, GROUP_W)], axis=-1)
    return y.astype(h.dtype) @ w_out


def setup_inputs(seed: int = 0) -> dict:
    key = jax.random.key(seed)
    ks = jax.random.split(key, 17)

    def dense(k, shape, fan_in):
        return jax.random.normal(k, shape, jnp.float32) * fan_in ** -0.5

    def gain(k, shape):
        return 1.0 + 0.02 * jax.random.normal(k, shape, jnp.float32)

    gate_bias = jnp.stack([0.1 * jax.random.normal(ks[13], (N_ODD, ML_HEADS), jnp.float32),
                           jnp.linspace(3.0, 6.0, ML_HEADS, dtype=jnp.float32)[None, :]
                           + 0.1 * jax.random.normal(ks[14], (N_ODD, ML_HEADS), jnp.float32)], axis=1)
    return {
        "x": jax.random.normal(ks[0], (BATCH, SEQ, D_MODEL), jnp.float32),
        "ffn_norm": gain(ks[1], (DEPTH, 2, D_MODEL)),
        "ffn_w_in": dense(ks[2], (DEPTH, 2, D_MODEL, 2 * D_FF), D_MODEL),
        "ffn_w_out": dense(ks[3], (DEPTH, 2, D_FF, D_MODEL), D_FF),
        "mix_norm": gain(ks[4], (DEPTH, D_MODEL)),
        "ab_w_in": dense(ks[5], (N_EVEN, D_MODEL, AB_IN), D_MODEL),
        "ab_w_out": dense(ks[6], (N_EVEN, MIX_W, D_MODEL), MIX_W),
        "hgrn_lb_logits": 0.1 * jax.random.normal(ks[7], (DEPTH + 1, HG_HEADS * HG_DK), jnp.float32),
        "hgrn_out_norm": gain(ks[8], (N_EVEN, HG_HEADS, HG_DV)),
        "conv_w": dense(ks[9], (N_EVEN, CONV_W, SC_CH), CONV_W),
        "conv_b": 0.02 * jax.random.normal(ks[10], (N_EVEN, SC_CH), jnp.float32),
        "cd_w_in": dense(ks[11], (N_ODD, D_MODEL, CD_IN), D_MODEL),
        "cd_w_out": dense(ks[12], (N_ODD, MIX_W, D_MODEL), MIX_W),
        "mlstm_gate_bias": gate_bias,
        "mlstm_out_norm": gain(ks[15], (N_ODD, ML_HEADS, ML_DH)),
        "final_norm": gain(ks[16], (D_MODEL,)),
    }


def reference(x, ffn_norm, ffn_w_in, ffn_w_out, mix_norm, ab_w_in, ab_w_out, hgrn_lb_logits,
              hgrn_out_norm, conv_w, conv_b, cd_w_in, cd_w_out, mlstm_gate_bias, mlstm_out_norm,
              final_norm):
    lower_bounds = jnp.cumsum(jax.nn.softmax(hgrn_lb_logits.astype(jnp.float32), axis=0), axis=0)
    for layer in range(DEPTH):
        x = x + 0.5 * swiglu(rms_norm(x, ffn_norm[layer, 0]), ffn_w_in[layer, 0], ffn_w_out[layer, 0])
        h = rms_norm(x, mix_norm[layer])
        if layer % 2 == 0:
            e = layer // 2
            x = x + mixer_ab(h, ab_w_in[e], ab_w_out[e], lower_bounds[layer], hgrn_out_norm[e],
                             conv_w[e], conv_b[e])
        else:
            o = layer // 2
            x = x + mixer_cd(h, cd_w_in[o], cd_w_out[o], mlstm_gate_bias[o], mlstm_out_norm[o])
        x = x + 0.5 * swiglu(rms_norm(x, ffn_norm[layer, 1]), ffn_w_in[layer, 1], ffn_w_out[layer, 1])
    return rms_norm(x, final_norm)
```

```python
import functools

import jax
import jax.numpy as jnp
from jax import lax
from jax.experimental import pallas as pl
from jax.experimental.pallas import tpu as pltpu

F32 = jnp.float32
BF16 = jnp.bfloat16

D_MODEL = 1024
D_FF = 2816
GROUP_W = 512
N_HEADS = 4
D_HEAD = 128
RMS_EPS = 1e-6
HG_CHUNK = 32
ML_CHUNK = 64
MB_BLOCK = 256
MB_TOPK = 3
CONV_W = 3
ROPE_THETA = 500000.0
ROPE_DIM = D_HEAD // 4

LANES = 128
VMEM_LIMIT = 48 * 1024 * 1024

FFN_TM = 512
FFN_TF = 1408
PROJ_TM = 512
SEQ_T = 256

AB_COLS = 7 * GROUP_W
CD_COLS = 7 * GROUP_W + 2 * LANES


def _rms(x, gain):
    return x * lax.rsqrt(jnp.mean(x * x, axis=-1, keepdims=True) + RMS_EPS) * gain


def _dot(a, b):
    return jnp.dot(a, b, preferred_element_type=F32)


def _dot_nt(a, b):
    return lax.dot_general(a, b, (((1,), (1,)), ((), ())), preferred_element_type=F32)


def _dot_mask(mask01, x):
    hi = x.astype(BF16)
    r1 = x - hi.astype(F32)
    mid = r1.astype(BF16)
    lo = (r1 - mid.astype(F32)).astype(BF16)
    return _dot(mask01, hi) + _dot(mask01, mid) + _dot(mask01, lo)


def _chunk_masks(n, chunk):
    r = lax.broadcasted_iota(jnp.int32, (n, n), 0)
    c = lax.broadcasted_iota(jnp.int32, (n, n), 1)
    same = (r // chunk) == (c // chunk)
    return jnp.where(same, jnp.where(c <= r, 1, 0), 0) > 0, same


def _expand_mask(n, chunk, width):
    n_c = n // chunk
    r = lax.broadcasted_iota(jnp.int32, (n, n_c * width), 0)
    c = lax.broadcasted_iota(jnp.int32, (n, n_c * width), 1)
    return (r // chunk) == (c // width)


def _ffn_body(x_ref, g_ref, wg_ref, wu_ref, wo_ref, fg_ref, o_ref, hn_ref, acc_ref, *, n_f, final_norm):
    j = pl.program_id(1)

    @pl.when(j == 0)
    def _():
        hn_ref[...] = _rms(x_ref[...], g_ref[...]).astype(BF16)
        acc_ref[...] = jnp.zeros_like(acc_ref)

    h = hn_ref[...]
    gate = _dot(h, wg_ref[...])
    up = _dot(h, wu_ref[...])
    act = (gate * jax.nn.sigmoid(gate) * up).astype(BF16)
    acc_ref[...] += _dot(act, wo_ref[...])

    @pl.when(j == n_f - 1)
    def _():
        y = x_ref[...] + 0.5 * acc_ref[...]
        if final_norm:
            y = _rms(y, fg_ref[...])
        o_ref[...] = y


def _ffn(x, gain, w_in, w_out, final_gain, final_norm):
    t = x.shape[0]
    n_f = D_FF // FFN_TF
    return pl.pallas_call(
        functools.partial(_ffn_body, n_f=n_f, final_norm=final_norm),
        out_shape=jax.ShapeDtypeStruct((t, D_MODEL), F32),
        grid=(t // FFN_TM, n_f),
        in_specs=[
            pl.BlockSpec((FFN_TM, D_MODEL), lambda i, j: (i, 0)),
            pl.BlockSpec((1, D_MODEL), lambda i, j: (0, 0)),
            pl.BlockSpec((D_MODEL, FFN_TF), lambda i, j: (0, j)),
            pl.BlockSpec((D_MODEL, FFN_TF), lambda i, j: (0, j + D_FF // FFN_TF)),
            pl.BlockSpec((FFN_TF, D_MODEL), lambda i, j: (j, 0)),
            pl.BlockSpec((1, D_MODEL), lambda i, j: (0, 0)),
        ],
        out_specs=pl.BlockSpec((FFN_TM, D_MODEL), lambda i, j: (i, 0)),
        scratch_shapes=[pltpu.VMEM((FFN_TM, D_MODEL), BF16), pltpu.VMEM((FFN_TM, D_MODEL), F32)],
        compiler_params=pltpu.CompilerParams(
            dimension_semantics=("parallel", "arbitrary"), vmem_limit_bytes=VMEM_LIMIT),
        name="ffn",
    )(x, gain.reshape(1, D_MODEL), w_in, w_in, w_out, final_gain.reshape(1, D_MODEL))


def _inproj_body(x_ref, g_ref, w_ref, o_ref):
    o_ref[...] = _dot(_rms(x_ref[...], g_ref[...]).astype(BF16), w_ref[...])


def _inproj(x, gain, w):
    t, n = x.shape[0], w.shape[1]
    return pl.pallas_call(
        _inproj_body,
        out_shape=jax.ShapeDtypeStruct((t, n), F32),
        grid=(t // PROJ_TM,),
        in_specs=[
            pl.BlockSpec((PROJ_TM, D_MODEL), lambda i: (i, 0)),
            pl.BlockSpec((1, D_MODEL), lambda i: (0, 0)),
            pl.BlockSpec((D_MODEL, n), lambda i: (0, 0)),
        ],
        out_specs=pl.BlockSpec((PROJ_TM, n), lambda i: (i, 0)),
        compiler_params=pltpu.CompilerParams(
            dimension_semantics=("parallel",), vmem_limit_bytes=VMEM_LIMIT),
        name="inproj",
    )(x, gain.reshape(1, D_MODEL), w)


def _outproj_body(x_ref, ya_ref, yb_ref, wa_ref, wb_ref, o_ref):
    o_ref[...] = x_ref[...] + _dot(ya_ref[...], wa_ref[...]) + _dot(yb_ref[...], wb_ref[...])


def _outproj(x, ya, yb, w):
    t = x.shape[0]
    return pl.pallas_call(
        _outproj_body,
        out_shape=jax.ShapeDtypeStruct((t, D_MODEL), F32),
        grid=(t // PROJ_TM,),
        in_specs=[
            pl.BlockSpec((PROJ_TM, D_MODEL), lambda i: (i, 0)),
            pl.BlockSpec((PROJ_TM, GROUP_W), lambda i: (i, 0)),
            pl.BlockSpec((PROJ_TM, GROUP_W), lambda i: (i, 0)),
            pl.BlockSpec((GROUP_W, D_MODEL), lambda i: (0, 0)),
            pl.BlockSpec((GROUP_W, D_MODEL), lambda i: (1, 0)),
        ],
        out_specs=pl.BlockSpec((PROJ_TM, D_MODEL), lambda i: (i, 0)),
        compiler_params=pltpu.CompilerParams(
            dimension_semantics=("parallel",), vmem_limit_bytes=VMEM_LIMIT),
        name="outproj",
    )(x, ya, yb, w, w)


def _hgrn_conv_body(p_ref, lbl_ref, hgn_ref, cw_ref, cb_ref, yhg_ref, ysc_ref, st_ref, zb_ref, *, layer):
    s_idx = pl.program_id(1)
    n_c = SEQ_T // HG_CHUNK

    @pl.when(s_idx == 0)
    def _():
        st_ref[...] = jnp.zeros_like(st_ref)
        zb_ref[0:8, :] = jnp.zeros((8, GROUP_W), F32)

    lg = lbl_ref[...]
    ex = jnp.exp(lg - jnp.max(lg, axis=0, keepdims=True))
    sm = ex / jnp.sum(ex, axis=0, keepdims=True)
    lb = jnp.sum(sm[0:layer + 1, :], axis=0, keepdims=True)

    tril, same = _chunk_masks(SEQ_T, HG_CHUNK)
    tril01 = jnp.where(tril, 1.0, 0.0).astype(BF16)
    same01 = jnp.where(same, 1.0, 0.0).astype(BF16)
    emask = _expand_mask(SEQ_T, HG_CHUNK, D_HEAD)

    f = lb + (1.0 - lb) * jax.nn.sigmoid(p_ref[:, GROUP_W:2 * GROUP_W])
    logf = jnp.log(f)
    b_all = _dot_mask(tril01, logf)
    e_all = _dot_mask(same01, logf)

    for h in range(N_HEADS):
        lo, hi = h * D_HEAD, (h + 1) * D_HEAD
        q = p_ref[:, lo:hi]
        v = p_ref[:, 2 * GROUP_W + lo:2 * GROUP_W + hi]
        g = p_ref[:, 3 * GROUP_W + lo:3 * GROUP_W + hi]
        b = b_all[:, lo:hi]
        e = e_all[:, lo:hi]
        kk = 1.0 - f[:, lo:hi]
        q_dec = (q * jax.nn.sigmoid(q) * jnp.exp(b)).astype(BF16)
        k_dec = (kk * jnp.exp(-b)).astype(BF16)
        k_end = (kk * jnp.exp(e - b)).astype(BF16)
        v16 = v.astype(BF16)

        attn = jnp.where(tril, _dot_nt(q_dec, k_dec), 0.0)
        o = _dot(attn.astype(BF16), v16)

        k_exp = jnp.where(emask, jnp.tile(k_end, (1, n_c)), jnp.zeros((), BF16))
        d_all = _dot(v.T.astype(BF16), k_exp)

        st = st_ref[h]
        prev = []
        for c in range(n_c):
            prev.append(st)
            decay = jnp.exp(e[c * HG_CHUNK:c * HG_CHUNK + 1, :])
            st = decay * st + d_all[:, c * D_HEAD:(c + 1) * D_HEAD]
        st_ref[h] = st
        s_prev = jnp.concatenate(prev, axis=1).astype(BF16)
        q_exp = jnp.where(emask, jnp.tile(q_dec, (1, n_c)), jnp.zeros((), BF16))
        o = o + _dot_nt(q_exp, s_prev)

        o = _rms(o, hgn_ref[:, lo:hi]) * (g * jax.nn.sigmoid(g))
        yhg_ref[:, lo:hi] = o.astype(BF16)

    z = p_ref[:, 5 * GROUP_W:6 * GROUP_W] * p_ref[:, 6 * GROUP_W:7 * GROUP_W]
    zb_ref[8:SEQ_T + 8, :] = z
    z1 = zb_ref[7:SEQ_T + 7, :]
    z2 = zb_ref[6:SEQ_T + 6, :]
    y = cb_ref[...] + cw_ref[0:1, :] * z2 + cw_ref[1:2, :] * z1 + cw_ref[2:3, :] * z
    ysc_ref[...] = (p_ref[:, 4 * GROUP_W:5 * GROUP_W] * y).astype(BF16)
    zb_ref[0:8, :] = zb_ref[SEQ_T:SEQ_T + 8, :]


def _hgrn_conv(proj, lb_logits, hg_norm, conv_w, conv_b, layer):
    bsz, seq = proj.shape[0], proj.shape[1]
    n_l = lb_logits.shape[0]
    row = lambda b, s: (0, 0)
    return pl.pallas_call(
        functools.partial(_hgrn_conv_body, layer=layer),
        out_shape=[jax.ShapeDtypeStruct((bsz, seq, GROUP_W), BF16)] * 2,
        grid=(bsz, seq // SEQ_T),
        in_specs=[
            pl.BlockSpec((None, SEQ_T, AB_COLS), lambda b, s: (b, s, 0)),
            pl.BlockSpec((n_l, GROUP_W), row),
            pl.BlockSpec((1, GROUP_W), row),
            pl.BlockSpec((CONV_W, GROUP_W), row),
            pl.BlockSpec((1, GROUP_W), row),
        ],
        out_specs=[pl.BlockSpec((None, SEQ_T, GROUP_W), lambda b, s: (b, s, 0))] * 2,
        scratch_shapes=[pltpu.VMEM((N_HEADS, D_HEAD, D_HEAD), F32), pltpu.VMEM((SEQ_T + 8, GROUP_W), F32)],
        compiler_params=pltpu.CompilerParams(
            dimension_semantics=("arbitrary", "arbitrary"), vmem_limit_bytes=VMEM_LIMIT),
        name="hgrn_conv",
    )(proj, lb_logits, hg_norm.reshape(1, GROUP_W), conv_w, conv_b.reshape(1, GROUP_W))


def _mlstm_body(p_ref, gi_ref, gf_ref, bias_ref, mln_ref, o_ref, c_ref, n_ref, m_ref):
    s_idx = pl.program_id(1)
    n_c = SEQ_T // ML_CHUNK

    @pl.when(s_idx == 0)
    def _():
        c_ref[...] = jnp.zeros_like(c_ref)
        n_ref[...] = jnp.zeros_like(n_ref)
        m_ref[...] = jnp.zeros_like(m_ref)

    tril, same = _chunk_masks(SEQ_T, ML_CHUNK)
    tril01 = jnp.where(tril, 1.0, 0.0).astype(BF16)
    same01 = jnp.where(same, 1.0, 0.0).astype(BF16)
    emask = _expand_mask(SEQ_T, ML_CHUNK, D_HEAD)

    log_i = gi_ref[...] + bias_ref[0:1, :]
    log_f = jax.nn.log_sigmoid(gf_ref[...] + bias_ref[1:2, :])
    b_col = _dot_mask(tril01, log_f)
    e_col = _dot_mask(same01, log_f)
    w_end = e_col - b_col + log_i

    m = m_ref[...]
    m_prev_rows, m_new_rows, a_rows = [], [], []
    for c in range(n_c):
        r0 = c * ML_CHUNK
        be = e_col[r0:r0 + 1, :]
        m_end = jnp.max(w_end[r0:r0 + ML_CHUNK, :], axis=0, keepdims=True)
        m_new = jnp.maximum(be + m, m_end)
        a_rows.append(jnp.exp(be + m - m_new))
        m_prev_rows.append(jnp.broadcast_to(m, (ML_CHUNK, LANES)))
        m_new_rows.append(jnp.broadcast_to(m_new, (ML_CHUNK, LANES)))
        m = m_new
    m_ref[...] = m
    log_inter = b_col + jnp.concatenate(m_prev_rows, axis=0)
    wk_scale = jnp.exp(w_end - jnp.concatenate(m_new_rows, axis=0))
    r_rows = (b_col - log_i).T

    for h in range(N_HEADS):
        lo, hi = h * D_HEAD, (h + 1) * D_HEAD
        q = p_ref[:, lo:hi] * (D_HEAD ** -0.5)
        k = p_ref[:, GROUP_W + lo:GROUP_W + hi]
        v = p_ref[:, 2 * GROUP_W + lo:2 * GROUP_W + hi]
        og = p_ref[:, 3 * GROUP_W + lo:3 * GROUP_W + hi]
        q16, k16, v16 = q.astype(BF16), k.astype(BF16), v.astype(BF16)

        d_log = jnp.where(tril, b_col[:, h:h + 1] - r_rows[h:h + 1, :], -jnp.inf)
        linter = log_inter[:, h:h + 1]
        m_t = jnp.maximum(linter, jnp.max(d_log, axis=-1, keepdims=True))
        w = jnp.exp(d_log - m_t) * _dot_nt(q16, k16)
        a_in = jnp.exp(linter - m_t)
        num = _dot(w.astype(BF16), v16)
        den = jnp.sum(w, axis=-1, keepdims=True)

        wk = wk_scale[:, h:h + 1] * k
        v_exp = jnp.where(emask, jnp.tile(v16, (1, n_c)), jnp.zeros((), BF16))
        dc_all = _dot(wk.T.astype(BF16), v_exp)

        c_mat = c_ref[h]
        n_vec = n_ref[h]
        c_prev, n_prev = [], []
        for c in range(n_c):
            r0 = c * ML_CHUNK
            c_prev.append(c_mat)
            n_prev.append(jnp.broadcast_to(n_vec, (ML_CHUNK, D_HEAD)))
            a = a_rows[c][:, h:h + 1]
            c_mat = a * c_mat + dc_all[:, c * D_HEAD:(c + 1) * D_HEAD]
            n_vec = a * n_vec + jnp.sum(wk[r0:r0 + ML_CHUNK, :], axis=0, keepdims=True)
        c_ref[h] = c_mat
        n_ref[h] = n_vec

        q_exp = jnp.where(emask, jnp.tile(q16, (1, n_c)), jnp.zeros((), BF16))
        q_c = _dot(q_exp, jnp.concatenate(c_prev, axis=0).astype(BF16))
        q_n = jnp.sum(q * jnp.concatenate(n_prev, axis=0), axis=-1, keepdims=True)
        num = num + a_in * q_c
        den = den + a_in * q_n
        hh = num / jnp.maximum(jnp.abs(den), jnp.exp(-m_t))
        o_ref[:, lo:hi] = (_rms(hh, mln_ref[:, lo:hi]) * jax.nn.sigmoid(og)).astype(BF16)


def _mlstm(proj, bias_rows, ml_norm):
    bsz, seq = proj.shape[0], proj.shape[1]
    gate_blk = 7 * GROUP_W // LANES
    return pl.pallas_call(
        _mlstm_body,
        out_shape=jax.ShapeDtypeStruct((bsz, seq, GROUP_W), BF16),
        grid=(bsz, seq // SEQ_T),
        in_specs=[
            pl.BlockSpec((None, SEQ_T, 4 * GROUP_W), lambda b, s: (b, s, 0)),
            pl.BlockSpec((None, SEQ_T, LANES), lambda b, s: (b, s, gate_blk)),
            pl.BlockSpec((None, SEQ_T, LANES), lambda b, s: (b, s, gate_blk + 1)),
            pl.BlockSpec((2, LANES), lambda b, s: (0, 0)),
            pl.BlockSpec((1, GROUP_W), lambda b, s: (0, 0)),
        ],
        out_specs=pl.BlockSpec((None, SEQ_T, GROUP_W), lambda b, s: (b, s, 0)),
        scratch_shapes=[pltpu.VMEM((N_HEADS, D_HEAD, D_HEAD), F32),
                        pltpu.VMEM((N_HEADS, 1, D_HEAD), F32),
                        pltpu.VMEM((1, LANES), F32)],
        compiler_params=pltpu.CompilerParams(
            dimension_semantics=("arbitrary", "arbitrary"), vmem_limit_bytes=VMEM_LIMIT),
        name="mlstm",
    )(proj, proj, proj, bias_rows, ml_norm.reshape(1, GROUP_W))


def _rope(x, cos_t, sin_t):
    lane = lax.broadcasted_iota(jnp.int32, x.shape, 1)
    half = ROPE_DIM // 2
    swapped = jnp.where(lane < half, pltpu.roll(x, LANES - half, axis=1), pltpu.roll(x, half, axis=1))
    return x * cos_t + swapped * sin_t


def _moba_prep_body(q_ref, k_ref, v_ref, cos_ref, sin_ref, qo_ref, ko_ref, vt_ref, km_ref):
    cos_t, sin_t = cos_ref[...], sin_ref[...]
    for h in range(N_HEADS):
        lo, hi = h * D_HEAD, (h + 1) * D_HEAD
        q = _rope(q_ref[:, lo:hi], cos_t, sin_t) * (D_HEAD ** -0.5)
        k = _rope(k_ref[:, lo:hi], cos_t, sin_t)
        qo_ref[:, lo:hi] = q.astype(BF16)
        ko_ref[h] = k.astype(BF16)
        vt_ref[h] = v_ref[:, lo:hi].T.astype(BF16)
        km_ref[:, lo:hi] = jnp.mean(k, axis=0, keepdims=True)


def _moba_prep(proj, cos_t, sin_t):
    bsz, seq = proj.shape[0], proj.shape[1]
    n_blk = seq // MB_BLOCK
    return pl.pallas_call(
        _moba_prep_body,
        out_shape=[
            jax.ShapeDtypeStruct((bsz, seq, GROUP_W), BF16),
            jax.ShapeDtypeStruct((bsz, N_HEADS, n_blk, MB_BLOCK, D_HEAD), BF16),
            jax.ShapeDtypeStruct((bsz, N_HEADS, n_blk, D_HEAD, MB_BLOCK), BF16),
            jax.ShapeDtypeStruct((bsz, n_blk, 1, GROUP_W), F32),
        ],
        grid=(bsz, n_blk),
        in_specs=[
            pl.BlockSpec((None, MB_BLOCK, GROUP_W), lambda b, s: (b, s, 4)),
            pl.BlockSpec((None, MB_BLOCK, GROUP_W), lambda b, s: (b, s, 5)),
            pl.BlockSpec((None, MB_BLOCK, GROUP_W), lambda b, s: (b, s, 6)),
            pl.BlockSpec((MB_BLOCK, LANES), lambda b, s: (s, 0)),
            pl.BlockSpec((MB_BLOCK, LANES), lambda b, s: (s, 0)),
        ],
        out_specs=[
            pl.BlockSpec((None, MB_BLOCK, GROUP_W), lambda b, s: (b, s, 0)),
            pl.BlockSpec((None, N_HEADS, None, MB_BLOCK, D_HEAD), lambda b, s: (b, 0, s, 0, 0)),
            pl.BlockSpec((None, N_HEADS, None, D_HEAD, MB_BLOCK), lambda b, s: (b, 0, s, 0, 0)),
            pl.BlockSpec((None, None, 1, GROUP_W), lambda b, s: (b, s, 0, 0)),
        ],
        compiler_params=pltpu.CompilerParams(
            dimension_semantics=("parallel", "parallel"), vmem_limit_bytes=VMEM_LIMIT),
        name="moba_prep",
    )(proj, proj, proj, cos_t, sin_t)


def _moba_body(q_ref, k_ref, vt_ref, km_ref, o_ref, sel_ref, *, n_blk):
    own = pl.program_id(2)
    q = q_ref[...]

    km = km_ref[...]
    km_hi = km.astype(BF16)
    km_lo = (km - km_hi.astype(F32)).astype(BF16)
    gate = _dot_nt(km_hi, q) + _dot_nt(km_lo, q)
    blk = lax.broadcasted_iota(jnp.int32, gate.shape, 0)
    gate = jnp.where(blk < own, gate, -jnp.inf)
    sel = jnp.zeros(gate.shape, F32)
    for _ in range(MB_TOPK):
        mx = jnp.max(gate, axis=0, keepdims=True)
        idx = jnp.min(jnp.where(gate == mx, blk, n_blk), axis=0, keepdims=True)
        pick = blk == jnp.where(mx > -jnp.inf, idx, -1)
        sel = jnp.where(pick, 1.0, sel)
        gate = jnp.where(pick, -jnp.inf, gate)
    sel_ref[...] = sel

    kpos = lax.broadcasted_iota(jnp.int32, (MB_BLOCK, MB_BLOCK), 0)
    qpos = lax.broadcasted_iota(jnp.int32, (MB_BLOCK, MB_BLOCK), 1)
    s = jnp.where(kpos <= qpos, _dot_nt(k_ref[own], q), -jnp.inf)
    m0 = jnp.max(s, axis=0, keepdims=True)
    p = jnp.exp(s - m0)
    l0 = jnp.sum(p, axis=0, keepdims=True)
    acc0 = _dot(vt_ref[own], p.astype(BF16))

    def past_block(j, carry):
        m_i, l_i, acc = carry
        s = jnp.where(sel_ref[pl.ds(j, 1), :] > 0.0, _dot_nt(k_ref[j], q), -jnp.inf)
        m_n = jnp.maximum(m_i, jnp.max(s, axis=0, keepdims=True))
        alpha = jnp.exp(m_i - m_n)
        p = jnp.exp(s - m_n)
        l_n = alpha * l_i + jnp.sum(p, axis=0, keepdims=True)
        return m_n, l_n, alpha * acc + _dot(vt_ref[j], p.astype(BF16))

    _, l_f, acc = lax.fori_loop(0, own, past_block, (m0, l0, acc0))
    o_ref[...] = (acc / l_f).T.astype(BF16)


def _moba(q, k_blk, vt_blk, k_mean):
    bsz, seq = q.shape[0], q.shape[1]
    n_blk = seq // MB_BLOCK
    return pl.pallas_call(
        functools.partial(_moba_body, n_blk=n_blk),
        out_shape=jax.ShapeDtypeStruct((bsz, seq, GROUP_W), BF16),
        grid=(bsz, N_HEADS, n_blk),
        in_specs=[
            pl.BlockSpec((None, MB_BLOCK, D_HEAD), lambda b, h, i: (b, i, h)),
            pl.BlockSpec((None, None, n_blk, MB_BLOCK, D_HEAD), lambda b, h, i: (b, h, 0, 0, 0)),
            pl.BlockSpec((None, None, n_blk, D_HEAD, MB_BLOCK), lambda b, h, i: (b, h, 0, 0, 0)),
            pl.BlockSpec((None, n_blk, D_HEAD), lambda b, h, i: (b, 0, h)),
        ],
        out_specs=pl.BlockSpec((None, MB_BLOCK, D_HEAD), lambda b, h, i: (b, i, h)),
        scratch_shapes=[pltpu.VMEM((n_blk, MB_BLOCK), F32)],
        compiler_params=pltpu.CompilerParams(
            dimension_semantics=("parallel", "parallel", "arbitrary"), vmem_limit_bytes=VMEM_LIMIT),
        name="moba",
    )(q, k_blk, vt_blk, k_mean)


def _rope_tables(seq):
    half = ROPE_DIM // 2
    inv_freq = jnp.float32(ROPE_THETA) ** (-jnp.arange(half, dtype=F32) * 2.0 / ROPE_DIM)
    ang = jnp.arange(seq).astype(F32)[:, None] * inv_freq[None, :]
    cos, sin = jnp.cos(ang), jnp.sin(ang)
    rest = LANES - ROPE_DIM
    cos_t = jnp.concatenate([cos, cos, jnp.ones((seq, rest), F32)], axis=1)
    sin_t = jnp.concatenate([-sin, sin, jnp.zeros((seq, rest), F32)], axis=1)
    return cos_t, sin_t


def kernel(x, ffn_norm, ffn_w_in, ffn_w_out, mix_norm, ab_w_in, ab_w_out, hgrn_lb_logits, hgrn_out_norm,
           conv_w, conv_b, cd_w_in, cd_w_out, mlstm_gate_bias, mlstm_out_norm, final_norm):
    bsz, seq, d = x.shape
    depth = ffn_norm.shape[0]
    t = bsz * seq
    xt = x.reshape(t, d)
    cos_t, sin_t = _rope_tables(seq)

    for layer in range(depth):
        xt = _ffn(xt, ffn_norm[layer, 0], ffn_w_in[layer, 0].astype(BF16), ffn_w_out[layer, 0].astype(BF16),
                  final_norm, False)
        if layer % 2 == 0:
            e = layer // 2
            proj = _inproj(xt, mix_norm[layer], ab_w_in[e].astype(BF16)).reshape(bsz, seq, AB_COLS)
            ya, yb = _hgrn_conv(proj, hgrn_lb_logits, hgrn_out_norm[e], conv_w[e], conv_b[e], layer)
            w_out = ab_w_out[e]
        else:
            o = layer // 2
            w = cd_w_in[o]
            n_ml = 4 * GROUP_W
            pad = jnp.zeros((d, LANES - N_HEADS), w.dtype)
            w = jnp.concatenate([w[:, :n_ml], w[:, n_ml + 2 * N_HEADS:],
                                 w[:, n_ml:n_ml + N_HEADS], pad,
                                 w[:, n_ml + N_HEADS:n_ml + 2 * N_HEADS], pad], axis=1)
            proj = _inproj(xt, mix_norm[layer], w.astype(BF16)).reshape(bsz, seq, CD_COLS)
            bias = mlstm_gate_bias[o]
            bias_rows = jnp.pad(bias, ((0, 0), (0, LANES - N_HEADS)))
            ya = _mlstm(proj, bias_rows, mlstm_out_norm[o])
            q_r, k_blk, vt_blk, k_mean = _moba_prep(proj, cos_t, sin_t)
            yb = _moba(q_r, k_blk, vt_blk, k_mean.reshape(bsz, seq // MB_BLOCK, GROUP_W))
            w_out = cd_w_out[o]
        xt = _outproj(xt, ya.reshape(t, GROUP_W), yb.reshape(t, GROUP_W), w_out.astype(BF16))
        xt = _ffn(xt, ffn_norm[layer, 1], ffn_w_in[layer, 1].astype(BF16), ffn_w_out[layer, 1].astype(BF16),
                  final_norm, layer == depth - 1)
    return xt.reshape(bsz, seq, d)
```

```python
import functools

import jax
import jax.numpy as jnp
from jax import lax
from jax.experimental import pallas as pl
from jax.experimental.pallas import tpu as pltpu

F32 = jnp.float32
BF16 = jnp.bfloat16

D_MODEL = 1024
D_FF = 2816
GROUP_W = 512
N_HEADS = 4
D_HEAD = 128
RMS_EPS = 1e-6
HG_CHUNK = 32
ML_CHUNK = 64
MB_BLOCK = 256
MB_TOPK = 3
CONV_W = 3
ROPE_THETA = 500000.0
ROPE_DIM = D_HEAD // 4

LANES = 128
VMEM_LIMIT = 48 * 1024 * 1024

FFN_TM = 512
FFN_TF = 1408
PROJ_TM = 512
SEQ_T = 256
MB_TILE = 2 * MB_BLOCK

AB_COLS = 7 * GROUP_W
CD_COLS = 7 * GROUP_W + 2 * LANES


def _rms(x, gain):
    return x * lax.rsqrt(jnp.mean(x * x, axis=-1, keepdims=True) + RMS_EPS) * gain


def _dot(a, b):
    return jnp.dot(a, b, preferred_element_type=F32)


def _dot_nt(a, b):
    return lax.dot_general(a, b, (((1,), (1,)), ((), ())), preferred_element_type=F32)


def _dot_mask(mask01, x):
    hi = x.astype(BF16)
    r1 = x - hi.astype(F32)
    mid = r1.astype(BF16)
    lo = (r1 - mid.astype(F32)).astype(BF16)
    return _dot(mask01, hi) + _dot(mask01, mid) + _dot(mask01, lo)


def _chunk_masks(n, chunk):
    r = lax.broadcasted_iota(jnp.int32, (n, n), 0)
    c = lax.broadcasted_iota(jnp.int32, (n, n), 1)
    same = (r // chunk) == (c // chunk)
    return jnp.where(same, jnp.where(c <= r, 1, 0), 0) > 0, same


def _expand_mask(n, chunk, width):
    n_c = n // chunk
    r = lax.broadcasted_iota(jnp.int32, (n, n_c * width), 0)
    c = lax.broadcasted_iota(jnp.int32, (n, n_c * width), 1)
    return (r // chunk) == (c // width)


def _ffn_body(x_ref, g_ref, wg_ref, wu_ref, wo_ref, fg_ref, o_ref, hn_ref, acc_ref, *, n_f, final_norm):
    j = pl.program_id(1)

    @pl.when(j == 0)
    def _():
        hn_ref[...] = _rms(x_ref[...], g_ref[...]).astype(BF16)
        acc_ref[...] = jnp.zeros_like(acc_ref)

    h = hn_ref[...]
    gate = _dot(h, wg_ref[...])
    up = _dot(h, wu_ref[...])
    act = (gate * jax.nn.sigmoid(gate) * up).astype(BF16)
    acc_ref[...] += _dot(act, wo_ref[...])

    @pl.when(j == n_f - 1)
    def _():
        y = x_ref[...] + 0.5 * acc_ref[...]
        if final_norm:
            y = _rms(y, fg_ref[...])
        o_ref[...] = y


def _ffn(x, gain, w_in, w_out, final_gain, final_norm):
    t = x.shape[0]
    n_f = D_FF // FFN_TF
    return pl.pallas_call(
        functools.partial(_ffn_body, n_f=n_f, final_norm=final_norm),
        out_shape=jax.ShapeDtypeStruct((t, D_MODEL), F32),
        grid=(t // FFN_TM, n_f),
        in_specs=[
            pl.BlockSpec((FFN_TM, D_MODEL), lambda i, j: (i, 0)),
            pl.BlockSpec((1, D_MODEL), lambda i, j: (0, 0)),
            pl.BlockSpec((D_MODEL, FFN_TF), lambda i, j: (0, j)),
            pl.BlockSpec((D_MODEL, FFN_TF), lambda i, j: (0, j + D_FF // FFN_TF)),
            pl.BlockSpec((FFN_TF, D_MODEL), lambda i, j: (j, 0)),
            pl.BlockSpec((1, D_MODEL), lambda i, j: (0, 0)),
        ],
        out_specs=pl.BlockSpec((FFN_TM, D_MODEL), lambda i, j: (i, 0)),
        scratch_shapes=[pltpu.VMEM((FFN_TM, D_MODEL), BF16), pltpu.VMEM((FFN_TM, D_MODEL), F32)],
        compiler_params=pltpu.CompilerParams(
            dimension_semantics=("parallel", "arbitrary"), vmem_limit_bytes=VMEM_LIMIT),
        name="ffn",
    )(x, gain.reshape(1, D_MODEL), w_in, w_in, w_out, final_gain.reshape(1, D_MODEL))


def _inproj_body(x_ref, g_ref, w_ref, o_ref):
    o_ref[...] = _dot(_rms(x_ref[...], g_ref[...]).astype(BF16), w_ref[...])


def _inproj(x, gain, w):
    t, n = x.shape[0], w.shape[1]
    return pl.pallas_call(
        _inproj_body,
        out_shape=jax.ShapeDtypeStruct((t, n), F32),
        grid=(t // PROJ_TM,),
        in_specs=[
            pl.BlockSpec((PROJ_TM, D_MODEL), lambda i: (i, 0)),
            pl.BlockSpec((1, D_MODEL), lambda i: (0, 0)),
            pl.BlockSpec((D_MODEL, n), lambda i: (0, 0)),
        ],
        out_specs=pl.BlockSpec((PROJ_TM, n), lambda i: (i, 0)),
        compiler_params=pltpu.CompilerParams(
            dimension_semantics=("parallel",), vmem_limit_bytes=VMEM_LIMIT),
        name="inproj",
    )(x, gain.reshape(1, D_MODEL), w)


def _outproj_body(x_ref, ya_ref, yb_ref, wa_ref, wb_ref, o_ref):
    o_ref[...] = x_ref[...] + _dot(ya_ref[...], wa_ref[...]) + _dot(yb_ref[...], wb_ref[...])


def _outproj(x, ya, yb, w):
    t = x.shape[0]
    return pl.pallas_call(
        _outproj_body,
        out_shape=jax.ShapeDtypeStruct((t, D_MODEL), F32),
        grid=(t // PROJ_TM,),
        in_specs=[
            pl.BlockSpec((PROJ_TM, D_MODEL), lambda i: (i, 0)),
            pl.BlockSpec((PROJ_TM, GROUP_W), lambda i: (i, 0)),
            pl.BlockSpec((PROJ_TM, GROUP_W), lambda i: (i, 0)),
            pl.BlockSpec((GROUP_W, D_MODEL), lambda i: (0, 0)),
            pl.BlockSpec((GROUP_W, D_MODEL), lambda i: (1, 0)),
        ],
        out_specs=pl.BlockSpec((PROJ_TM, D_MODEL), lambda i: (i, 0)),
        compiler_params=pltpu.CompilerParams(
            dimension_semantics=("parallel",), vmem_limit_bytes=VMEM_LIMIT),
        name="outproj",
    )(x, ya, yb, w, w)


def _hgrn_conv_body(p_ref, lbl_ref, hgn_ref, cw_ref, cb_ref, yhg_ref, ysc_ref, st_ref, zb_ref, *, layer):
    s_idx = pl.program_id(1)
    n_c = SEQ_T // HG_CHUNK

    @pl.when(s_idx == 0)
    def _():
        st_ref[...] = jnp.zeros_like(st_ref)
        zb_ref[0:8, :] = jnp.zeros((8, GROUP_W), F32)

    lg = lbl_ref[...]
    ex = jnp.exp(lg - jnp.max(lg, axis=0, keepdims=True))
    sm = ex / jnp.sum(ex, axis=0, keepdims=True)
    lb = jnp.sum(sm[0:layer + 1, :], axis=0, keepdims=True)

    tril, same = _chunk_masks(SEQ_T, HG_CHUNK)
    tril01 = jnp.where(tril, 1.0, 0.0).astype(BF16)
    same01 = jnp.where(same, 1.0, 0.0).astype(BF16)
    emask = _expand_mask(SEQ_T, HG_CHUNK, D_HEAD)

    f = lb + (1.0 - lb) * jax.nn.sigmoid(p_ref[:, GROUP_W:2 * GROUP_W])
    logf = jnp.log(f)
    b_all = _dot_mask(tril01, logf)
    e_all = _dot_mask(same01, logf)

    for h in range(N_HEADS):
        lo, hi = h * D_HEAD, (h + 1) * D_HEAD
        q = p_ref[:, lo:hi]
        v = p_ref[:, 2 * GROUP_W + lo:2 * GROUP_W + hi]
        g = p_ref[:, 3 * GROUP_W + lo:3 * GROUP_W + hi]
        b = b_all[:, lo:hi]
        e = e_all[:, lo:hi]
        kk = 1.0 - f[:, lo:hi]
        q_dec = (q * jax.nn.sigmoid(q) * jnp.exp(b)).astype(BF16)
        k_dec = (kk * jnp.exp(-b)).astype(BF16)
        k_end = (kk * jnp.exp(e - b)).astype(BF16)
        v16 = v.astype(BF16)

        attn = jnp.where(tril, _dot_nt(q_dec, k_dec), 0.0)
        o = _dot(attn.astype(BF16), v16)

        k_exp = jnp.where(emask, jnp.tile(k_end, (1, n_c)), jnp.zeros((), BF16))
        d_all = _dot(v.T.astype(BF16), k_exp)

        st = st_ref[h]
        prev = []
        for c in range(n_c):
            prev.append(st)
            decay = jnp.exp(e[c * HG_CHUNK:c * HG_CHUNK + 1, :])
            st = decay * st + d_all[:, c * D_HEAD:(c + 1) * D_HEAD]
        st_ref[h] = st
        s_prev = jnp.concatenate(prev, axis=1).astype(BF16)
        q_exp = jnp.where(emask, jnp.tile(q_dec, (1, n_c)), jnp.zeros((), BF16))
        o = o + _dot_nt(q_exp, s_prev)

        o = _rms(o, hgn_ref[:, lo:hi]) * (g * jax.nn.sigmoid(g))
        yhg_ref[:, lo:hi] = o.astype(BF16)

    z = p_ref[:, 5 * GROUP_W:6 * GROUP_W] * p_ref[:, 6 * GROUP_W:7 * GROUP_W]
    zb_ref[8:SEQ_T + 8, :] = z
    z1 = zb_ref[7:SEQ_T + 7, :]
    z2 = zb_ref[6:SEQ_T + 6, :]
    y = cb_ref[...] + cw_ref[0:1, :] * z2 + cw_ref[1:2, :] * z1 + cw_ref[2:3, :] * z
    ysc_ref[...] = (p_ref[:, 4 * GROUP_W:5 * GROUP_W] * y).astype(BF16)
    zb_ref[0:8, :] = zb_ref[SEQ_T:SEQ_T + 8, :]


def _hgrn_conv(proj, lb_logits, hg_norm, conv_w, conv_b, layer):
    bsz, seq = proj.shape[0], proj.shape[1]
    n_l = lb_logits.shape[0]
    row = lambda b, s: (0, 0)
    return pl.pallas_call(
        functools.partial(_hgrn_conv_body, layer=layer),
        out_shape=[jax.ShapeDtypeStruct((bsz, seq, GROUP_W), BF16)] * 2,
        grid=(bsz, seq // SEQ_T),
        in_specs=[
            pl.BlockSpec((None, SEQ_T, AB_COLS), lambda b, s: (b, s, 0)),
            pl.BlockSpec((n_l, GROUP_W), row),
            pl.BlockSpec((1, GROUP_W), row),
            pl.BlockSpec((CONV_W, GROUP_W), row),
            pl.BlockSpec((1, GROUP_W), row),
        ],
        out_specs=[pl.BlockSpec((None, SEQ_T, GROUP_W), lambda b, s: (b, s, 0))] * 2,
        scratch_shapes=[pltpu.VMEM((N_HEADS, D_HEAD, D_HEAD), F32), pltpu.VMEM((SEQ_T + 8, GROUP_W), F32)],
        compiler_params=pltpu.CompilerParams(
            dimension_semantics=("arbitrary", "arbitrary"), vmem_limit_bytes=VMEM_LIMIT),
        name="hgrn_conv",
    )(proj, lb_logits, hg_norm.reshape(1, GROUP_W), conv_w, conv_b.reshape(1, GROUP_W))


def _mlstm_body(p_ref, gi_ref, gf_ref, bias_ref, mln_ref, o_ref, c_ref, n_ref, m_ref):
    s_idx = pl.program_id(1)
    n_c = SEQ_T // ML_CHUNK

    @pl.when(s_idx == 0)
    def _():
        c_ref[...] = jnp.zeros_like(c_ref)
        n_ref[...] = jnp.zeros_like(n_ref)
        m_ref[...] = jnp.zeros_like(m_ref)

    tril, same = _chunk_masks(SEQ_T, ML_CHUNK)
    tril01 = jnp.where(tril, 1.0, 0.0).astype(BF16)
    same01 = jnp.where(same, 1.0, 0.0).astype(BF16)
    emask = _expand_mask(SEQ_T, ML_CHUNK, D_HEAD)

    log_i = gi_ref[...] + bias_ref[0:1, :]
    log_f = jax.nn.log_sigmoid(gf_ref[...] + bias_ref[1:2, :])
    b_col = _dot_mask(tril01, log_f)
    e_col = _dot_mask(same01, log_f)
    w_end = e_col - b_col + log_i

    m = m_ref[...]
    m_prev_rows, m_new_rows, a_rows = [], [], []
    for c in range(n_c):
        r0 = c * ML_CHUNK
        be = e_col[r0:r0 + 1, :]
        m_end = jnp.max(w_end[r0:r0 + ML_CHUNK, :], axis=0, keepdims=True)
        m_new = jnp.maximum(be + m, m_end)
        a_rows.append(jnp.exp(be + m - m_new))
        m_prev_rows.append(jnp.broadcast_to(m, (ML_CHUNK, LANES)))
        m_new_rows.append(jnp.broadcast_to(m_new, (ML_CHUNK, LANES)))
        m = m_new
    m_ref[...] = m
    log_inter = b_col + jnp.concatenate(m_prev_rows, axis=0)
    wk_scale = jnp.exp(w_end - jnp.concatenate(m_new_rows, axis=0))
    r_rows = (b_col - log_i).T

    for h in range(N_HEADS):
        lo, hi = h * D_HEAD, (h + 1) * D_HEAD
        q = p_ref[:, lo:hi] * (D_HEAD ** -0.5)
        k = p_ref[:, GROUP_W + lo:GROUP_W + hi]
        v = p_ref[:, 2 * GROUP_W + lo:2 * GROUP_W + hi]
        og = p_ref[:, 3 * GROUP_W + lo:3 * GROUP_W + hi]
        q16, k16, v16 = q.astype(BF16), k.astype(BF16), v.astype(BF16)

        d_log = jnp.where(tril, b_col[:, h:h + 1] - r_rows[h:h + 1, :], -jnp.inf)
        linter = log_inter[:, h:h + 1]
        m_t = jnp.maximum(linter, jnp.max(d_log, axis=-1, keepdims=True))
        w = jnp.exp(d_log - m_t) * _dot_nt(q16, k16)
        a_in = jnp.exp(linter - m_t)
        num = _dot(w.astype(BF16), v16)
        den = jnp.sum(w, axis=-1, keepdims=True)

        wk = wk_scale[:, h:h + 1] * k
        v_exp = jnp.where(emask, jnp.tile(v16, (1, n_c)), jnp.zeros((), BF16))
        dc_all = _dot(wk.T.astype(BF16), v_exp)

        c_mat = c_ref[h]
        n_vec = n_ref[h]
        c_prev, n_prev = [], []
        for c in range(n_c):
            r0 = c * ML_CHUNK
            c_prev.append(c_mat)
            n_prev.append(jnp.broadcast_to(n_vec, (ML_CHUNK, D_HEAD)))
            a = a_rows[c][:, h:h + 1]
            c_mat = a * c_mat + dc_all[:, c * D_HEAD:(c + 1) * D_HEAD]
            n_vec = a * n_vec + jnp.sum(wk[r0:r0 + ML_CHUNK, :], axis=0, keepdims=True)
        c_ref[h] = c_mat
        n_ref[h] = n_vec

        q_exp = jnp.where(emask, jnp.tile(q16, (1, n_c)), jnp.zeros((), BF16))
        q_c = _dot(q_exp, jnp.concatenate(c_prev, axis=0).astype(BF16))
        q_n = jnp.sum(q * jnp.concatenate(n_prev, axis=0), axis=-1, keepdims=True)
        num = num + a_in * q_c
        den = den + a_in * q_n
        hh = num / jnp.maximum(jnp.abs(den), jnp.exp(-m_t))
        o_ref[:, lo:hi] = (_rms(hh, mln_ref[:, lo:hi]) * jax.nn.sigmoid(og)).astype(BF16)


def _mlstm(proj, bias_rows, ml_norm):
    bsz, seq = proj.shape[0], proj.shape[1]
    gate_blk = 7 * GROUP_W // LANES
    return pl.pallas_call(
        _mlstm_body,
        out_shape=jax.ShapeDtypeStruct((bsz, seq, GROUP_W), BF16),
        grid=(bsz, seq // SEQ_T),
        in_specs=[
            pl.BlockSpec((None, SEQ_T, 4 * GROUP_W), lambda b, s: (b, s, 0)),
            pl.BlockSpec((None, SEQ_T, LANES), lambda b, s: (b, s, gate_blk)),
            pl.BlockSpec((None, SEQ_T, LANES), lambda b, s: (b, s, gate_blk + 1)),
            pl.BlockSpec((2, LANES), lambda b, s: (0, 0)),
            pl.BlockSpec((1, GROUP_W), lambda b, s: (0, 0)),
        ],
        out_specs=pl.BlockSpec((None, SEQ_T, GROUP_W), lambda b, s: (b, s, 0)),
        scratch_shapes=[pltpu.VMEM((N_HEADS, D_HEAD, D_HEAD), F32),
                        pltpu.VMEM((N_HEADS, 1, D_HEAD), F32),
                        pltpu.VMEM((1, LANES), F32)],
        compiler_params=pltpu.CompilerParams(
            dimension_semantics=("arbitrary", "arbitrary"), vmem_limit_bytes=VMEM_LIMIT),
        name="mlstm",
    )(proj, proj, proj, bias_rows, ml_norm.reshape(1, GROUP_W))


def _rope(x, cos_t, sin_t):
    lane = lax.broadcasted_iota(jnp.int32, x.shape, 1)
    half = ROPE_DIM // 2
    swapped = jnp.where(lane < half, pltpu.roll(x, LANES - half, axis=1), pltpu.roll(x, half, axis=1))
    return x * cos_t + swapped * sin_t


def _moba_prep_body(q_ref, k_ref, v_ref, cos_ref, sin_ref, qo_ref, ko_ref, vt_ref, km_ref):
    cos_t, sin_t = cos_ref[...], sin_ref[...]
    for h in range(N_HEADS):
        lo, hi = h * D_HEAD, (h + 1) * D_HEAD
        q = _rope(q_ref[:, lo:hi], cos_t, sin_t) * (D_HEAD ** -0.5)
        k = _rope(k_ref[:, lo:hi], cos_t, sin_t)
        qo_ref[:, lo:hi] = q.astype(BF16)
        ko_ref[h] = k.astype(BF16)
        vt_ref[h] = v_ref[:, lo:hi].T.astype(BF16)
        for r in range(MB_TILE // MB_BLOCK):
            km_ref[r:r + 1, lo:hi] = jnp.mean(k[r * MB_BLOCK:(r + 1) * MB_BLOCK, :], axis=0, keepdims=True)


def _moba_prep(proj, cos_t, sin_t):
    bsz, seq = proj.shape[0], proj.shape[1]
    n_tile = seq // MB_TILE
    per_tile = MB_TILE // MB_BLOCK
    return pl.pallas_call(
        _moba_prep_body,
        out_shape=[
            jax.ShapeDtypeStruct((bsz, seq, GROUP_W), BF16),
            jax.ShapeDtypeStruct((bsz, N_HEADS, n_tile, MB_TILE, D_HEAD), BF16),
            jax.ShapeDtypeStruct((bsz, N_HEADS, n_tile, D_HEAD, MB_TILE), BF16),
            jax.ShapeDtypeStruct((bsz, n_tile, per_tile, GROUP_W), F32),
        ],
        grid=(bsz, n_tile),
        in_specs=[
            pl.BlockSpec((None, MB_TILE, GROUP_W), lambda b, s: (b, s, 4)),
            pl.BlockSpec((None, MB_TILE, GROUP_W), lambda b, s: (b, s, 5)),
            pl.BlockSpec((None, MB_TILE, GROUP_W), lambda b, s: (b, s, 6)),
            pl.BlockSpec((MB_TILE, LANES), lambda b, s: (s, 0)),
            pl.BlockSpec((MB_TILE, LANES), lambda b, s: (s, 0)),
        ],
        out_specs=[
            pl.BlockSpec((None, MB_TILE, GROUP_W), lambda b, s: (b, s, 0)),
            pl.BlockSpec((None, N_HEADS, None, MB_TILE, D_HEAD), lambda b, s: (b, 0, s, 0, 0)),
            pl.BlockSpec((None, N_HEADS, None, D_HEAD, MB_TILE), lambda b, s: (b, 0, s, 0, 0)),
            pl.BlockSpec((None, None, per_tile, GROUP_W), lambda b, s: (b, s, 0, 0)),
        ],
        compiler_params=pltpu.CompilerParams(
            dimension_semantics=("parallel", "parallel"), vmem_limit_bytes=VMEM_LIMIT),
        name="moba_prep",
    )(proj, proj, proj, cos_t, sin_t)


def _moba_body(q_ref, k_ref, vt_ref, km_ref, o_ref, sel_ref, acc_ref, *, n_blk):
    t_own = pl.program_id(1)
    heads = [(h, h * D_HEAD, (h + 1) * D_HEAD) for h in range(N_HEADS)]
    bk = MB_BLOCK

    blk = lax.broadcasted_iota(jnp.int32, (n_blk, MB_TILE), 0)
    lane = lax.broadcasted_iota(jnp.int32, (n_blk, MB_TILE), 1)
    own = 2 * t_own + lane // bk
    for h, lo, hi in heads:
        q = q_ref[:, lo:hi]
        km = km_ref[:, lo:hi]
        km_hi = km.astype(BF16)
        km_lo = (km - km_hi.astype(F32)).astype(BF16)
        gate = _dot_nt(km_hi, q) + _dot_nt(km_lo, q)
        gate = jnp.where(blk < own, gate, -jnp.inf)
        sel = jnp.zeros(gate.shape, F32)
        for _ in range(MB_TOPK):
            mx = jnp.max(gate, axis=0, keepdims=True)
            idx = jnp.min(jnp.where(gate == mx, blk, n_blk), axis=0, keepdims=True)
            pick = blk == jnp.where(mx > -jnp.inf, idx, -1)
            sel = jnp.where(pick, 1.0, sel)
            gate = jnp.where(pick, -jnp.inf, gate)
        sel_ref[h] = sel

    def live_rows(h, t):
        return sel_ref[h, pl.ds(2 * t, 1), :] > 0.0, sel_ref[h, pl.ds(2 * t + 1, 1), :] > 0.0

    def tile_max(t, ms):
        out = []
        for h, lo, hi in heads:
            s = _dot_nt(k_ref[h, t], q_ref[:, lo:hi])
            live_a, live_b = live_rows(h, t)
            m_a = jnp.where(live_a, jnp.max(s[:bk], axis=0, keepdims=True), -jnp.inf)
            m_b = jnp.where(live_b, jnp.max(s[bk:], axis=0, keepdims=True), -jnp.inf)
            out.append(jnp.maximum(ms[h], jnp.maximum(m_a, m_b)))
        return tuple(out)

    m_init = tuple(jnp.full((1, MB_TILE), -jnp.inf, F32) for _ in heads)
    m_past = lax.fori_loop(0, t_own, tile_max, m_init)

    kpos = lax.broadcasted_iota(jnp.int32, (MB_TILE, MB_TILE), 0)
    qpos = lax.broadcasted_iota(jnp.int32, (MB_TILE, MB_TILE), 1)
    kb, qb = kpos // bk, qpos // bk
    causal = jnp.where(kb == qb, jnp.where(kpos <= qpos, 1.0, 0.0), 0.0)
    cross = jnp.where(kb < qb, 1.0, 0.0)
    m_row, l0 = [], []
    for h, lo, hi in heads:
        s = _dot_nt(k_ref[h, t_own], q_ref[:, lo:hi])
        vis = (causal + cross * sel_ref[h, pl.ds(2 * t_own, 1), :]) > 0.0
        m_h = jnp.maximum(m_past[h], jnp.max(jnp.where(vis, s, -jnp.inf), axis=0, keepdims=True))
        p = jnp.where(vis, jnp.exp(s - m_h), 0.0)
        m_row.append(m_h)
        l0.append(jnp.sum(p, axis=0, keepdims=True))
        acc_ref[h] = _dot(vt_ref[h, t_own], p.astype(BF16))

    def tile_accumulate(t, ls):
        out = []
        for h, lo, hi in heads:
            e = jnp.exp(_dot_nt(k_ref[h, t], q_ref[:, lo:hi]) - m_row[h])
            live_a, live_b = live_rows(h, t)
            p = jnp.concatenate([jnp.where(live_a, e[:bk], 0.0), jnp.where(live_b, e[bk:], 0.0)], axis=0)
            out.append(ls[h] + jnp.sum(p, axis=0, keepdims=True))
            acc_ref[h] += _dot(vt_ref[h, t], p.astype(BF16))
        return tuple(out)

    l_f = lax.fori_loop(0, t_own, tile_accumulate, tuple(l0))
    for h, lo, hi in heads:
        o_ref[:, lo:hi] = (acc_ref[h] / l_f[h]).T.astype(BF16)


def _moba(q, k_tiles, vt_tiles, k_mean):
    bsz, seq = q.shape[0], q.shape[1]
    n_blk, n_tile = seq // MB_BLOCK, seq // MB_TILE
    once = pl.Buffered(1)
    return pl.pallas_call(
        functools.partial(_moba_body, n_blk=n_blk),
        out_shape=jax.ShapeDtypeStruct((bsz, seq, GROUP_W), BF16),
        grid=(bsz, n_tile),
        in_specs=[
            pl.BlockSpec((None, MB_TILE, GROUP_W), lambda b, i: (b, i, 0)),
            pl.BlockSpec((None, N_HEADS, n_tile, MB_TILE, D_HEAD), lambda b, i: (b, 0, 0, 0, 0),
                         pipeline_mode=once),
            pl.BlockSpec((None, N_HEADS, n_tile, D_HEAD, MB_TILE), lambda b, i: (b, 0, 0, 0, 0),
                         pipeline_mode=once),
            pl.BlockSpec((None, n_blk, GROUP_W), lambda b, i: (b, 0, 0)),
        ],
        out_specs=pl.BlockSpec((None, MB_TILE, GROUP_W), lambda b, i: (b, i, 0)),
        scratch_shapes=[pltpu.VMEM((N_HEADS, n_blk, MB_TILE), F32),
                        pltpu.VMEM((N_HEADS, D_HEAD, MB_TILE), F32)],
        compiler_params=pltpu.CompilerParams(
            dimension_semantics=("parallel", "arbitrary"), vmem_limit_bytes=VMEM_LIMIT),
        name="moba",
    )(q, k_tiles, vt_tiles, k_mean)


def _rope_tables(seq):
    half = ROPE_DIM // 2
    inv_freq = jnp.float32(ROPE_THETA) ** (-jnp.arange(half, dtype=F32) * 2.0 / ROPE_DIM)
    ang = jnp.arange(seq).astype(F32)[:, None] * inv_freq[None, :]
    cos, sin = jnp.cos(ang), jnp.sin(ang)
    rest = LANES - ROPE_DIM
    cos_t = jnp.concatenate([cos, cos, jnp.ones((seq, rest), F32)], axis=1)
    sin_t = jnp.concatenate([-sin, sin, jnp.zeros((seq, rest), F32)], axis=1)
    return cos_t, sin_t


def kernel(x, ffn_norm, ffn_w_in, ffn_w_out, mix_norm, ab_w_in, ab_w_out, hgrn_lb_logits, hgrn_out_norm,
           conv_w, conv_b, cd_w_in, cd_w_out, mlstm_gate_bias, mlstm_out_norm, final_norm):
    bsz, seq, d = x.shape
    depth = ffn_norm.shape[0]
    t = bsz * seq
    xt = x.reshape(t, d)
    cos_t, sin_t = _rope_tables(seq)

    for layer in range(depth):
        xt = _ffn(xt, ffn_norm[layer, 0], ffn_w_in[layer, 0].astype(BF16), ffn_w_out[layer, 0].astype(BF16),
                  final_norm, False)
        if layer % 2 == 0:
            e = layer // 2
            proj = _inproj(xt, mix_norm[layer], ab_w_in[e].astype(BF16)).reshape(bsz, seq, AB_COLS)
            ya, yb = _hgrn_conv(proj, hgrn_lb_logits, hgrn_out_norm[e], conv_w[e], conv_b[e], layer)
            w_out = ab_w_out[e]
        else:
            o = layer // 2
            w = cd_w_in[o]
            n_ml = 4 * GROUP_W
            pad = jnp.zeros((d, LANES - N_HEADS), w.dtype)
            w = jnp.concatenate([w[:, :n_ml], w[:, n_ml + 2 * N_HEADS:],
                                 w[:, n_ml:n_ml + N_HEADS], pad,
                                 w[:, n_ml + N_HEADS:n_ml + 2 * N_HEADS], pad], axis=1)
            proj = _inproj(xt, mix_norm[layer], w.astype(BF16)).reshape(bsz, seq, CD_COLS)
            bias = mlstm_gate_bias[o]
            bias_rows = jnp.pad(bias, ((0, 0), (0, LANES - N_HEADS)))
            ya = _mlstm(proj, bias_rows, mlstm_out_norm[o])
            q_r, k_tiles, vt_tiles, k_mean = _moba_prep(proj, cos_t, sin_t)
            yb = _moba(q_r, k_tiles, vt_tiles, k_mean.reshape(bsz, seq // MB_BLOCK, GROUP_W))
            w_out = cd_w_out[o]
        xt = _outproj(xt, ya.reshape(t, GROUP_W), yb.reshape(t, GROUP_W), w_out.astype(BF16))
        xt = _ffn(xt, ffn_norm[layer, 1], ffn_w_in[layer, 1].astype(BF16), ffn_w_out[layer, 1].astype(BF16),
                  final_norm, layer == depth - 1)
    return xt.reshape(bsz, seq, d)
```

```python
import functools

import jax
import jax.numpy as jnp
from jax import lax
from jax.experimental import pallas as pl
from jax.experimental.pallas import tpu as pltpu

F32 = jnp.float32
BF16 = jnp.bfloat16

D_MODEL = 1024
D_FF = 2816
GROUP_W = 512
N_HEADS = 4
D_HEAD = 128
RMS_EPS = 1e-6
HG_CHUNK = 32
ML_CHUNK = 64
MB_BLOCK = 256
MB_TOPK = 3
CONV_W = 3
ROPE_THETA = 500000.0
ROPE_DIM = D_HEAD // 4

LANES = 128
VMEM_LIMIT = 48 * 1024 * 1024

FFN_TM = 512
FFN_TF = 1408
PROJ_TM = 512
SEQ_T = 256
MB_TILE = 2 * MB_BLOCK
MB_BOUND_WINDOW = 80.0
LOG2E = 1.4426950408889634

AB_COLS = 7 * GROUP_W
CD_COLS = 7 * GROUP_W + 2 * LANES


def _rms(x, gain):
    return x * lax.rsqrt(jnp.mean(x * x, axis=-1, keepdims=True) + RMS_EPS) * gain


def _dot(a, b):
    return jnp.dot(a, b, preferred_element_type=F32)


def _dot_nt(a, b):
    return lax.dot_general(a, b, (((1,), (1,)), ((), ())), preferred_element_type=F32)


def _dot_mask(mask01, x):
    hi = x.astype(BF16)
    r1 = x - hi.astype(F32)
    mid = r1.astype(BF16)
    lo = (r1 - mid.astype(F32)).astype(BF16)
    return _dot(mask01, hi) + _dot(mask01, mid) + _dot(mask01, lo)


def _chunk_masks(n, chunk):
    r = lax.broadcasted_iota(jnp.int32, (n, n), 0)
    c = lax.broadcasted_iota(jnp.int32, (n, n), 1)
    same = (r // chunk) == (c // chunk)
    return jnp.where(same, jnp.where(c <= r, 1, 0), 0) > 0, same


def _expand_mask(n, chunk, width):
    n_c = n // chunk
    r = lax.broadcasted_iota(jnp.int32, (n, n_c * width), 0)
    c = lax.broadcasted_iota(jnp.int32, (n, n_c * width), 1)
    return (r // chunk) == (c // width)


def _ffn_body(x_ref, g_ref, wg_ref, wu_ref, wo_ref, fg_ref, o_ref, hn_ref, acc_ref, *, n_f, final_norm):
    j = pl.program_id(1)

    @pl.when(j == 0)
    def _():
        hn_ref[...] = _rms(x_ref[...], g_ref[...]).astype(BF16)
        acc_ref[...] = jnp.zeros_like(acc_ref)

    h = hn_ref[...]
    gate = _dot(h, wg_ref[...])
    up = _dot(h, wu_ref[...])
    act = (gate * jax.nn.sigmoid(gate) * up).astype(BF16)
    acc_ref[...] += _dot(act, wo_ref[...])

    @pl.when(j == n_f - 1)
    def _():
        y = x_ref[...] + 0.5 * acc_ref[...]
        if final_norm:
            y = _rms(y, fg_ref[...])
        o_ref[...] = y


def _ffn(x, gain, w_in, w_out, final_gain, final_norm):
    t = x.shape[0]
    n_f = D_FF // FFN_TF
    return pl.pallas_call(
        functools.partial(_ffn_body, n_f=n_f, final_norm=final_norm),
        out_shape=jax.ShapeDtypeStruct((t, D_MODEL), F32),
        grid=(t // FFN_TM, n_f),
        in_specs=[
            pl.BlockSpec((FFN_TM, D_MODEL), lambda i, j: (i, 0)),
            pl.BlockSpec((1, D_MODEL), lambda i, j: (0, 0)),
            pl.BlockSpec((D_MODEL, FFN_TF), lambda i, j: (0, j)),
            pl.BlockSpec((D_MODEL, FFN_TF), lambda i, j: (0, j + D_FF // FFN_TF)),
            pl.BlockSpec((FFN_TF, D_MODEL), lambda i, j: (j, 0)),
            pl.BlockSpec((1, D_MODEL), lambda i, j: (0, 0)),
        ],
        out_specs=pl.BlockSpec((FFN_TM, D_MODEL), lambda i, j: (i, 0)),
        scratch_shapes=[pltpu.VMEM((FFN_TM, D_MODEL), BF16), pltpu.VMEM((FFN_TM, D_MODEL), F32)],
        compiler_params=pltpu.CompilerParams(
            dimension_semantics=("parallel", "arbitrary"), vmem_limit_bytes=VMEM_LIMIT),
        name="ffn",
    )(x, gain.reshape(1, D_MODEL), w_in, w_in, w_out, final_gain.reshape(1, D_MODEL))


def _inproj_body(x_ref, g_ref, w_ref, o_ref):
    o_ref[...] = _dot(_rms(x_ref[...], g_ref[...]).astype(BF16), w_ref[...])


def _inproj(x, gain, w):
    t, n = x.shape[0], w.shape[1]
    return pl.pallas_call(
        _inproj_body,
        out_shape=jax.ShapeDtypeStruct((t, n), F32),
        grid=(t // PROJ_TM,),
        in_specs=[
            pl.BlockSpec((PROJ_TM, D_MODEL), lambda i: (i, 0)),
            pl.BlockSpec((1, D_MODEL), lambda i: (0, 0)),
            pl.BlockSpec((D_MODEL, n), lambda i: (0, 0)),
        ],
        out_specs=pl.BlockSpec((PROJ_TM, n), lambda i: (i, 0)),
        compiler_params=pltpu.CompilerParams(
            dimension_semantics=("parallel",), vmem_limit_bytes=VMEM_LIMIT),
        name="inproj",
    )(x, gain.reshape(1, D_MODEL), w)


def _outproj_body(x_ref, ya_ref, yb_ref, wa_ref, wb_ref, o_ref):
    o_ref[...] = x_ref[...] + _dot(ya_ref[...], wa_ref[...]) + _dot(yb_ref[...], wb_ref[...])


def _outproj(x, ya, yb, w):
    t = x.shape[0]
    return pl.pallas_call(
        _outproj_body,
        out_shape=jax.ShapeDtypeStruct((t, D_MODEL), F32),
        grid=(t // PROJ_TM,),
        in_specs=[
            pl.BlockSpec((PROJ_TM, D_MODEL), lambda i: (i, 0)),
            pl.BlockSpec((PROJ_TM, GROUP_W), lambda i: (i, 0)),
            pl.BlockSpec((PROJ_TM, GROUP_W), lambda i: (i, 0)),
            pl.BlockSpec((GROUP_W, D_MODEL), lambda i: (0, 0)),
            pl.BlockSpec((GROUP_W, D_MODEL), lambda i: (1, 0)),
        ],
        out_specs=pl.BlockSpec((PROJ_TM, D_MODEL), lambda i: (i, 0)),
        compiler_params=pltpu.CompilerParams(
            dimension_semantics=("parallel",), vmem_limit_bytes=VMEM_LIMIT),
        name="outproj",
    )(x, ya, yb, w, w)


def _hgrn_conv_body(p_ref, lbl_ref, hgn_ref, cw_ref, cb_ref, yhg_ref, ysc_ref, st_ref, zb_ref, *, layer):
    s_idx = pl.program_id(1)
    n_c = SEQ_T // HG_CHUNK

    @pl.when(s_idx == 0)
    def _():
        st_ref[...] = jnp.zeros_like(st_ref)
        zb_ref[0:8, :] = jnp.zeros((8, GROUP_W), F32)

    lg = lbl_ref[...]
    ex = jnp.exp(lg - jnp.max(lg, axis=0, keepdims=True))
    sm = ex / jnp.sum(ex, axis=0, keepdims=True)
    lb = jnp.sum(sm[0:layer + 1, :], axis=0, keepdims=True)

    tril, same = _chunk_masks(SEQ_T, HG_CHUNK)
    tril01 = jnp.where(tril, 1.0, 0.0).astype(BF16)
    same01 = jnp.where(same, 1.0, 0.0).astype(BF16)
    emask = _expand_mask(SEQ_T, HG_CHUNK, D_HEAD)

    f = lb + (1.0 - lb) * jax.nn.sigmoid(p_ref[:, GROUP_W:2 * GROUP_W])
    logf = jnp.log(f)
    b_all = _dot_mask(tril01, logf)
    e_all = _dot_mask(same01, logf)

    for h in range(N_HEADS):
        lo, hi = h * D_HEAD, (h + 1) * D_HEAD
        q = p_ref[:, lo:hi]
        v = p_ref[:, 2 * GROUP_W + lo:2 * GROUP_W + hi]
        g = p_ref[:, 3 * GROUP_W + lo:3 * GROUP_W + hi]
        b = b_all[:, lo:hi]
        e = e_all[:, lo:hi]
        kk = 1.0 - f[:, lo:hi]
        q_dec = (q * jax.nn.sigmoid(q) * jnp.exp(b)).astype(BF16)
        k_dec = (kk * jnp.exp(-b)).astype(BF16)
        k_end = (kk * jnp.exp(e - b)).astype(BF16)
        v16 = v.astype(BF16)

        attn = jnp.where(tril, _dot_nt(q_dec, k_dec), 0.0)
        o = _dot(attn.astype(BF16), v16)

        k_exp = jnp.where(emask, jnp.tile(k_end, (1, n_c)), jnp.zeros((), BF16))
        d_all = _dot(v.T.astype(BF16), k_exp)

        st = st_ref[h]
        prev = []
        for c in range(n_c):
            prev.append(st)
            decay = jnp.exp(e[c * HG_CHUNK:c * HG_CHUNK + 1, :])
            st = decay * st + d_all[:, c * D_HEAD:(c + 1) * D_HEAD]
        st_ref[h] = st
        s_prev = jnp.concatenate(prev, axis=1).astype(BF16)
        q_exp = jnp.where(emask, jnp.tile(q_dec, (1, n_c)), jnp.zeros((), BF16))
        o = o + _dot_nt(q_exp, s_prev)

        o = _rms(o, hgn_ref[:, lo:hi]) * (g * jax.nn.sigmoid(g))
        yhg_ref[:, lo:hi] = o.astype(BF16)

    z = p_ref[:, 5 * GROUP_W:6 * GROUP_W] * p_ref[:, 6 * GROUP_W:7 * GROUP_W]
    zb_ref[8:SEQ_T + 8, :] = z
    z1 = zb_ref[7:SEQ_T + 7, :]
    z2 = zb_ref[6:SEQ_T + 6, :]
    y = cb_ref[...] + cw_ref[0:1, :] * z2 + cw_ref[1:2, :] * z1 + cw_ref[2:3, :] * z
    ysc_ref[...] = (p_ref[:, 4 * GROUP_W:5 * GROUP_W] * y).astype(BF16)
    zb_ref[0:8, :] = zb_ref[SEQ_T:SEQ_T + 8, :]


def _hgrn_conv(proj, lb_logits, hg_norm, conv_w, conv_b, layer):
    bsz, seq = proj.shape[0], proj.shape[1]
    n_l = lb_logits.shape[0]
    row = lambda b, s: (0, 0)
    return pl.pallas_call(
        functools.partial(_hgrn_conv_body, layer=layer),
        out_shape=[jax.ShapeDtypeStruct((bsz, seq, GROUP_W), BF16)] * 2,
        grid=(bsz, seq // SEQ_T),
        in_specs=[
            pl.BlockSpec((None, SEQ_T, AB_COLS), lambda b, s: (b, s, 0)),
            pl.BlockSpec((n_l, GROUP_W), row),
            pl.BlockSpec((1, GROUP_W), row),
            pl.BlockSpec((CONV_W, GROUP_W), row),
            pl.BlockSpec((1, GROUP_W), row),
        ],
        out_specs=[pl.BlockSpec((None, SEQ_T, GROUP_W), lambda b, s: (b, s, 0))] * 2,
        scratch_shapes=[pltpu.VMEM((N_HEADS, D_HEAD, D_HEAD), F32), pltpu.VMEM((SEQ_T + 8, GROUP_W), F32)],
        compiler_params=pltpu.CompilerParams(
            dimension_semantics=("arbitrary", "arbitrary"), vmem_limit_bytes=VMEM_LIMIT),
        name="hgrn_conv",
    )(proj, lb_logits, hg_norm.reshape(1, GROUP_W), conv_w, conv_b.reshape(1, GROUP_W))


def _mlstm_body(p_ref, gi_ref, gf_ref, bias_ref, mln_ref, o_ref, c_ref, n_ref, m_ref):
    s_idx = pl.program_id(1)
    n_c = SEQ_T // ML_CHUNK

    @pl.when(s_idx == 0)
    def _():
        c_ref[...] = jnp.zeros_like(c_ref)
        n_ref[...] = jnp.zeros_like(n_ref)
        m_ref[...] = jnp.zeros_like(m_ref)

    tril, same = _chunk_masks(SEQ_T, ML_CHUNK)
    tril01 = jnp.where(tril, 1.0, 0.0).astype(BF16)
    same01 = jnp.where(same, 1.0, 0.0).astype(BF16)
    emask = _expand_mask(SEQ_T, ML_CHUNK, D_HEAD)

    log_i = gi_ref[...] + bias_ref[0:1, :]
    log_f = jax.nn.log_sigmoid(gf_ref[...] + bias_ref[1:2, :])
    b_col = _dot_mask(tril01, log_f)
    e_col = _dot_mask(same01, log_f)
    w_end = e_col - b_col + log_i

    m = m_ref[...]
    m_prev_rows, m_new_rows, a_rows = [], [], []
    for c in range(n_c):
        r0 = c * ML_CHUNK
        be = e_col[r0:r0 + 1, :]
        m_end = jnp.max(w_end[r0:r0 + ML_CHUNK, :], axis=0, keepdims=True)
        m_new = jnp.maximum(be + m, m_end)
        a_rows.append(jnp.exp(be + m - m_new))
        m_prev_rows.append(jnp.broadcast_to(m, (ML_CHUNK, LANES)))
        m_new_rows.append(jnp.broadcast_to(m_new, (ML_CHUNK, LANES)))
        m = m_new
    m_ref[...] = m
    log_inter = b_col + jnp.concatenate(m_prev_rows, axis=0)
    wk_scale = jnp.exp(w_end - jnp.concatenate(m_new_rows, axis=0))
    r_rows = (b_col - log_i).T

    for h in range(N_HEADS):
        lo, hi = h * D_HEAD, (h + 1) * D_HEAD
        q = p_ref[:, lo:hi] * (D_HEAD ** -0.5)
        k = p_ref[:, GROUP_W + lo:GROUP_W + hi]
        v = p_ref[:, 2 * GROUP_W + lo:2 * GROUP_W + hi]
        og = p_ref[:, 3 * GROUP_W + lo:3 * GROUP_W + hi]
        q16, k16, v16 = q.astype(BF16), k.astype(BF16), v.astype(BF16)

        d_log = jnp.where(tril, b_col[:, h:h + 1] - r_rows[h:h + 1, :], -jnp.inf)
        linter = log_inter[:, h:h + 1]
        m_t = jnp.maximum(linter, jnp.max(d_log, axis=-1, keepdims=True))
        w = jnp.exp(d_log - m_t) * _dot_nt(q16, k16)
        a_in = jnp.exp(linter - m_t)
        num = _dot(w.astype(BF16), v16)
        den = jnp.sum(w, axis=-1, keepdims=True)

        wk = wk_scale[:, h:h + 1] * k
        v_exp = jnp.where(emask, jnp.tile(v16, (1, n_c)), jnp.zeros((), BF16))
        dc_all = _dot(wk.T.astype(BF16), v_exp)

        c_mat = c_ref[h]
        n_vec = n_ref[h]
        c_prev, n_prev = [], []
        for c in range(n_c):
            r0 = c * ML_CHUNK
            c_prev.append(c_mat)
            n_prev.append(jnp.broadcast_to(n_vec, (ML_CHUNK, D_HEAD)))
            a = a_rows[c][:, h:h + 1]
            c_mat = a * c_mat + dc_all[:, c * D_HEAD:(c + 1) * D_HEAD]
            n_vec = a * n_vec + jnp.sum(wk[r0:r0 + ML_CHUNK, :], axis=0, keepdims=True)
        c_ref[h] = c_mat
        n_ref[h] = n_vec

        q_exp = jnp.where(emask, jnp.tile(q16, (1, n_c)), jnp.zeros((), BF16))
        q_c = _dot(q_exp, jnp.concatenate(c_prev, axis=0).astype(BF16))
        q_n = jnp.sum(q * jnp.concatenate(n_prev, axis=0), axis=-1, keepdims=True)
        num = num + a_in * q_c
        den = den + a_in * q_n
        hh = num / jnp.maximum(jnp.abs(den), jnp.exp(-m_t))
        o_ref[:, lo:hi] = (_rms(hh, mln_ref[:, lo:hi]) * jax.nn.sigmoid(og)).astype(BF16)


def _mlstm(proj, bias_rows, ml_norm):
    bsz, seq = proj.shape[0], proj.shape[1]
    gate_blk = 7 * GROUP_W // LANES
    return pl.pallas_call(
        _mlstm_body,
        out_shape=jax.ShapeDtypeStruct((bsz, seq, GROUP_W), BF16),
        grid=(bsz, seq // SEQ_T),
        in_specs=[
            pl.BlockSpec((None, SEQ_T, 4 * GROUP_W), lambda b, s: (b, s, 0)),
            pl.BlockSpec((None, SEQ_T, LANES), lambda b, s: (b, s, gate_blk)),
            pl.BlockSpec((None, SEQ_T, LANES), lambda b, s: (b, s, gate_blk + 1)),
            pl.BlockSpec((2, LANES), lambda b, s: (0, 0)),
            pl.BlockSpec((1, GROUP_W), lambda b, s: (0, 0)),
        ],
        out_specs=pl.BlockSpec((None, SEQ_T, GROUP_W), lambda b, s: (b, s, 0)),
        scratch_shapes=[pltpu.VMEM((N_HEADS, D_HEAD, D_HEAD), F32),
                        pltpu.VMEM((N_HEADS, 1, D_HEAD), F32),
                        pltpu.VMEM((1, LANES), F32)],
        compiler_params=pltpu.CompilerParams(
            dimension_semantics=("arbitrary", "arbitrary"), vmem_limit_bytes=VMEM_LIMIT),
        name="mlstm",
    )(proj, proj, proj, bias_rows, ml_norm.reshape(1, GROUP_W))


def _rope(x, cos_t, sin_t):
    lane = lax.broadcasted_iota(jnp.int32, x.shape, 1)
    half = ROPE_DIM // 2
    swapped = jnp.where(lane < half, pltpu.roll(x, LANES - half, axis=1), pltpu.roll(x, half, axis=1))
    return x * cos_t + swapped * sin_t


def _moba_prep_body(q_ref, k_ref, v_ref, cos_ref, sin_ref, qo_ref, ko_ref, vt_ref, km_ref, ka_ref):
    cos_t, sin_t = cos_ref[...], sin_ref[...]
    for h in range(N_HEADS):
        lo, hi = h * D_HEAD, (h + 1) * D_HEAD
        q = _rope(q_ref[:, lo:hi], cos_t, sin_t) * (D_HEAD ** -0.5 * LOG2E)
        k = _rope(k_ref[:, lo:hi], cos_t, sin_t)
        qo_ref[:, lo:hi] = q.astype(BF16)
        ko_ref[h] = k.astype(BF16)
        vt_ref[h] = v_ref[:, lo:hi].T.astype(BF16)
        for r in range(MB_TILE // MB_BLOCK):
            k_blk = k[r * MB_BLOCK:(r + 1) * MB_BLOCK, :]
            km_ref[r:r + 1, lo:hi] = jnp.mean(k_blk, axis=0, keepdims=True)
            ka_ref[r:r + 1, lo:hi] = jnp.max(jnp.abs(k_blk), axis=0, keepdims=True)


def _moba_prep(proj, cos_t, sin_t):
    bsz, seq = proj.shape[0], proj.shape[1]
    n_tile = seq // MB_TILE
    per_tile = MB_TILE // MB_BLOCK
    return pl.pallas_call(
        _moba_prep_body,
        out_shape=[
            jax.ShapeDtypeStruct((bsz, seq, GROUP_W), BF16),
            jax.ShapeDtypeStruct((bsz, N_HEADS, n_tile, MB_TILE, D_HEAD), BF16),
            jax.ShapeDtypeStruct((bsz, N_HEADS, n_tile, D_HEAD, MB_TILE), BF16),
            jax.ShapeDtypeStruct((bsz, n_tile, per_tile, GROUP_W), F32),
            jax.ShapeDtypeStruct((bsz, n_tile, per_tile, GROUP_W), F32),
        ],
        grid=(bsz, n_tile),
        in_specs=[
            pl.BlockSpec((None, MB_TILE, GROUP_W), lambda b, s: (b, s, 4)),
            pl.BlockSpec((None, MB_TILE, GROUP_W), lambda b, s: (b, s, 5)),
            pl.BlockSpec((None, MB_TILE, GROUP_W), lambda b, s: (b, s, 6)),
            pl.BlockSpec((MB_TILE, LANES), lambda b, s: (s, 0)),
            pl.BlockSpec((MB_TILE, LANES), lambda b, s: (s, 0)),
        ],
        out_specs=[
            pl.BlockSpec((None, MB_TILE, GROUP_W), lambda b, s: (b, s, 0)),
            pl.BlockSpec((None, N_HEADS, None, MB_TILE, D_HEAD), lambda b, s: (b, 0, s, 0, 0)),
            pl.BlockSpec((None, N_HEADS, None, D_HEAD, MB_TILE), lambda b, s: (b, 0, s, 0, 0)),
            pl.BlockSpec((None, None, per_tile, GROUP_W), lambda b, s: (b, s, 0, 0)),
            pl.BlockSpec((None, None, per_tile, GROUP_W), lambda b, s: (b, s, 0, 0)),
        ],
        compiler_params=pltpu.CompilerParams(
            dimension_semantics=("parallel", "parallel"), vmem_limit_bytes=VMEM_LIMIT),
        name="moba_prep",
    )(proj, proj, proj, cos_t, sin_t)


def _moba_body(q_ref, k_ref, vt_ref, km_ref, ka_ref, o_ref, sel_ref, sd_ref, acc_ref, *, n_blk):
    t_own = pl.program_id(1)
    heads = [(h, h * D_HEAD, (h + 1) * D_HEAD) for h in range(N_HEADS)]
    bk = MB_BLOCK

    blk = lax.broadcasted_iota(jnp.int32, (n_blk, MB_TILE), 0)
    lane = lax.broadcasted_iota(jnp.int32, (n_blk, MB_TILE), 1)
    own = 2 * t_own + lane // bk
    bound_past = []
    for h, lo, hi in heads:
        q = q_ref[:, lo:hi]
        km = km_ref[:, lo:hi]
        km_hi = km.astype(BF16)
        km_lo = (km - km_hi.astype(F32)).astype(BF16)
        gate = _dot_nt(km_hi, q) + _dot_nt(km_lo, q)
        gate = jnp.where(blk < own, gate, -jnp.inf)
        sel = jnp.zeros(gate.shape, F32)
        for _ in range(MB_TOPK):
            mx = jnp.max(gate, axis=0, keepdims=True)
            idx = jnp.min(jnp.where(gate == mx, blk, n_blk), axis=0, keepdims=True)
            pick = blk == jnp.where(mx > -jnp.inf, idx, -1)
            sel = jnp.where(pick, 1.0, sel)
            gate = jnp.where(pick, -jnp.inf, gate)
        sel_ref[h] = sel
        bound = _dot_nt(ka_ref[:, lo:hi].astype(BF16), jnp.abs(q)) * (1.0 + 2.0 ** -6)
        bound_past.append(jnp.max(jnp.where(sel > 0.0, bound, -jnp.inf), axis=0, keepdims=True))

    def live_rows(h, t):
        return sel_ref[h, pl.ds(2 * t, 1), :] > 0.0, sel_ref[h, pl.ds(2 * t + 1, 1), :] > 0.0

    kpos = lax.broadcasted_iota(jnp.int32, (MB_TILE, MB_TILE), 0)
    qpos = lax.broadcasted_iota(jnp.int32, (MB_TILE, MB_TILE), 1)
    kb, qb = kpos // bk, qpos // bk
    causal = jnp.where(kb == qb, jnp.where(kpos <= qpos, 1.0, 0.0), 0.0)
    cross = jnp.where(kb < qb, 1.0, 0.0)

    def visible(h):
        return (causal + cross * sel_ref[h, pl.ds(2 * t_own, 1), :]) > 0.0

    m_diag = []
    for h, lo, hi in heads:
        s = _dot_nt(k_ref[h, t_own], q_ref[:, lo:hi])
        sd_ref[h] = s
        m_diag.append(jnp.max(jnp.where(visible(h), s, -jnp.inf), axis=0, keepdims=True))

    def exact_past_max():
        def tile_max(t, ms):
            out = []
            for h, lo, hi in heads:
                s = _dot_nt(k_ref[h, t], q_ref[:, lo:hi])
                live_a, live_b = live_rows(h, t)
                m_a = jnp.where(live_a, jnp.max(s[:bk], axis=0, keepdims=True), -jnp.inf)
                m_b = jnp.where(live_b, jnp.max(s[bk:], axis=0, keepdims=True), -jnp.inf)
                out.append(jnp.maximum(ms[h], jnp.maximum(m_a, m_b)))
            return tuple(out)

        return lax.fori_loop(0, t_own, tile_max, tuple(jnp.full((1, MB_TILE), -jnp.inf, F32) for _ in heads))

    slack = functools.reduce(jnp.maximum, [jnp.max(bound_past[h] - m_diag[h]) for h, _, _ in heads])
    m_past = lax.cond(slack > MB_BOUND_WINDOW, exact_past_max, lambda: tuple(bound_past))

    m_row, l0 = [], []
    for h, lo, hi in heads:
        m_h = jnp.maximum(m_diag[h], m_past[h])
        p = jnp.where(visible(h), jnp.exp2(sd_ref[h] - m_h), 0.0)
        m_row.append(m_h)
        l0.append(jnp.sum(p, axis=0, keepdims=True))
        acc_ref[h] = _dot(vt_ref[h, t_own], p.astype(BF16))

    def tile_accumulate(t, ls):
        out = []
        for h, lo, hi in heads:
            e = jnp.exp2(_dot_nt(k_ref[h, t], q_ref[:, lo:hi]) - m_row[h])
            live_a, live_b = live_rows(h, t)
            p = jnp.concatenate([jnp.where(live_a, e[:bk], 0.0), jnp.where(live_b, e[bk:], 0.0)], axis=0)
            out.append(ls[h] + jnp.sum(p, axis=0, keepdims=True))
            acc_ref[h] += _dot(vt_ref[h, t], p.astype(BF16))
        return tuple(out)

    l_f = lax.fori_loop(0, t_own, tile_accumulate, tuple(l0))
    for h, lo, hi in heads:
        o_ref[:, lo:hi] = (acc_ref[h] / l_f[h]).T.astype(BF16)


def _moba(q, k_tiles, vt_tiles, k_mean, k_absmax):
    bsz, seq = q.shape[0], q.shape[1]
    n_blk, n_tile = seq // MB_BLOCK, seq // MB_TILE
    once = pl.Buffered(1)
    return pl.pallas_call(
        functools.partial(_moba_body, n_blk=n_blk),
        out_shape=jax.ShapeDtypeStruct((bsz, seq, GROUP_W), BF16),
        grid=(bsz, n_tile),
        in_specs=[
            pl.BlockSpec((None, MB_TILE, GROUP_W), lambda b, i: (b, i, 0)),
            pl.BlockSpec((None, N_HEADS, n_tile, MB_TILE, D_HEAD), lambda b, i: (b, 0, 0, 0, 0),
                         pipeline_mode=once),
            pl.BlockSpec((None, N_HEADS, n_tile, D_HEAD, MB_TILE), lambda b, i: (b, 0, 0, 0, 0),
                         pipeline_mode=once),
            pl.BlockSpec((None, n_blk, GROUP_W), lambda b, i: (b, 0, 0)),
            pl.BlockSpec((None, n_blk, GROUP_W), lambda b, i: (b, 0, 0)),
        ],
        out_specs=pl.BlockSpec((None, MB_TILE, GROUP_W), lambda b, i: (b, i, 0)),
        scratch_shapes=[pltpu.VMEM((N_HEADS, n_blk, MB_TILE), F32),
                        pltpu.VMEM((N_HEADS, MB_TILE, MB_TILE), F32),
                        pltpu.VMEM((N_HEADS, D_HEAD, MB_TILE), F32)],
        compiler_params=pltpu.CompilerParams(
            dimension_semantics=("parallel", "arbitrary"), vmem_limit_bytes=VMEM_LIMIT),
        name="moba",
    )(q, k_tiles, vt_tiles, k_mean, k_absmax)


def _rope_tables(seq):
    half = ROPE_DIM // 2
    inv_freq = jnp.float32(ROPE_THETA) ** (-jnp.arange(half, dtype=F32) * 2.0 / ROPE_DIM)
    ang = jnp.arange(seq).astype(F32)[:, None] * inv_freq[None, :]
    cos, sin = jnp.cos(ang), jnp.sin(ang)
    rest = LANES - ROPE_DIM
    cos_t = jnp.concatenate([cos, cos, jnp.ones((seq, rest), F32)], axis=1)
    sin_t = jnp.concatenate([-sin, sin, jnp.zeros((seq, rest), F32)], axis=1)
    return cos_t, sin_t


def kernel(x, ffn_norm, ffn_w_in, ffn_w_out, mix_norm, ab_w_in, ab_w_out, hgrn_lb_logits, hgrn_out_norm,
           conv_w, conv_b, cd_w_in, cd_w_out, mlstm_gate_bias, mlstm_out_norm, final_norm):
    bsz, seq, d = x.shape
    depth = ffn_norm.shape[0]
    t = bsz * seq
    xt = x.reshape(t, d)
    cos_t, sin_t = _rope_tables(seq)

    for layer in range(depth):
        xt = _ffn(xt, ffn_norm[layer, 0], ffn_w_in[layer, 0].astype(BF16), ffn_w_out[layer, 0].astype(BF16),
                  final_norm, False)
        if layer % 2 == 0:
            e = layer // 2
            proj = _inproj(xt, mix_norm[layer], ab_w_in[e].astype(BF16)).reshape(bsz, seq, AB_COLS)
            ya, yb = _hgrn_conv(proj, hgrn_lb_logits, hgrn_out_norm[e], conv_w[e], conv_b[e], layer)
            w_out = ab_w_out[e]
        else:
            o = layer // 2
            w = cd_w_in[o]
            n_ml = 4 * GROUP_W
            pad = jnp.zeros((d, LANES - N_HEADS), w.dtype)
            w = jnp.concatenate([w[:, :n_ml], w[:, n_ml + 2 * N_HEADS:],
                                 w[:, n_ml:n_ml + N_HEADS], pad,
                                 w[:, n_ml + N_HEADS:n_ml + 2 * N_HEADS], pad], axis=1)
            proj = _inproj(xt, mix_norm[layer], w.astype(BF16)).reshape(bsz, seq, CD_COLS)
            bias = mlstm_gate_bias[o]
            bias_rows = jnp.pad(bias, ((0, 0), (0, LANES - N_HEADS)))
            ya = _mlstm(proj, bias_rows, mlstm_out_norm[o])
            q_r, k_tiles, vt_tiles, k_mean, k_absmax = _moba_prep(proj, cos_t, sin_t)
            n_blk = seq // MB_BLOCK
            yb = _moba(q_r, k_tiles, vt_tiles, k_mean.reshape(bsz, n_blk, GROUP_W),
                       k_absmax.reshape(bsz, n_blk, GROUP_W))
            w_out = cd_w_out[o]
        xt = _outproj(xt, ya.reshape(t, GROUP_W), yb.reshape(t, GROUP_W), w_out.astype(BF16))
        xt = _ffn(xt, ffn_norm[layer, 1], ffn_w_in[layer, 1].astype(BF16), ffn_w_out[layer, 1].astype(BF16),
                  final_norm, layer == depth - 1)
    return xt.reshape(bsz, seq, d)
```

```python
import functools

import jax
import jax.numpy as jnp
from jax import lax
from jax.experimental import pallas as pl
from jax.experimental.pallas import tpu as pltpu

F32 = jnp.float32
BF16 = jnp.bfloat16

D_MODEL = 1024
D_FF = 2816
GROUP_W = 512
N_HEADS = 4
D_HEAD = 128
RMS_EPS = 1e-6
HG_CHUNK = 32
ML_CHUNK = 64
MB_BLOCK = 256
MB_TOPK = 3
CONV_W = 3
ROPE_THETA = 500000.0
ROPE_DIM = D_HEAD // 4

LANES = 128
VMEM_LIMIT = 48 * 1024 * 1024

FFN_TM = 512
FFN_CH = 256
PROJ_TM = 512
SEQ_T = 256
MB_TILE = 2 * MB_BLOCK
MB_BOUND_WINDOW = 80.0
LOG2E = 1.4426950408889634

AB_COLS = 7 * GROUP_W
CD_COLS = 7 * GROUP_W + 2 * LANES


def _rms(x, gain):
    return x * lax.rsqrt(jnp.mean(x * x, axis=-1, keepdims=True) + RMS_EPS) * gain


def _dot(a, b):
    return jnp.dot(a, b, preferred_element_type=F32)


def _dot_nt(a, b):
    return lax.dot_general(a, b, (((1,), (1,)), ((), ())), preferred_element_type=F32)


def _dot_mask(mask01, x):
    hi = x.astype(BF16)
    r1 = x - hi.astype(F32)
    mid = r1.astype(BF16)
    lo = (r1 - mid.astype(F32)).astype(BF16)
    return _dot(mask01, hi) + _dot(mask01, mid) + _dot(mask01, lo)


def _chunk_masks(n, chunk):
    r = lax.broadcasted_iota(jnp.int32, (n, n), 0)
    c = lax.broadcasted_iota(jnp.int32, (n, n), 1)
    same = (r // chunk) == (c // chunk)
    return jnp.where(same, jnp.where(c <= r, 1, 0), 0) > 0, same


def _expand_mask(n, chunk, width):
    n_c = n // chunk
    r = lax.broadcasted_iota(jnp.int32, (n, n_c * width), 0)
    c = lax.broadcasted_iota(jnp.int32, (n, n_c * width), 1)
    return (r // chunk) == (c // width)


def _ffn_body(x_ref, g_ref, wg_ref, wu_ref, wo_ref, fg_ref, o_ref, act_ref, *, final_norm):
    h = _rms(x_ref[...], g_ref[...]).astype(BF16)
    for c0 in range(0, D_FF, FFN_CH):
        gate = _dot(h, wg_ref[:, c0:c0 + FFN_CH])
        up = _dot(h, wu_ref[:, c0:c0 + FFN_CH])
        act_ref[:, c0:c0 + FFN_CH] = (gate * jax.nn.sigmoid(gate) * up).astype(BF16)
    y = x_ref[...] + 0.5 * _dot(act_ref[...], wo_ref[...])
    if final_norm:
        y = _rms(y, fg_ref[...])
    o_ref[...] = y


def _ffn(x, gain, w_in, w_out, final_gain, final_norm):
    t = x.shape[0]
    once = pl.Buffered(1)
    return pl.pallas_call(
        functools.partial(_ffn_body, final_norm=final_norm),
        out_shape=jax.ShapeDtypeStruct((t, D_MODEL), F32),
        grid=(t // FFN_TM,),
        in_specs=[
            pl.BlockSpec((FFN_TM, D_MODEL), lambda i: (i, 0)),
            pl.BlockSpec((1, D_MODEL), lambda i: (0, 0)),
            pl.BlockSpec((D_MODEL, D_FF), lambda i: (0, 0), pipeline_mode=once),
            pl.BlockSpec((D_MODEL, D_FF), lambda i: (0, 1), pipeline_mode=once),
            pl.BlockSpec((D_FF, D_MODEL), lambda i: (0, 0), pipeline_mode=once),
            pl.BlockSpec((1, D_MODEL), lambda i: (0, 0)),
        ],
        out_specs=pl.BlockSpec((FFN_TM, D_MODEL), lambda i: (i, 0)),
        scratch_shapes=[pltpu.VMEM((FFN_TM, D_FF), BF16)],
        compiler_params=pltpu.CompilerParams(
            dimension_semantics=("parallel",), vmem_limit_bytes=VMEM_LIMIT),
        name="ffn",
    )(x, gain.reshape(1, D_MODEL), w_in, w_in, w_out, final_gain.reshape(1, D_MODEL))


def _inproj_body(x_ref, g_ref, w_ref, o_ref):
    o_ref[...] = _dot(_rms(x_ref[...], g_ref[...]).astype(BF16), w_ref[...])


def _inproj(x, gain, w):
    t, n = x.shape[0], w.shape[1]
    return pl.pallas_call(
        _inproj_body,
        out_shape=jax.ShapeDtypeStruct((t, n), F32),
        grid=(t // PROJ_TM,),
        in_specs=[
            pl.BlockSpec((PROJ_TM, D_MODEL), lambda i: (i, 0)),
            pl.BlockSpec((1, D_MODEL), lambda i: (0, 0)),
            pl.BlockSpec((D_MODEL, n), lambda i: (0, 0)),
        ],
        out_specs=pl.BlockSpec((PROJ_TM, n), lambda i: (i, 0)),
        compiler_params=pltpu.CompilerParams(
            dimension_semantics=("parallel",), vmem_limit_bytes=VMEM_LIMIT),
        name="inproj",
    )(x, gain.reshape(1, D_MODEL), w)


def _outproj_body(x_ref, ya_ref, yb_ref, wa_ref, wb_ref, o_ref):
    o_ref[...] = x_ref[...] + _dot(ya_ref[...], wa_ref[...]) + _dot(yb_ref[...], wb_ref[...])


def _outproj(x, ya, yb, w):
    t = x.shape[0]
    return pl.pallas_call(
        _outproj_body,
        out_shape=jax.ShapeDtypeStruct((t, D_MODEL), F32),
        grid=(t // PROJ_TM,),
        in_specs=[
            pl.BlockSpec((PROJ_TM, D_MODEL), lambda i: (i, 0)),
            pl.BlockSpec((PROJ_TM, GROUP_W), lambda i: (i, 0)),
            pl.BlockSpec((PROJ_TM, GROUP_W), lambda i: (i, 0)),
            pl.BlockSpec((GROUP_W, D_MODEL), lambda i: (0, 0)),
            pl.BlockSpec((GROUP_W, D_MODEL), lambda i: (1, 0)),
        ],
        out_specs=pl.BlockSpec((PROJ_TM, D_MODEL), lambda i: (i, 0)),
        compiler_params=pltpu.CompilerParams(
            dimension_semantics=("parallel",), vmem_limit_bytes=VMEM_LIMIT),
        name="outproj",
    )(x, ya, yb, w, w)


def _hgrn_conv_body(p_ref, lbl_ref, hgn_ref, cw_ref, cb_ref, yhg_ref, ysc_ref, st_ref, zb_ref, *, layer):
    s_idx = pl.program_id(1)
    n_c = SEQ_T // HG_CHUNK

    @pl.when(s_idx == 0)
    def _():
        st_ref[...] = jnp.zeros_like(st_ref)
        zb_ref[0:8, :] = jnp.zeros((8, GROUP_W), F32)

    lg = lbl_ref[...]
    ex = jnp.exp(lg - jnp.max(lg, axis=0, keepdims=True))
    sm = ex / jnp.sum(ex, axis=0, keepdims=True)
    lb = jnp.sum(sm[0:layer + 1, :], axis=0, keepdims=True)

    tril, same = _chunk_masks(SEQ_T, HG_CHUNK)
    tril01 = jnp.where(tril, 1.0, 0.0).astype(BF16)
    same01 = jnp.where(same, 1.0, 0.0).astype(BF16)
    emask = _expand_mask(SEQ_T, HG_CHUNK, D_HEAD)

    f = lb + (1.0 - lb) * jax.nn.sigmoid(p_ref[:, GROUP_W:2 * GROUP_W])
    logf = jnp.log(f)
    b_all = _dot_mask(tril01, logf)
    e_all = _dot_mask(same01, logf)

    for h in range(N_HEADS):
        lo, hi = h * D_HEAD, (h + 1) * D_HEAD
        q = p_ref[:, lo:hi]
        v = p_ref[:, 2 * GROUP_W + lo:2 * GROUP_W + hi]
        g = p_ref[:, 3 * GROUP_W + lo:3 * GROUP_W + hi]
        b = b_all[:, lo:hi]
        e = e_all[:, lo:hi]
        kk = 1.0 - f[:, lo:hi]
        q_dec = (q * jax.nn.sigmoid(q) * jnp.exp(b)).astype(BF16)
        k_dec = (kk * jnp.exp(-b)).astype(BF16)
        k_end = (kk * jnp.exp(e - b)).astype(BF16)
        v16 = v.astype(BF16)

        attn = jnp.where(tril, _dot_nt(q_dec, k_dec), 0.0)
        o = _dot(attn.astype(BF16), v16)

        k_exp = jnp.where(emask, jnp.tile(k_end, (1, n_c)), jnp.zeros((), BF16))
        d_all = _dot(v.T.astype(BF16), k_exp)

        st = st_ref[h]
        prev = []
        for c in range(n_c):
            prev.append(st)
            decay = jnp.exp(e[c * HG_CHUNK:c * HG_CHUNK + 1, :])
            st = decay * st + d_all[:, c * D_HEAD:(c + 1) * D_HEAD]
        st_ref[h] = st
        s_prev = jnp.concatenate(prev, axis=1).astype(BF16)
        q_exp = jnp.where(emask, jnp.tile(q_dec, (1, n_c)), jnp.zeros((), BF16))
        o = o + _dot_nt(q_exp, s_prev)

        o = _rms(o, hgn_ref[:, lo:hi]) * (g * jax.nn.sigmoid(g))
        yhg_ref[:, lo:hi] = o.astype(BF16)

    z = p_ref[:, 5 * GROUP_W:6 * GROUP_W] * p_ref[:, 6 * GROUP_W:7 * GROUP_W]
    zb_ref[8:SEQ_T + 8, :] = z
    z1 = zb_ref[7:SEQ_T + 7, :]
    z2 = zb_ref[6:SEQ_T + 6, :]
    y = cb_ref[...] + cw_ref[0:1, :] * z2 + cw_ref[1:2, :] * z1 + cw_ref[2:3, :] * z
    ysc_ref[...] = (p_ref[:, 4 * GROUP_W:5 * GROUP_W] * y).astype(BF16)
    zb_ref[0:8, :] = zb_ref[SEQ_T:SEQ_T + 8, :]


def _hgrn_conv(proj, lb_logits, hg_norm, conv_w, conv_b, layer):
    bsz, seq = proj.shape[0], proj.shape[1]
    n_l = lb_logits.shape[0]
    row = lambda b, s: (0, 0)
    return pl.pallas_call(
        functools.partial(_hgrn_conv_body, layer=layer),
        out_shape=[jax.ShapeDtypeStruct((bsz, seq, GROUP_W), BF16)] * 2,
        grid=(bsz, seq // SEQ_T),
        in_specs=[
            pl.BlockSpec((None, SEQ_T, AB_COLS), lambda b, s: (b, s, 0)),
            pl.BlockSpec((n_l, GROUP_W), row),
            pl.BlockSpec((1, GROUP_W), row),
            pl.BlockSpec((CONV_W, GROUP_W), row),
            pl.BlockSpec((1, GROUP_W), row),
        ],
        out_specs=[pl.BlockSpec((None, SEQ_T, GROUP_W), lambda b, s: (b, s, 0))] * 2,
        scratch_shapes=[pltpu.VMEM((N_HEADS, D_HEAD, D_HEAD), F32), pltpu.VMEM((SEQ_T + 8, GROUP_W), F32)],
        compiler_params=pltpu.CompilerParams(
            dimension_semantics=("arbitrary", "arbitrary"), vmem_limit_bytes=VMEM_LIMIT),
        name="hgrn_conv",
    )(proj, lb_logits, hg_norm.reshape(1, GROUP_W), conv_w, conv_b.reshape(1, GROUP_W))


def _mlstm_body(p_ref, gi_ref, gf_ref, bias_ref, mln_ref, o_ref, c_ref, n_ref, m_ref):
    s_idx = pl.program_id(1)
    n_c = SEQ_T // ML_CHUNK

    @pl.when(s_idx == 0)
    def _():
        c_ref[...] = jnp.zeros_like(c_ref)
        n_ref[...] = jnp.zeros_like(n_ref)
        m_ref[...] = jnp.zeros_like(m_ref)

    tril, same = _chunk_masks(SEQ_T, ML_CHUNK)
    tril01 = jnp.where(tril, 1.0, 0.0).astype(BF16)
    same01 = jnp.where(same, 1.0, 0.0).astype(BF16)
    emask = _expand_mask(SEQ_T, ML_CHUNK, D_HEAD)

    log_i = gi_ref[...] + bias_ref[0:1, :]
    log_f = jax.nn.log_sigmoid(gf_ref[...] + bias_ref[1:2, :])
    b_col = _dot_mask(tril01, log_f)
    e_col = _dot_mask(same01, log_f)
    w_end = e_col - b_col + log_i

    m = m_ref[...]
    m_prev_rows, m_new_rows, a_rows = [], [], []
    for c in range(n_c):
        r0 = c * ML_CHUNK
        be = e_col[r0:r0 + 1, :]
        m_end = jnp.max(w_end[r0:r0 + ML_CHUNK, :], axis=0, keepdims=True)
        m_new = jnp.maximum(be + m, m_end)
        a_rows.append(jnp.exp(be + m - m_new))
        m_prev_rows.append(jnp.broadcast_to(m, (ML_CHUNK, LANES)))
        m_new_rows.append(jnp.broadcast_to(m_new, (ML_CHUNK, LANES)))
        m = m_new
    m_ref[...] = m
    log_inter = b_col + jnp.concatenate(m_prev_rows, axis=0)
    wk_scale = jnp.exp(w_end - jnp.concatenate(m_new_rows, axis=0))
    r_rows = (b_col - log_i).T

    for h in range(N_HEADS):
        lo, hi = h * D_HEAD, (h + 1) * D_HEAD
        q = p_ref[:, lo:hi] * (D_HEAD ** -0.5)
        k = p_ref[:, GROUP_W + lo:GROUP_W + hi]
        v = p_ref[:, 2 * GROUP_W + lo:2 * GROUP_W + hi]
        og = p_ref[:, 3 * GROUP_W + lo:3 * GROUP_W + hi]
        q16, k16, v16 = q.astype(BF16), k.astype(BF16), v.astype(BF16)

        d_log = jnp.where(tril, b_col[:, h:h + 1] - r_rows[h:h + 1, :], -jnp.inf)
        linter = log_inter[:, h:h + 1]
        m_t = jnp.maximum(linter, jnp.max(d_log, axis=-1, keepdims=True))
        w = jnp.exp(d_log - m_t) * _dot_nt(q16, k16)
        a_in = jnp.exp(linter - m_t)
        num = _dot(w.astype(BF16), v16)
        den = jnp.sum(w, axis=-1, keepdims=True)

        wk = wk_scale[:, h:h + 1] * k
        v_exp = jnp.where(emask, jnp.tile(v16, (1, n_c)), jnp.zeros((), BF16))
        dc_all = _dot(wk.T.astype(BF16), v_exp)

        c_mat = c_ref[h]
        n_vec = n_ref[h]
        c_prev, n_prev = [], []
        for c in range(n_c):
            r0 = c * ML_CHUNK
            c_prev.append(c_mat)
            n_prev.append(jnp.broadcast_to(n_vec, (ML_CHUNK, D_HEAD)))
            a = a_rows[c][:, h:h + 1]
            c_mat = a * c_mat + dc_all[:, c * D_HEAD:(c + 1) * D_HEAD]
            n_vec = a * n_vec + jnp.sum(wk[r0:r0 + ML_CHUNK, :], axis=0, keepdims=True)
        c_ref[h] = c_mat
        n_ref[h] = n_vec

        q_exp = jnp.where(emask, jnp.tile(q16, (1, n_c)), jnp.zeros((), BF16))
        q_c = _dot(q_exp, jnp.concatenate(c_prev, axis=0).astype(BF16))
        q_n = jnp.sum(q * jnp.concatenate(n_prev, axis=0), axis=-1, keepdims=True)
        num = num + a_in * q_c
        den = den + a_in * q_n
        hh = num / jnp.maximum(jnp.abs(den), jnp.exp(-m_t))
        o_ref[:, lo:hi] = (_rms(hh, mln_ref[:, lo:hi]) * jax.nn.sigmoid(og)).astype(BF16)


def _mlstm(proj, bias_rows, ml_norm):
    bsz, seq = proj.shape[0], proj.shape[1]
    gate_blk = 7 * GROUP_W // LANES
    return pl.pallas_call(
        _mlstm_body,
        out_shape=jax.ShapeDtypeStruct((bsz, seq, GROUP_W), BF16),
        grid=(bsz, seq // SEQ_T),
        in_specs=[
            pl.BlockSpec((None, SEQ_T, 4 * GROUP_W), lambda b, s: (b, s, 0)),
            pl.BlockSpec((None, SEQ_T, LANES), lambda b, s: (b, s, gate_blk)),
            pl.BlockSpec((None, SEQ_T, LANES), lambda b, s: (b, s, gate_blk + 1)),
            pl.BlockSpec((2, LANES), lambda b, s: (0, 0)),
            pl.BlockSpec((1, GROUP_W), lambda b, s: (0, 0)),
        ],
        out_specs=pl.BlockSpec((None, SEQ_T, GROUP_W), lambda b, s: (b, s, 0)),
        scratch_shapes=[pltpu.VMEM((N_HEADS, D_HEAD, D_HEAD), F32),
                        pltpu.VMEM((N_HEADS, 1, D_HEAD), F32),
                        pltpu.VMEM((1, LANES), F32)],
        compiler_params=pltpu.CompilerParams(
            dimension_semantics=("arbitrary", "arbitrary"), vmem_limit_bytes=VMEM_LIMIT),
        name="mlstm",
    )(proj, proj, proj, bias_rows, ml_norm.reshape(1, GROUP_W))


def _rope(x, cos_t, sin_t):
    lane = lax.broadcasted_iota(jnp.int32, x.shape, 1)
    half = ROPE_DIM // 2
    swapped = jnp.where(lane < half, pltpu.roll(x, LANES - half, axis=1), pltpu.roll(x, half, axis=1))
    return x * cos_t + swapped * sin_t


def _moba_prep_body(q_ref, k_ref, v_ref, cos_ref, sin_ref, qo_ref, ko_ref, vt_ref, km_ref, ka_ref):
    cos_t, sin_t = cos_ref[...], sin_ref[...]
    for h in range(N_HEADS):
        lo, hi = h * D_HEAD, (h + 1) * D_HEAD
        q = _rope(q_ref[:, lo:hi], cos_t, sin_t) * (D_HEAD ** -0.5 * LOG2E)
        k = _rope(k_ref[:, lo:hi], cos_t, sin_t)
        qo_ref[:, lo:hi] = q.astype(BF16)
        ko_ref[h] = k.astype(BF16)
        vt_ref[h] = v_ref[:, lo:hi].T.astype(BF16)
        for r in range(MB_TILE // MB_BLOCK):
            k_blk = k[r * MB_BLOCK:(r + 1) * MB_BLOCK, :]
            km_ref[r:r + 1, lo:hi] = jnp.mean(k_blk, axis=0, keepdims=True)
            ka_ref[r:r + 1, lo:hi] = jnp.max(jnp.abs(k_blk), axis=0, keepdims=True)


def _moba_prep(proj, cos_t, sin_t):
    bsz, seq = proj.shape[0], proj.shape[1]
    n_tile = seq // MB_TILE
    per_tile = MB_TILE // MB_BLOCK
    return pl.pallas_call(
        _moba_prep_body,
        out_shape=[
            jax.ShapeDtypeStruct((bsz, seq, GROUP_W), BF16),
            jax.ShapeDtypeStruct((bsz, N_HEADS, n_tile, MB_TILE, D_HEAD), BF16),
            jax.ShapeDtypeStruct((bsz, N_HEADS, n_tile, D_HEAD, MB_TILE), BF16),
            jax.ShapeDtypeStruct((bsz, n_tile, per_tile, GROUP_W), F32),
            jax.ShapeDtypeStruct((bsz, n_tile, per_tile, GROUP_W), F32),
        ],
        grid=(bsz, n_tile),
        in_specs=[
            pl.BlockSpec((None, MB_TILE, GROUP_W), lambda b, s: (b, s, 4)),
            pl.BlockSpec((None, MB_TILE, GROUP_W), lambda b, s: (b, s, 5)),
            pl.BlockSpec((None, MB_TILE, GROUP_W), lambda b, s: (b, s, 6)),
            pl.BlockSpec((MB_TILE, LANES), lambda b, s: (s, 0)),
            pl.BlockSpec((MB_TILE, LANES), lambda b, s: (s, 0)),
        ],
        out_specs=[
            pl.BlockSpec((None, MB_TILE, GROUP_W), lambda b, s: (b, s, 0)),
            pl.BlockSpec((None, N_HEADS, None, MB_TILE, D_HEAD), lambda b, s: (b, 0, s, 0, 0)),
            pl.BlockSpec((None, N_HEADS, None, D_HEAD, MB_TILE), lambda b, s: (b, 0, s, 0, 0)),
            pl.BlockSpec((None, None, per_tile, GROUP_W), lambda b, s: (b, s, 0, 0)),
            pl.BlockSpec((None, None, per_tile, GROUP_W), lambda b, s: (b, s, 0, 0)),
        ],
        compiler_params=pltpu.CompilerParams(
            dimension_semantics=("parallel", "parallel"), vmem_limit_bytes=VMEM_LIMIT),
        name="moba_prep",
    )(proj, proj, proj, cos_t, sin_t)


def _moba_body(q_ref, k_ref, vt_ref, km_ref, ka_ref, o_ref, sel_ref, sd_ref, acc_ref, *, n_blk):
    t_own = pl.program_id(1)
    heads = [(h, h * D_HEAD, (h + 1) * D_HEAD) for h in range(N_HEADS)]
    bk = MB_BLOCK

    blk = lax.broadcasted_iota(jnp.int32, (n_blk, MB_TILE), 0)
    lane = lax.broadcasted_iota(jnp.int32, (n_blk, MB_TILE), 1)
    own = 2 * t_own + lane // bk
    bound_past = []
    for h, lo, hi in heads:
        q = q_ref[:, lo:hi]
        km = km_ref[:, lo:hi]
        km_hi = km.astype(BF16)
        km_lo = (km - km_hi.astype(F32)).astype(BF16)
        gate = _dot_nt(km_hi, q) + _dot_nt(km_lo, q)
        gate = jnp.where(blk < own, gate, -jnp.inf)
        sel = jnp.zeros(gate.shape, F32)
        for _ in range(MB_TOPK):
            mx = jnp.max(gate, axis=0, keepdims=True)
            idx = jnp.min(jnp.where(gate == mx, blk, n_blk), axis=0, keepdims=True)
            pick = blk == jnp.where(mx > -jnp.inf, idx, -1)
            sel = jnp.where(pick, 1.0, sel)
            gate = jnp.where(pick, -jnp.inf, gate)
        sel_ref[h] = sel
        bound = _dot_nt(ka_ref[:, lo:hi].astype(BF16), jnp.abs(q)) * (1.0 + 2.0 ** -6)
        bound_past.append(jnp.max(jnp.where(sel > 0.0, bound, -jnp.inf), axis=0, keepdims=True))

    def live_rows(h, t):
        return sel_ref[h, pl.ds(2 * t, 1), :] > 0.0, sel_ref[h, pl.ds(2 * t + 1, 1), :] > 0.0

    kpos = lax.broadcasted_iota(jnp.int32, (MB_TILE, MB_TILE), 0)
    qpos = lax.broadcasted_iota(jnp.int32, (MB_TILE, MB_TILE), 1)
    kb, qb = kpos // bk, qpos // bk
    causal = jnp.where(kb == qb, jnp.where(kpos <= qpos, 1.0, 0.0), 0.0)
    cross = jnp.where(kb < qb, 1.0, 0.0)

    def visible(h):
        return (causal + cross * sel_ref[h, pl.ds(2 * t_own, 1), :]) > 0.0

    m_diag = []
    for h, lo, hi in heads:
        s = _dot_nt(k_ref[h, t_own], q_ref[:, lo:hi])
        sd_ref[h] = s
        m_diag.append(jnp.max(jnp.where(visible(h), s, -jnp.inf), axis=0, keepdims=True))

    def exact_past_max():
        def tile_max(t, ms):
            out = []
            for h, lo, hi in heads:
                s = _dot_nt(k_ref[h, t], q_ref[:, lo:hi])
                live_a, live_b = live_rows(h, t)
                m_a = jnp.where(live_a, jnp.max(s[:bk], axis=0, keepdims=True), -jnp.inf)
                m_b = jnp.where(live_b, jnp.max(s[bk:], axis=0, keepdims=True), -jnp.inf)
                out.append(jnp.maximum(ms[h], jnp.maximum(m_a, m_b)))
            return tuple(out)

        return lax.fori_loop(0, t_own, tile_max, tuple(jnp.full((1, MB_TILE), -jnp.inf, F32) for _ in heads))

    slack = functools.reduce(jnp.maximum, [jnp.max(bound_past[h] - m_diag[h]) for h, _, _ in heads])
    m_past = lax.cond(slack > MB_BOUND_WINDOW, exact_past_max, lambda: tuple(bound_past))

    m_row, l0 = [], []
    for h, lo, hi in heads:
        m_h = jnp.maximum(m_diag[h], m_past[h])
        p = jnp.where(visible(h), jnp.exp2(sd_ref[h] - m_h), 0.0)
        m_row.append(m_h)
        l0.append(jnp.sum(p, axis=0, keepdims=True))
        acc_ref[h] = _dot(vt_ref[h, t_own], p.astype(BF16))

    def tile_accumulate(t, ls):
        out = []
        for h, lo, hi in heads:
            e = jnp.exp2(_dot_nt(k_ref[h, t], q_ref[:, lo:hi]) - m_row[h])
            live_a, live_b = live_rows(h, t)
            p = jnp.concatenate([jnp.where(live_a, e[:bk], 0.0), jnp.where(live_b, e[bk:], 0.0)], axis=0)
            out.append(ls[h] + jnp.sum(p, axis=0, keepdims=True))
            acc_ref[h] += _dot(vt_ref[h, t], p.astype(BF16))
        return tuple(out)

    l_f = lax.fori_loop(0, t_own, tile_accumulate, tuple(l0))
    for h, lo, hi in heads:
        o_ref[:, lo:hi] = (acc_ref[h] / l_f[h]).T.astype(BF16)


def _moba(q, k_tiles, vt_tiles, k_mean, k_absmax):
    bsz, seq = q.shape[0], q.shape[1]
    n_blk, n_tile = seq // MB_BLOCK, seq // MB_TILE
    once = pl.Buffered(1)
    return pl.pallas_call(
        functools.partial(_moba_body, n_blk=n_blk),
        out_shape=jax.ShapeDtypeStruct((bsz, seq, GROUP_W), BF16),
        grid=(bsz, n_tile),
        in_specs=[
            pl.BlockSpec((None, MB_TILE, GROUP_W), lambda b, i: (b, i, 0)),
            pl.BlockSpec((None, N_HEADS, n_tile, MB_TILE, D_HEAD), lambda b, i: (b, 0, 0, 0, 0),
                         pipeline_mode=once),
            pl.BlockSpec((None, N_HEADS, n_tile, D_HEAD, MB_TILE), lambda b, i: (b, 0, 0, 0, 0),
                         pipeline_mode=once),
            pl.BlockSpec((None, n_blk, GROUP_W), lambda b, i: (b, 0, 0)),
            pl.BlockSpec((None, n_blk, GROUP_W), lambda b, i: (b, 0, 0)),
        ],
        out_specs=pl.BlockSpec((None, MB_TILE, GROUP_W), lambda b, i: (b, i, 0)),
        scratch_shapes=[pltpu.VMEM((N_HEADS, n_blk, MB_TILE), F32),
                        pltpu.VMEM((N_HEADS, MB_TILE, MB_TILE), F32),
                        pltpu.VMEM((N_HEADS, D_HEAD, MB_TILE), F32)],
        compiler_params=pltpu.CompilerParams(
            dimension_semantics=("parallel", "arbitrary"), vmem_limit_bytes=VMEM_LIMIT),
        name="moba",
    )(q, k_tiles, vt_tiles, k_mean, k_absmax)


def _rope_tables(seq):
    half = ROPE_DIM // 2
    inv_freq = jnp.float32(ROPE_THETA) ** (-jnp.arange(half, dtype=F32) * 2.0 / ROPE_DIM)
    ang = jnp.arange(seq).astype(F32)[:, None] * inv_freq[None, :]
    cos, sin = jnp.cos(ang), jnp.sin(ang)
    rest = LANES - ROPE_DIM
    cos_t = jnp.concatenate([cos, cos, jnp.ones((seq, rest), F32)], axis=1)
    sin_t = jnp.concatenate([-sin, sin, jnp.zeros((seq, rest), F32)], axis=1)
    return cos_t, sin_t


def kernel(x, ffn_norm, ffn_w_in, ffn_w_out, mix_norm, ab_w_in, ab_w_out, hgrn_lb_logits, hgrn_out_norm,
           conv_w, conv_b, cd_w_in, cd_w_out, mlstm_gate_bias, mlstm_out_norm, final_norm):
    bsz, seq, d = x.shape
    depth = ffn_norm.shape[0]
    t = bsz * seq
    xt = x.reshape(t, d)
    cos_t, sin_t = _rope_tables(seq)

    for layer in range(depth):
        xt = _ffn(xt, ffn_norm[layer, 0], ffn_w_in[layer, 0].astype(BF16), ffn_w_out[layer, 0].astype(BF16),
                  final_norm, False)
        if layer % 2 == 0:
            e = layer // 2
            proj = _inproj(xt, mix_norm[layer], ab_w_in[e].astype(BF16)).reshape(bsz, seq, AB_COLS)
            ya, yb = _hgrn_conv(proj, hgrn_lb_logits, hgrn_out_norm[e], conv_w[e], conv_b[e], layer)
            w_out = ab_w_out[e]
        else:
            o = layer // 2
            w = cd_w_in[o]
            n_ml = 4 * GROUP_W
            pad = jnp.zeros((d, LANES - N_HEADS), w.dtype)
            w = jnp.concatenate([w[:, :n_ml], w[:, n_ml + 2 * N_HEADS:],
                                 w[:, n_ml:n_ml + N_HEADS], pad,
                                 w[:, n_ml + N_HEADS:n_ml + 2 * N_HEADS], pad], axis=1)
            proj = _inproj(xt, mix_norm[layer], w.astype(BF16)).reshape(bsz, seq, CD_COLS)
            bias = mlstm_gate_bias[o]
            bias_rows = jnp.pad(bias, ((0, 0), (0, LANES - N_HEADS)))
            ya = _mlstm(proj, bias_rows, mlstm_out_norm[o])
            q_r, k_tiles, vt_tiles, k_mean, k_absmax = _moba_prep(proj, cos_t, sin_t)
            n_blk = seq // MB_BLOCK
            yb = _moba(q_r, k_tiles, vt_tiles, k_mean.reshape(bsz, n_blk, GROUP_W),
                       k_absmax.reshape(bsz, n_blk, GROUP_W))
            w_out = cd_w_out[o]
        xt = _outproj(xt, ya.reshape(t, GROUP_W), yb.reshape(t, GROUP_W), w_out.astype(BF16))
        xt = _ffn(xt, ffn_norm[layer, 1], ffn_w_in[layer, 1].astype(BF16), ffn_w_out[layer, 1].astype(BF16),
                  final_norm, layer == depth - 1)
    return xt.reshape(bsz, seq, d)
```

```python
import functools

import jax
import jax.numpy as jnp
from jax import lax
from jax.experimental import pallas as pl
from jax.experimental.pallas import tpu as pltpu

F32 = jnp.float32
BF16 = jnp.bfloat16

D_MODEL = 1024
D_FF = 2816
GROUP_W = 512
N_HEADS = 4
D_HEAD = 128
RMS_EPS = 1e-6
HG_CHUNK = 32
ML_CHUNK = 64
MB_BLOCK = 256
MB_TOPK = 3
CONV_W = 3
ROPE_THETA = 500000.0
ROPE_DIM = D_HEAD // 4

LANES = 128
VMEM_LIMIT = 48 * 1024 * 1024

FFN_TM = 1024
FFN_SUB = 512
FFN_CH = 256
PROJ_CH = 256
SEQ_T = 256
MB_TILE = 2 * MB_BLOCK
MB_BOUND_WINDOW = 80.0
LOG2E = 1.4426950408889634

AB_COLS = 7 * GROUP_W
CD_COLS = 7 * GROUP_W + 2 * LANES


def _rms(x, gain):
    return x * lax.rsqrt(jnp.mean(x * x, axis=-1, keepdims=True) + RMS_EPS) * gain


def _dot(a, b):
    return jnp.dot(a, b, preferred_element_type=F32)


def _dot_nt(a, b):
    return lax.dot_general(a, b, (((1,), (1,)), ((), ())), preferred_element_type=F32)


def _dot_mask(mask01, x):
    hi = x.astype(BF16)
    r1 = x - hi.astype(F32)
    mid = r1.astype(BF16)
    lo = (r1 - mid.astype(F32)).astype(BF16)
    return _dot(mask01, hi) + _dot(mask01, mid) + _dot(mask01, lo)


def _chunk_masks(n, chunk):
    r = lax.broadcasted_iota(jnp.int32, (n, n), 0)
    c = lax.broadcasted_iota(jnp.int32, (n, n), 1)
    same = (r // chunk) == (c // chunk)
    return jnp.where(same, jnp.where(c <= r, 1, 0), 0) > 0, same


def _expand_mask(n, chunk, width):
    n_c = n // chunk
    r = lax.broadcasted_iota(jnp.int32, (n, n_c * width), 0)
    c = lax.broadcasted_iota(jnp.int32, (n, n_c * width), 1)
    return (r // chunk) == (c // width)


def _ffn_body(x_ref, g_ref, wg_ref, wu_ref, wo_ref, fg_ref, o_ref, act_ref, *, final_norm):
    for r0 in range(0, FFN_TM, FFN_SUB):
        rows = pl.ds(r0, FFN_SUB)
        h = _rms(x_ref[rows, :], g_ref[...]).astype(BF16)
        for c0 in range(0, D_FF, FFN_CH):
            gate = _dot(h, wg_ref[:, c0:c0 + FFN_CH])
            up = _dot(h, wu_ref[:, c0:c0 + FFN_CH])
            act_ref[rows, c0:c0 + FFN_CH] = (gate * jax.nn.sigmoid(gate) * up).astype(BF16)
        y = x_ref[rows, :] + 0.5 * _dot(act_ref[rows, :], wo_ref[...])
        if final_norm:
            y = _rms(y, fg_ref[...])
        o_ref[rows, :] = y


def _ffn(x, gain, w_in, w_out, final_gain, final_norm):
    t = x.shape[0]
    once = pl.Buffered(1)
    return pl.pallas_call(
        functools.partial(_ffn_body, final_norm=final_norm),
        out_shape=jax.ShapeDtypeStruct((t, D_MODEL), F32),
        grid=(t // FFN_TM,),
        in_specs=[
            pl.BlockSpec((FFN_TM, D_MODEL), lambda i: (i, 0)),
            pl.BlockSpec((1, D_MODEL), lambda i: (0, 0)),
            pl.BlockSpec((D_MODEL, D_FF), lambda i: (0, 0), pipeline_mode=once),
            pl.BlockSpec((D_MODEL, D_FF), lambda i: (0, 1), pipeline_mode=once),
            pl.BlockSpec((D_FF, D_MODEL), lambda i: (0, 0), pipeline_mode=once),
            pl.BlockSpec((1, D_MODEL), lambda i: (0, 0)),
        ],
        out_specs=pl.BlockSpec((FFN_TM, D_MODEL), lambda i: (i, 0)),
        scratch_shapes=[pltpu.VMEM((FFN_TM, D_FF), BF16)],
        compiler_params=pltpu.CompilerParams(
            dimension_semantics=("parallel",), vmem_limit_bytes=VMEM_LIMIT),
        name="ffn",
    )(x, gain.reshape(1, D_MODEL), w_in, w_in, w_out, final_gain.reshape(1, D_MODEL))


def _project_in(x_ref, g_ref, w_ref, p_ref):
    h = _rms(x_ref[...], g_ref[...]).astype(BF16)
    for c0 in range(0, w_ref.shape[1], PROJ_CH):
        p_ref[:, c0:c0 + PROJ_CH] = _dot(h, w_ref[:, c0:c0 + PROJ_CH])


class _ChunkedProjection:
    def __init__(self, x_ref, g_ref, w_ref, p_ref):
        self.h = _rms(x_ref[...], g_ref[...]).astype(BF16)
        self.w_ref, self.p_ref, self.c0 = w_ref, p_ref, 0

    def emit(self, n_chunks):
        n_cols = self.w_ref.shape[1]
        while self.c0 < n_cols and (n_chunks is None or n_chunks > 0):
            c0 = self.c0
            self.p_ref[:, c0:c0 + PROJ_CH] = _dot(self.h, self.w_ref[:, c0:c0 + PROJ_CH])
            self.c0 += PROJ_CH
            n_chunks = None if n_chunks is None else n_chunks - 1


def _with_pipelined_projection(x_ref, xn_ref, g_ref, w_ref, pa_ref, pb_ref, tile_fn):
    s_idx = pl.program_id(1)

    @pl.when(s_idx == 0)
    def _():
        _project_in(x_ref, g_ref, w_ref, pa_ref)

    def branch(parity, p_cur, p_nxt):
        @pl.when(s_idx % 2 == parity)
        def _():
            ahead = _ChunkedProjection(xn_ref, g_ref, w_ref, p_nxt)
            tile_fn(p_cur, ahead)
            ahead.emit(None)

    branch(0, pa_ref, pb_ref)
    branch(1, pb_ref, pa_ref)


def _next_tile(n_s):
    return lambda b, s: (b, jnp.minimum(s + 1, n_s - 1), 0)


def _mix_ab_body(x_ref, xn_ref, g_ref, w_ref, lbl_ref, hgn_ref, cw_ref, cb_ref, wo_ref, o_ref,
                 pa_ref, pb_ref, y_ref, st_ref, zb_ref, *, layer):
    tile = functools.partial(_mix_ab_tile, x_ref=x_ref, lbl_ref=lbl_ref, hgn_ref=hgn_ref, cw_ref=cw_ref,
                             cb_ref=cb_ref, wo_ref=wo_ref, o_ref=o_ref, y_ref=y_ref, st_ref=st_ref,
                             zb_ref=zb_ref, layer=layer)
    _with_pipelined_projection(x_ref, xn_ref, g_ref, w_ref, pa_ref, pb_ref, tile)


def _mix_ab_tile(p_ref, ahead, *, x_ref, lbl_ref, hgn_ref, cw_ref, cb_ref, wo_ref, o_ref, y_ref, st_ref, zb_ref, layer):
    s_idx = pl.program_id(1)
    n_c = SEQ_T // HG_CHUNK

    @pl.when(s_idx == 0)
    def _():
        st_ref[...] = jnp.zeros_like(st_ref)
        zb_ref[0:8, :] = jnp.zeros((8, GROUP_W), F32)

    lg = lbl_ref[...]
    ex = jnp.exp(lg - jnp.max(lg, axis=0, keepdims=True))
    sm = ex / jnp.sum(ex, axis=0, keepdims=True)
    lb = jnp.sum(sm[0:layer + 1, :], axis=0, keepdims=True)

    tril, same = _chunk_masks(SEQ_T, HG_CHUNK)
    tril01 = jnp.where(tril, 1.0, 0.0).astype(BF16)
    same01 = jnp.where(same, 1.0, 0.0).astype(BF16)
    emask = _expand_mask(SEQ_T, HG_CHUNK, D_HEAD)

    f = lb + (1.0 - lb) * jax.nn.sigmoid(p_ref[:, GROUP_W:2 * GROUP_W])
    logf = jnp.log(f)
    b_all = _dot_mask(tril01, logf)
    e_all = _dot_mask(same01, logf)
    ahead.emit(2)

    for h in range(N_HEADS):
        ahead.emit(3)
        lo, hi = h * D_HEAD, (h + 1) * D_HEAD
        q = p_ref[:, lo:hi]
        v = p_ref[:, 2 * GROUP_W + lo:2 * GROUP_W + hi]
        g = p_ref[:, 3 * GROUP_W + lo:3 * GROUP_W + hi]
        b = b_all[:, lo:hi]
        e = e_all[:, lo:hi]
        kk = 1.0 - f[:, lo:hi]
        q_dec = (q * jax.nn.sigmoid(q) * jnp.exp(b)).astype(BF16)
        k_dec = (kk * jnp.exp(-b)).astype(BF16)
        k_end = (kk * jnp.exp(e - b)).astype(BF16)
        v16 = v.astype(BF16)

        attn = jnp.where(tril, _dot_nt(q_dec, k_dec), 0.0)
        o = _dot(attn.astype(BF16), v16)

        k_exp = jnp.where(emask, jnp.tile(k_end, (1, n_c)), jnp.zeros((), BF16))
        d_all = _dot(v.T.astype(BF16), k_exp)

        st = st_ref[h]
        prev = []
        for c in range(n_c):
            prev.append(st)
            decay = jnp.exp(e[c * HG_CHUNK:c * HG_CHUNK + 1, :])
            st = decay * st + d_all[:, c * D_HEAD:(c + 1) * D_HEAD]
        st_ref[h] = st
        s_prev = jnp.concatenate(prev, axis=1).astype(BF16)
        q_exp = jnp.where(emask, jnp.tile(q_dec, (1, n_c)), jnp.zeros((), BF16))
        o = o + _dot_nt(q_exp, s_prev)

        o = _rms(o, hgn_ref[:, lo:hi]) * (g * jax.nn.sigmoid(g))
        y_ref[:, lo:hi] = o.astype(BF16)

    z = p_ref[:, 5 * GROUP_W:6 * GROUP_W] * p_ref[:, 6 * GROUP_W:7 * GROUP_W]
    zb_ref[8:SEQ_T + 8, :] = z
    z1 = zb_ref[7:SEQ_T + 7, :]
    z2 = zb_ref[6:SEQ_T + 6, :]
    y = cb_ref[...] + cw_ref[0:1, :] * z2 + cw_ref[1:2, :] * z1 + cw_ref[2:3, :] * z
    y_ref[:, GROUP_W:] = (p_ref[:, 4 * GROUP_W:5 * GROUP_W] * y).astype(BF16)
    zb_ref[0:8, :] = zb_ref[SEQ_T:SEQ_T + 8, :]

    o_ref[...] = x_ref[...] + _dot(y_ref[...], wo_ref[...])


def _mix_ab(x, gain, w_in, lb_logits, hg_norm, conv_w, conv_b, w_out, layer):
    bsz, seq = x.shape[0], x.shape[1]
    n_l = lb_logits.shape[0]
    row = lambda b, s: (0, 0)
    once = pl.Buffered(1)
    return pl.pallas_call(
        functools.partial(_mix_ab_body, layer=layer),
        out_shape=jax.ShapeDtypeStruct((bsz, seq, D_MODEL), F32),
        grid=(bsz, seq // SEQ_T),
        in_specs=[
            pl.BlockSpec((None, SEQ_T, D_MODEL), lambda b, s: (b, s, 0)),
            pl.BlockSpec((None, SEQ_T, D_MODEL), _next_tile(seq // SEQ_T)),
            pl.BlockSpec((1, D_MODEL), row),
            pl.BlockSpec((D_MODEL, AB_COLS), row, pipeline_mode=once),
            pl.BlockSpec((n_l, GROUP_W), row),
            pl.BlockSpec((1, GROUP_W), row),
            pl.BlockSpec((CONV_W, GROUP_W), row),
            pl.BlockSpec((1, GROUP_W), row),
            pl.BlockSpec((D_MODEL, D_MODEL), row, pipeline_mode=once),
        ],
        out_specs=pl.BlockSpec((None, SEQ_T, D_MODEL), lambda b, s: (b, s, 0)),
        scratch_shapes=[pltpu.VMEM((SEQ_T, AB_COLS), F32), pltpu.VMEM((SEQ_T, AB_COLS), F32),
                        pltpu.VMEM((SEQ_T, D_MODEL), BF16),
                        pltpu.VMEM((N_HEADS, D_HEAD, D_HEAD), F32), pltpu.VMEM((SEQ_T + 8, GROUP_W), F32)],
        compiler_params=pltpu.CompilerParams(
            dimension_semantics=("arbitrary", "arbitrary"), vmem_limit_bytes=VMEM_LIMIT),
        name="mix_ab",
    )(x, x, gain.reshape(1, D_MODEL), w_in, lb_logits, hg_norm.reshape(1, GROUP_W), conv_w,
      conv_b.reshape(1, GROUP_W), w_out)


def _mix_cd_body(x_ref, xn_ref, g_ref, w_ref, bias_ref, mln_ref, cos_ref, sin_ref,
                 o_ref, qo_ref, ko_ref, vt_ref, km_ref, ka_ref, pa_ref, pb_ref, c_ref, n_ref, m_ref):
    tile = functools.partial(_mix_cd_tile, bias_ref=bias_ref, mln_ref=mln_ref, cos_ref=cos_ref, sin_ref=sin_ref,
                             o_ref=o_ref, qo_ref=qo_ref, ko_ref=ko_ref, vt_ref=vt_ref, km_ref=km_ref,
                             ka_ref=ka_ref, c_ref=c_ref, n_ref=n_ref, m_ref=m_ref)
    _with_pipelined_projection(x_ref, xn_ref, g_ref, w_ref, pa_ref, pb_ref, tile)


def _mix_cd_tile(p_ref, ahead, *, bias_ref, mln_ref, cos_ref, sin_ref, o_ref, qo_ref, ko_ref, vt_ref, km_ref, ka_ref,
                 c_ref, n_ref, m_ref):
    s_idx = pl.program_id(1)
    n_c = SEQ_T // ML_CHUNK
    gi_ref = p_ref.at[:, 7 * GROUP_W:7 * GROUP_W + LANES]
    gf_ref = p_ref.at[:, 7 * GROUP_W + LANES:7 * GROUP_W + 2 * LANES]

    @pl.when(s_idx == 0)
    def _():
        c_ref[...] = jnp.zeros_like(c_ref)
        n_ref[...] = jnp.zeros_like(n_ref)
        m_ref[...] = jnp.zeros_like(m_ref)

    cos_t, sin_t = cos_ref[...], sin_ref[...]
    for h in range(N_HEADS):
        ahead.emit(1)
        lo, hi = h * D_HEAD, (h + 1) * D_HEAD
        q = _rope(p_ref[:, 4 * GROUP_W + lo:4 * GROUP_W + hi], cos_t, sin_t) * (D_HEAD ** -0.5 * LOG2E)
        k = _rope(p_ref[:, 5 * GROUP_W + lo:5 * GROUP_W + hi], cos_t, sin_t)
        qo_ref[:, lo:hi] = q.astype(BF16)
        ko_ref[h] = k.astype(BF16)
        vt_ref[h] = p_ref[:, 6 * GROUP_W + lo:6 * GROUP_W + hi].T.astype(BF16)
        km_ref[:, lo:hi] = jnp.mean(k, axis=0, keepdims=True)
        ka_ref[:, lo:hi] = jnp.max(jnp.abs(k), axis=0, keepdims=True)

    tril, same = _chunk_masks(SEQ_T, ML_CHUNK)
    tril01 = jnp.where(tril, 1.0, 0.0).astype(BF16)
    same01 = jnp.where(same, 1.0, 0.0).astype(BF16)
    emask = _expand_mask(SEQ_T, ML_CHUNK, D_HEAD)

    log_i = gi_ref[...] + bias_ref[0:1, :]
    log_f = jax.nn.log_sigmoid(gf_ref[...] + bias_ref[1:2, :])
    b_col = _dot_mask(tril01, log_f)
    e_col = _dot_mask(same01, log_f)
    w_end = e_col - b_col + log_i

    m = m_ref[...]
    m_prev_rows, m_new_rows, a_rows = [], [], []
    for c in range(n_c):
        r0 = c * ML_CHUNK
        be = e_col[r0:r0 + 1, :]
        m_end = jnp.max(w_end[r0:r0 + ML_CHUNK, :], axis=0, keepdims=True)
        m_new = jnp.maximum(be + m, m_end)
        a_rows.append(jnp.exp(be + m - m_new))
        m_prev_rows.append(jnp.broadcast_to(m, (ML_CHUNK, LANES)))
        m_new_rows.append(jnp.broadcast_to(m_new, (ML_CHUNK, LANES)))
        m = m_new
    m_ref[...] = m
    log_inter = b_col + jnp.concatenate(m_prev_rows, axis=0)
    wk_scale = jnp.exp(w_end - jnp.concatenate(m_new_rows, axis=0))
    r_rows = (b_col - log_i).T

    for h in range(N_HEADS):
        ahead.emit(3)
        lo, hi = h * D_HEAD, (h + 1) * D_HEAD
        q = p_ref[:, lo:hi] * (D_HEAD ** -0.5)
        k = p_ref[:, GROUP_W + lo:GROUP_W + hi]
        v = p_ref[:, 2 * GROUP_W + lo:2 * GROUP_W + hi]
        og = p_ref[:, 3 * GROUP_W + lo:3 * GROUP_W + hi]
        q16, k16, v16 = q.astype(BF16), k.astype(BF16), v.astype(BF16)

        d_log = jnp.where(tril, b_col[:, h:h + 1] - r_rows[h:h + 1, :], -jnp.inf)
        linter = log_inter[:, h:h + 1]
        m_t = jnp.maximum(linter, jnp.max(d_log, axis=-1, keepdims=True))
        w = jnp.exp(d_log - m_t) * _dot_nt(q16, k16)
        a_in = jnp.exp(linter - m_t)
        num = _dot(w.astype(BF16), v16)
        den = jnp.sum(w, axis=-1, keepdims=True)

        wk = wk_scale[:, h:h + 1] * k
        v_exp = jnp.where(emask, jnp.tile(v16, (1, n_c)), jnp.zeros((), BF16))
        dc_all = _dot(wk.T.astype(BF16), v_exp)

        c_mat = c_ref[h]
        n_vec = n_ref[h]
        c_prev, n_prev = [], []
        for c in range(n_c):
            r0 = c * ML_CHUNK
            c_prev.append(c_mat)
            n_prev.append(jnp.broadcast_to(n_vec, (ML_CHUNK, D_HEAD)))
            a = a_rows[c][:, h:h + 1]
            c_mat = a * c_mat + dc_all[:, c * D_HEAD:(c + 1) * D_HEAD]
            n_vec = a * n_vec + jnp.sum(wk[r0:r0 + ML_CHUNK, :], axis=0, keepdims=True)
        c_ref[h] = c_mat
        n_ref[h] = n_vec

        q_exp = jnp.where(emask, jnp.tile(q16, (1, n_c)), jnp.zeros((), BF16))
        q_c = _dot(q_exp, jnp.concatenate(c_prev, axis=0).astype(BF16))
        q_n = jnp.sum(q * jnp.concatenate(n_prev, axis=0), axis=-1, keepdims=True)
        num = num + a_in * q_c
        den = den + a_in * q_n
        hh = num / jnp.maximum(jnp.abs(den), jnp.exp(-m_t))
        o_ref[:, lo:hi] = (_rms(hh, mln_ref[:, lo:hi]) * jax.nn.sigmoid(og)).astype(BF16)


def _mix_cd(x, gain, w_in, bias_rows, ml_norm, cos_t, sin_t):
    bsz, seq = x.shape[0], x.shape[1]
    n_blk, n_tile = seq // MB_BLOCK, seq // MB_TILE
    assert SEQ_T == MB_BLOCK and MB_TILE == 2 * MB_BLOCK
    row = lambda b, s: (0, 0)
    stat = pl.BlockSpec((None, None, 1, GROUP_W), lambda b, s: (b, s, 0, 0))
    return pl.pallas_call(
        _mix_cd_body,
        out_shape=[
            jax.ShapeDtypeStruct((bsz, seq, GROUP_W), BF16),
            jax.ShapeDtypeStruct((bsz, seq, GROUP_W), BF16),
            jax.ShapeDtypeStruct((bsz, N_HEADS, n_blk, MB_BLOCK, D_HEAD), BF16),
            jax.ShapeDtypeStruct((bsz, N_HEADS, n_tile, D_HEAD, MB_TILE), BF16),
            jax.ShapeDtypeStruct((bsz, n_blk, 1, GROUP_W), F32),
            jax.ShapeDtypeStruct((bsz, n_blk, 1, GROUP_W), F32),
        ],
        grid=(bsz, seq // SEQ_T),
        in_specs=[
            pl.BlockSpec((None, SEQ_T, D_MODEL), lambda b, s: (b, s, 0)),
            pl.BlockSpec((None, SEQ_T, D_MODEL), _next_tile(seq // SEQ_T)),
            pl.BlockSpec((1, D_MODEL), row),
            pl.BlockSpec((D_MODEL, CD_COLS), row, pipeline_mode=pl.Buffered(1)),
            pl.BlockSpec((2, LANES), row),
            pl.BlockSpec((1, GROUP_W), row),
            pl.BlockSpec((SEQ_T, LANES), lambda b, s: (s, 0)),
            pl.BlockSpec((SEQ_T, LANES), lambda b, s: (s, 0)),
        ],
        out_specs=[
            pl.BlockSpec((None, SEQ_T, GROUP_W), lambda b, s: (b, s, 0)),
            pl.BlockSpec((None, SEQ_T, GROUP_W), lambda b, s: (b, s, 0)),
            pl.BlockSpec((None, N_HEADS, None, MB_BLOCK, D_HEAD), lambda b, s: (b, 0, s, 0, 0)),
            pl.BlockSpec((None, N_HEADS, None, D_HEAD, MB_BLOCK), lambda b, s: (b, 0, s // 2, 0, s % 2)),
            stat, stat,
        ],
        scratch_shapes=[pltpu.VMEM((SEQ_T, CD_COLS), F32), pltpu.VMEM((SEQ_T, CD_COLS), F32),
                        pltpu.VMEM((N_HEADS, D_HEAD, D_HEAD), F32),
                        pltpu.VMEM((N_HEADS, 1, D_HEAD), F32),
                        pltpu.VMEM((1, LANES), F32)],
        compiler_params=pltpu.CompilerParams(
            dimension_semantics=("arbitrary", "arbitrary"), vmem_limit_bytes=VMEM_LIMIT),
        name="mix_cd",
    )(x, x, gain.reshape(1, D_MODEL), w_in, bias_rows, ml_norm.reshape(1, GROUP_W), cos_t, sin_t)


def _rope(x, cos_t, sin_t):
    lane = lax.broadcasted_iota(jnp.int32, x.shape, 1)
    half = ROPE_DIM // 2
    swapped = jnp.where(lane < half, pltpu.roll(x, LANES - half, axis=1), pltpu.roll(x, half, axis=1))
    return x * cos_t + swapped * sin_t


def _moba_body(q_ref, k_ref, vt_ref, km_ref, ka_ref, x_ref, ya_ref, wo_ref, o_ref,
               sel_ref, sd_ref, acc_ref, yb_ref, *, n_blk):
    t_own = pl.program_id(1)
    heads = [(h, h * D_HEAD, (h + 1) * D_HEAD) for h in range(N_HEADS)]
    bk = MB_BLOCK
    o_ref[...] = x_ref[...] + _dot(ya_ref[...], wo_ref[:GROUP_W, :])

    blk = lax.broadcasted_iota(jnp.int32, (n_blk, MB_TILE), 0)
    lane = lax.broadcasted_iota(jnp.int32, (n_blk, MB_TILE), 1)
    own = 2 * t_own + lane // bk
    bound_past = []
    for h, lo, hi in heads:
        q = q_ref[:, lo:hi]
        km = km_ref[:, lo:hi]
        km_hi = km.astype(BF16)
        km_lo = (km - km_hi.astype(F32)).astype(BF16)
        gate = _dot_nt(km_hi, q) + _dot_nt(km_lo, q)
        gate = jnp.where(blk < own, gate, -jnp.inf)
        sel = jnp.zeros(gate.shape, F32)
        for _ in range(MB_TOPK):
            mx = jnp.max(gate, axis=0, keepdims=True)
            idx = jnp.min(jnp.where(gate == mx, blk, n_blk), axis=0, keepdims=True)
            pick = blk == jnp.where(mx > -jnp.inf, idx, -1)
            sel = jnp.where(pick, 1.0, sel)
            gate = jnp.where(pick, -jnp.inf, gate)
        sel_ref[h] = sel
        bound = _dot_nt(ka_ref[:, lo:hi].astype(BF16), jnp.abs(q)) * (1.0 + 2.0 ** -6)
        bound_past.append(jnp.max(jnp.where(sel > 0.0, bound, -jnp.inf), axis=0, keepdims=True))

    def live_rows(h, t):
        return sel_ref[h, pl.ds(2 * t, 1), :] > 0.0, sel_ref[h, pl.ds(2 * t + 1, 1), :] > 0.0

    kpos = lax.broadcasted_iota(jnp.int32, (MB_TILE, MB_TILE), 0)
    qpos = lax.broadcasted_iota(jnp.int32, (MB_TILE, MB_TILE), 1)
    kb, qb = kpos // bk, qpos // bk
    causal = jnp.where(kb == qb, jnp.where(kpos <= qpos, 1.0, 0.0), 0.0)
    cross = jnp.where(kb < qb, 1.0, 0.0)

    def visible(h):
        return (causal + cross * sel_ref[h, pl.ds(2 * t_own, 1), :]) > 0.0

    m_diag = []
    for h, lo, hi in heads:
        s = _dot_nt(k_ref[h, t_own], q_ref[:, lo:hi])
        sd_ref[h] = s
        m_diag.append(jnp.max(jnp.where(visible(h), s, -jnp.inf), axis=0, keepdims=True))

    def exact_past_max():
        def tile_max(t, ms):
            out = []
            for h, lo, hi in heads:
                s = _dot_nt(k_ref[h, t], q_ref[:, lo:hi])
                live_a, live_b = live_rows(h, t)
                m_a = jnp.where(live_a, jnp.max(s[:bk], axis=0, keepdims=True), -jnp.inf)
                m_b = jnp.where(live_b, jnp.max(s[bk:], axis=0, keepdims=True), -jnp.inf)
                out.append(jnp.maximum(ms[h], jnp.maximum(m_a, m_b)))
            return tuple(out)

        return lax.fori_loop(0, t_own, tile_max, tuple(jnp.full((1, MB_TILE), -jnp.inf, F32) for _ in heads))

    slack = functools.reduce(jnp.maximum, [jnp.max(bound_past[h] - m_diag[h]) for h, _, _ in heads])
    m_past = lax.cond(slack > MB_BOUND_WINDOW, exact_past_max, lambda: tuple(bound_past))

    m_row, l0 = [], []
    for h, lo, hi in heads:
        m_h = jnp.maximum(m_diag[h], m_past[h])
        p = jnp.where(visible(h), jnp.exp2(sd_ref[h] - m_h), 0.0)
        m_row.append(m_h)
        l0.append(jnp.sum(p, axis=0, keepdims=True))
        acc_ref[h] = _dot(vt_ref[h, t_own], p.astype(BF16))

    def tile_accumulate(t, ls):
        out = []
        scores = _dot_nt(k_ref[0, t], q_ref[:, :D_HEAD])
        for h, lo, hi in heads:
            s = scores
            if h + 1 < N_HEADS:
                scores = _dot_nt(k_ref[h + 1, t], q_ref[:, hi:hi + D_HEAD])
            e = jnp.exp2(s - m_row[h])
            live_a, live_b = live_rows(h, t)
            p = jnp.concatenate([jnp.where(live_a, e[:bk], 0.0), jnp.where(live_b, e[bk:], 0.0)], axis=0)
            out.append(ls[h] + jnp.sum(p, axis=0, keepdims=True))
            acc_ref[h] += _dot(vt_ref[h, t], p.astype(BF16))
        return tuple(out)

    l_f = lax.fori_loop(0, t_own, tile_accumulate, tuple(l0))
    for h, lo, hi in heads:
        yb_ref[:, lo:hi] = (acc_ref[h] / l_f[h]).T.astype(BF16)

    o_ref[...] += _dot(yb_ref[...], wo_ref[GROUP_W:, :])


def _moba(q, k_tiles, vt_tiles, k_mean, k_absmax, x, ya, w_out):
    bsz, seq = q.shape[0], q.shape[1]
    n_blk, n_tile = seq // MB_BLOCK, seq // MB_TILE
    once = pl.Buffered(1)
    tile = lambda b, i: (b, i, 0)
    return pl.pallas_call(
        functools.partial(_moba_body, n_blk=n_blk),
        out_shape=jax.ShapeDtypeStruct((bsz, seq, D_MODEL), F32),
        grid=(bsz, n_tile),
        in_specs=[
            pl.BlockSpec((None, MB_TILE, GROUP_W), tile),
            pl.BlockSpec((None, N_HEADS, n_tile, MB_TILE, D_HEAD), lambda b, i: (b, 0, 0, 0, 0),
                         pipeline_mode=once),
            pl.BlockSpec((None, N_HEADS, n_tile, D_HEAD, MB_TILE), lambda b, i: (b, 0, 0, 0, 0),
                         pipeline_mode=once),
            pl.BlockSpec((None, n_blk, GROUP_W), lambda b, i: (b, 0, 0)),
            pl.BlockSpec((None, n_blk, GROUP_W), lambda b, i: (b, 0, 0)),
            pl.BlockSpec((None, MB_TILE, D_MODEL), tile),
            pl.BlockSpec((None, MB_TILE, GROUP_W), tile),
            pl.BlockSpec((D_MODEL, D_MODEL), lambda b, i: (0, 0), pipeline_mode=once),
        ],
        out_specs=pl.BlockSpec((None, MB_TILE, D_MODEL), tile),
        scratch_shapes=[pltpu.VMEM((N_HEADS, n_blk, MB_TILE), F32),
                        pltpu.VMEM((N_HEADS, MB_TILE, MB_TILE), F32),
                        pltpu.VMEM((N_HEADS, D_HEAD, MB_TILE), F32),
                        pltpu.VMEM((MB_TILE, GROUP_W), BF16)],
        compiler_params=pltpu.CompilerParams(
            dimension_semantics=("parallel", "arbitrary"), vmem_limit_bytes=VMEM_LIMIT),
        name="moba",
    )(q, k_tiles, vt_tiles, k_mean, k_absmax, x, ya, w_out)


def _rope_tables(seq):
    half = ROPE_DIM // 2
    inv_freq = jnp.float32(ROPE_THETA) ** (-jnp.arange(half, dtype=F32) * 2.0 / ROPE_DIM)
    ang = jnp.arange(seq).astype(F32)[:, None] * inv_freq[None, :]
    cos, sin = jnp.cos(ang), jnp.sin(ang)
    rest = LANES - ROPE_DIM
    cos_t = jnp.concatenate([cos, cos, jnp.ones((seq, rest), F32)], axis=1)
    sin_t = jnp.concatenate([-sin, sin, jnp.zeros((seq, rest), F32)], axis=1)
    return cos_t, sin_t


def kernel(x, ffn_norm, ffn_w_in, ffn_w_out, mix_norm, ab_w_in, ab_w_out, hgrn_lb_logits, hgrn_out_norm,
           conv_w, conv_b, cd_w_in, cd_w_out, mlstm_gate_bias, mlstm_out_norm, final_norm):
    bsz, seq, d = x.shape
    depth = ffn_norm.shape[0]
    t = bsz * seq
    xt = x.reshape(t, d)
    cos_t, sin_t = _rope_tables(seq)

    for layer in range(depth):
        xt = _ffn(xt, ffn_norm[layer, 0], ffn_w_in[layer, 0].astype(BF16), ffn_w_out[layer, 0].astype(BF16),
                  final_norm, False)
        x3 = xt.reshape(bsz, seq, d)
        if layer % 2 == 0:
            e = layer // 2
            x3 = _mix_ab(x3, mix_norm[layer], ab_w_in[e].astype(BF16), hgrn_lb_logits, hgrn_out_norm[e],
                         conv_w[e], conv_b[e], ab_w_out[e].astype(BF16), layer)
        else:
            o = layer // 2
            w = cd_w_in[o]
            n_ml = 4 * GROUP_W
            pad = jnp.zeros((d, LANES - N_HEADS), w.dtype)
            w = jnp.concatenate([w[:, :n_ml], w[:, n_ml + 2 * N_HEADS:],
                                 w[:, n_ml:n_ml + N_HEADS], pad,
                                 w[:, n_ml + N_HEADS:n_ml + 2 * N_HEADS], pad], axis=1)
            bias_rows = jnp.pad(mlstm_gate_bias[o], ((0, 0), (0, LANES - N_HEADS)))
            ya, q_r, k_blocks, vt_tiles, k_mean, k_absmax = _mix_cd(
                x3, mix_norm[layer], w.astype(BF16), bias_rows, mlstm_out_norm[o], cos_t, sin_t)
            n_blk = seq // MB_BLOCK
            x3 = _moba(q_r, k_blocks.reshape(bsz, N_HEADS, seq // MB_TILE, MB_TILE, D_HEAD), vt_tiles,
                       k_mean.reshape(bsz, n_blk, GROUP_W), k_absmax.reshape(bsz, n_blk, GROUP_W),
                       x3, ya, cd_w_out[o].astype(BF16))
        xt = x3.reshape(t, d)
        xt = _ffn(xt, ffn_norm[layer, 1], ffn_w_in[layer, 1].astype(BF16), ffn_w_out[layer, 1].astype(BF16),
                  final_norm, layer == depth - 1)
    return xt.reshape(bsz, seq, d)
```

```python
import functools

import jax
import jax.numpy as jnp
from jax import lax
from jax.experimental import pallas as pl
from jax.experimental.pallas import tpu as pltpu

F32 = jnp.float32
BF16 = jnp.bfloat16

D_MODEL = 1024
D_FF = 2816
GROUP_W = 512
N_HEADS = 4
D_HEAD = 128
RMS_EPS = 1e-6
HG_CHUNK = 32
ML_CHUNK = 64
MB_BLOCK = 256
MB_TOPK = 3
CONV_W = 3
ROPE_THETA = 500000.0
ROPE_DIM = D_HEAD // 4

LANES = 128
VMEM_LIMIT = 48 * 1024 * 1024

FFN_TM = 1024
FFN_SUB = 512
FFN_CH = 256
PROJ_CH = 256
SEQ_T = 256
MB_TILE = 2 * MB_BLOCK
MB_BOUND_WINDOW = 80.0
LOG2E = 1.4426950408889634

AB_COLS = 7 * GROUP_W
CD_COLS = 7 * GROUP_W + 2 * LANES


def _rms(x, gain):
    return x * lax.rsqrt(jnp.mean(x * x, axis=-1, keepdims=True) + RMS_EPS) * gain


def _dot(a, b):
    return jnp.dot(a, b, preferred_element_type=F32)


def _dot_nt(a, b):
    return lax.dot_general(a, b, (((1,), (1,)), ((), ())), preferred_element_type=F32)


def _dot_mask(mask01, x):
    hi = x.astype(BF16)
    r1 = x - hi.astype(F32)
    mid = r1.astype(BF16)
    lo = (r1 - mid.astype(F32)).astype(BF16)
    return _dot(mask01, hi) + _dot(mask01, mid) + _dot(mask01, lo)


def _chunk_masks(n, chunk):
    r = lax.broadcasted_iota(jnp.int32, (n, n), 0)
    c = lax.broadcasted_iota(jnp.int32, (n, n), 1)
    same = (r // chunk) == (c // chunk)
    return jnp.where(same, jnp.where(c <= r, 1, 0), 0) > 0, same


def _expand_mask(n, chunk, width):
    n_c = n // chunk
    r = lax.broadcasted_iota(jnp.int32, (n, n_c * width), 0)
    c = lax.broadcasted_iota(jnp.int32, (n, n_c * width), 1)
    return (r // chunk) == (c // width)


def _ffn_body(x_ref, g_ref, wg_ref, wu_ref, wo_ref, fg_ref, o_ref, act_ref, *, final_norm):
    for r0 in range(0, FFN_TM, FFN_SUB):
        rows = pl.ds(r0, FFN_SUB)
        h = _rms(x_ref[rows, :], g_ref[...]).astype(BF16)
        for c0 in range(0, D_FF, FFN_CH):
            gate = _dot(h, wg_ref[:, c0:c0 + FFN_CH])
            up = _dot(h, wu_ref[:, c0:c0 + FFN_CH])
            act_ref[rows, c0:c0 + FFN_CH] = (gate * jax.nn.sigmoid(gate) * up).astype(BF16)
        y = x_ref[rows, :] + 0.5 * _dot(act_ref[rows, :], wo_ref[...])
        if final_norm:
            y = _rms(y, fg_ref[...])
        o_ref[rows, :] = y


def _ffn(x, gain, w_in, w_out, layer, which, final_gain, final_norm):
    t = x.shape[0]
    once = pl.Buffered(1)
    return pl.pallas_call(
        functools.partial(_ffn_body, final_norm=final_norm),
        out_shape=jax.ShapeDtypeStruct((t, D_MODEL), F32),
        grid=(t // FFN_TM,),
        in_specs=[
            pl.BlockSpec((FFN_TM, D_MODEL), lambda i: (i, 0)),
            pl.BlockSpec((1, D_MODEL), lambda i: (0, 0)),
            pl.BlockSpec((None, None, D_MODEL, D_FF), lambda i: (layer, which, 0, 0), pipeline_mode=once),
            pl.BlockSpec((None, None, D_MODEL, D_FF), lambda i: (layer, which, 0, 1), pipeline_mode=once),
            pl.BlockSpec((None, None, D_FF, D_MODEL), lambda i: (layer, which, 0, 0), pipeline_mode=once),
            pl.BlockSpec((1, D_MODEL), lambda i: (0, 0)),
        ],
        out_specs=pl.BlockSpec((FFN_TM, D_MODEL), lambda i: (i, 0)),
        scratch_shapes=[pltpu.VMEM((FFN_TM, D_FF), BF16)],
        compiler_params=pltpu.CompilerParams(
            dimension_semantics=("parallel",), vmem_limit_bytes=VMEM_LIMIT),
        name="ffn",
    )(x, gain.reshape(1, D_MODEL), w_in, w_in, w_out, final_gain.reshape(1, D_MODEL))


class _ChunkedProjection:
    def __init__(self, x_ref, g_ref, w_refs, p_ref):
        self.h = _rms(x_ref[...], g_ref[...]).astype(BF16)
        self.p_ref = p_ref
        self.todo = [(w_ref, c0) for w_ref in w_refs for c0 in range(0, w_ref.shape[1], PROJ_CH)]
        self.done = 0

    def emit(self, n_chunks):
        end = len(self.todo) if n_chunks is None else min(len(self.todo), self.done + n_chunks)
        for i in range(self.done, end):
            w_ref, c0 = self.todo[i]
            self.p_ref[:, i * PROJ_CH:(i + 1) * PROJ_CH] = _dot(self.h, w_ref[:, c0:c0 + PROJ_CH])
        self.done = end


def _project_in(x_ref, g_ref, w_refs, p_ref):
    _ChunkedProjection(x_ref, g_ref, w_refs, p_ref).emit(None)


def _with_pipelined_projection(x_ref, xn_ref, g_ref, w_refs, pa_ref, pb_ref, tile_fn):
    s_idx = pl.program_id(1)

    @pl.when(s_idx == 0)
    def _():
        _project_in(x_ref, g_ref, w_refs, pa_ref)

    def branch(parity, p_cur, p_nxt):
        @pl.when(s_idx % 2 == parity)
        def _():
            ahead = _ChunkedProjection(xn_ref, g_ref, w_refs, p_nxt)
            tile_fn(p_cur, ahead)
            ahead.emit(None)

    branch(0, pa_ref, pb_ref)
    branch(1, pb_ref, pa_ref)


def _next_tile(n_s):
    return lambda b, s: (b, jnp.minimum(s + 1, n_s - 1), 0)


def _mix_ab_body(x_ref, xn_ref, g_ref, w_ref, lbl_ref, hgn_ref, cw_ref, cb_ref, wo_ref, o_ref,
                 pa_ref, pb_ref, y_ref, st_ref, zb_ref, *, layer):
    tile = functools.partial(_mix_ab_tile, x_ref=x_ref, lbl_ref=lbl_ref, hgn_ref=hgn_ref, cw_ref=cw_ref,
                             cb_ref=cb_ref, wo_ref=wo_ref, o_ref=o_ref, y_ref=y_ref, st_ref=st_ref,
                             zb_ref=zb_ref, layer=layer)
    _with_pipelined_projection(x_ref, xn_ref, g_ref, (w_ref,), pa_ref, pb_ref, tile)


def _mix_ab_tile(p_ref, ahead, *, x_ref, lbl_ref, hgn_ref, cw_ref, cb_ref, wo_ref, o_ref, y_ref, st_ref, zb_ref, layer):
    s_idx = pl.program_id(1)
    n_c = SEQ_T // HG_CHUNK

    @pl.when(s_idx == 0)
    def _():
        st_ref[...] = jnp.zeros_like(st_ref)
        zb_ref[0:8, :] = jnp.zeros((8, GROUP_W), F32)

    lg = lbl_ref[...]
    ex = jnp.exp(lg - jnp.max(lg, axis=0, keepdims=True))
    sm = ex / jnp.sum(ex, axis=0, keepdims=True)
    lb = jnp.sum(sm[0:layer + 1, :], axis=0, keepdims=True)

    tril, same = _chunk_masks(SEQ_T, HG_CHUNK)
    tril01 = jnp.where(tril, 1.0, 0.0).astype(BF16)
    same01 = jnp.where(same, 1.0, 0.0).astype(BF16)
    emask = _expand_mask(SEQ_T, HG_CHUNK, D_HEAD)

    f = lb + (1.0 - lb) * jax.nn.sigmoid(p_ref[:, GROUP_W:2 * GROUP_W])
    logf = jnp.log(f)
    b_all = _dot_mask(tril01, logf)
    e_all = _dot_mask(same01, logf)
    ahead.emit(2)

    for h in range(N_HEADS):
        ahead.emit(3)
        lo, hi = h * D_HEAD, (h + 1) * D_HEAD
        q = p_ref[:, lo:hi]
        v = p_ref[:, 2 * GROUP_W + lo:2 * GROUP_W + hi]
        g = p_ref[:, 3 * GROUP_W + lo:3 * GROUP_W + hi]
        b = b_all[:, lo:hi]
        e = e_all[:, lo:hi]
        kk = 1.0 - f[:, lo:hi]
        q_dec = (q * jax.nn.sigmoid(q) * jnp.exp(b)).astype(BF16)
        k_dec = (kk * jnp.exp(-b)).astype(BF16)
        k_end = (kk * jnp.exp(e - b)).astype(BF16)
        v16 = v.astype(BF16)

        attn = jnp.where(tril, _dot_nt(q_dec, k_dec), 0.0)
        o = _dot(attn.astype(BF16), v16)

        k_exp = jnp.where(emask, jnp.tile(k_end, (1, n_c)), jnp.zeros((), BF16))
        d_all = _dot(v.T.astype(BF16), k_exp)

        st = st_ref[h]
        prev = []
        for c in range(n_c):
            prev.append(st)
            decay = jnp.exp(e[c * HG_CHUNK:c * HG_CHUNK + 1, :])
            st = decay * st + d_all[:, c * D_HEAD:(c + 1) * D_HEAD]
        st_ref[h] = st
        s_prev = jnp.concatenate(prev, axis=1).astype(BF16)
        q_exp = jnp.where(emask, jnp.tile(q_dec, (1, n_c)), jnp.zeros((), BF16))
        o = o + _dot_nt(q_exp, s_prev)

        o = _rms(o, hgn_ref[:, lo:hi]) * (g * jax.nn.sigmoid(g))
        y_ref[:, lo:hi] = o.astype(BF16)

    z = p_ref[:, 5 * GROUP_W:6 * GROUP_W] * p_ref[:, 6 * GROUP_W:7 * GROUP_W]
    zb_ref[8:SEQ_T + 8, :] = z
    z1 = zb_ref[7:SEQ_T + 7, :]
    z2 = zb_ref[6:SEQ_T + 6, :]
    y = cb_ref[...] + cw_ref[0:1, :] * z2 + cw_ref[1:2, :] * z1 + cw_ref[2:3, :] * z
    y_ref[:, GROUP_W:] = (p_ref[:, 4 * GROUP_W:5 * GROUP_W] * y).astype(BF16)
    zb_ref[0:8, :] = zb_ref[SEQ_T:SEQ_T + 8, :]

    o_ref[...] = x_ref[...] + _dot(y_ref[...], wo_ref[...])


def _mix_ab(x, gain, w_in, lb_logits, hg_norm, conv_w, conv_b, w_out, layer):
    bsz, seq = x.shape[0], x.shape[1]
    n_l = lb_logits.shape[0]
    row = lambda b, s: (0, 0)
    once = pl.Buffered(1)
    return pl.pallas_call(
        functools.partial(_mix_ab_body, layer=layer),
        out_shape=jax.ShapeDtypeStruct((bsz, seq, D_MODEL), F32),
        grid=(bsz, seq // SEQ_T),
        in_specs=[
            pl.BlockSpec((None, SEQ_T, D_MODEL), lambda b, s: (b, s, 0)),
            pl.BlockSpec((None, SEQ_T, D_MODEL), _next_tile(seq // SEQ_T)),
            pl.BlockSpec((1, D_MODEL), row),
            pl.BlockSpec((D_MODEL, AB_COLS), row, pipeline_mode=once),
            pl.BlockSpec((n_l, GROUP_W), row),
            pl.BlockSpec((1, GROUP_W), row),
            pl.BlockSpec((CONV_W, GROUP_W), row),
            pl.BlockSpec((1, GROUP_W), row),
            pl.BlockSpec((D_MODEL, D_MODEL), row, pipeline_mode=once),
        ],
        out_specs=pl.BlockSpec((None, SEQ_T, D_MODEL), lambda b, s: (b, s, 0)),
        scratch_shapes=[pltpu.VMEM((SEQ_T, AB_COLS), F32), pltpu.VMEM((SEQ_T, AB_COLS), F32),
                        pltpu.VMEM((SEQ_T, D_MODEL), BF16),
                        pltpu.VMEM((N_HEADS, D_HEAD, D_HEAD), F32), pltpu.VMEM((SEQ_T + 8, GROUP_W), F32)],
        compiler_params=pltpu.CompilerParams(
            dimension_semantics=("arbitrary", "arbitrary"), vmem_limit_bytes=VMEM_LIMIT),
        name="mix_ab",
    )(x, x, gain.reshape(1, D_MODEL), w_in, lb_logits, hg_norm.reshape(1, GROUP_W), conv_w,
      conv_b.reshape(1, GROUP_W), w_out)


def _mix_cd_body(x_ref, xn_ref, g_ref, w_ml_ref, w_mb_ref, w_gate_ref, bias_ref, mln_ref, cos_ref, sin_ref,
                 o_ref, qo_ref, ko_ref, vt_ref, km_ref, ka_ref, pa_ref, pb_ref, c_ref, n_ref, m_ref):
    tile = functools.partial(_mix_cd_tile, bias_ref=bias_ref, mln_ref=mln_ref, cos_ref=cos_ref, sin_ref=sin_ref,
                             o_ref=o_ref, qo_ref=qo_ref, ko_ref=ko_ref, vt_ref=vt_ref, km_ref=km_ref,
                             ka_ref=ka_ref, c_ref=c_ref, n_ref=n_ref, m_ref=m_ref)
    _with_pipelined_projection(x_ref, xn_ref, g_ref, (w_ml_ref, w_mb_ref, w_gate_ref), pa_ref, pb_ref, tile)


def _mix_cd_tile(p_ref, ahead, *, bias_ref, mln_ref, cos_ref, sin_ref, o_ref, qo_ref, ko_ref, vt_ref, km_ref, ka_ref,
                 c_ref, n_ref, m_ref):
    s_idx = pl.program_id(1)
    n_c = SEQ_T // ML_CHUNK
    gi_ref = p_ref.at[:, 7 * GROUP_W:7 * GROUP_W + LANES]
    gf_ref = p_ref.at[:, 7 * GROUP_W + LANES:7 * GROUP_W + 2 * LANES]

    @pl.when(s_idx == 0)
    def _():
        c_ref[...] = jnp.zeros_like(c_ref)
        n_ref[...] = jnp.zeros_like(n_ref)
        m_ref[...] = jnp.zeros_like(m_ref)

    cos_t, sin_t = cos_ref[...], sin_ref[...]
    for h in range(N_HEADS):
        ahead.emit(1)
        lo, hi = h * D_HEAD, (h + 1) * D_HEAD
        q = _rope(p_ref[:, 4 * GROUP_W + lo:4 * GROUP_W + hi], cos_t, sin_t) * (D_HEAD ** -0.5 * LOG2E)
        k = _rope(p_ref[:, 5 * GROUP_W + lo:5 * GROUP_W + hi], cos_t, sin_t)
        qo_ref[:, lo:hi] = q.astype(BF16)
        ko_ref[h] = k.astype(BF16)
        vt_ref[h] = p_ref[:, 6 * GROUP_W + lo:6 * GROUP_W + hi].T.astype(BF16)
        km_ref[:, lo:hi] = jnp.mean(k, axis=0, keepdims=True)
        ka_ref[:, lo:hi] = jnp.max(jnp.abs(k), axis=0, keepdims=True)

    tril, same = _chunk_masks(SEQ_T, ML_CHUNK)
    tril01 = jnp.where(tril, 1.0, 0.0).astype(BF16)
    same01 = jnp.where(same, 1.0, 0.0).astype(BF16)
    emask = _expand_mask(SEQ_T, ML_CHUNK, D_HEAD)

    log_i = gi_ref[...] + bias_ref[0:1, :]
    log_f = jax.nn.log_sigmoid(gf_ref[...] + bias_ref[1:2, :])
    b_col = _dot_mask(tril01, log_f)
    e_col = _dot_mask(same01, log_f)
    w_end = e_col - b_col + log_i

    m = m_ref[...]
    m_prev_rows, m_new_rows, a_rows = [], [], []
    for c in range(n_c):
        r0 = c * ML_CHUNK
        be = e_col[r0:r0 + 1, :]
        m_end = jnp.max(w_end[r0:r0 + ML_CHUNK, :], axis=0, keepdims=True)
        m_new = jnp.maximum(be + m, m_end)
        a_rows.append(jnp.exp(be + m - m_new))
        m_prev_rows.append(jnp.broadcast_to(m, (ML_CHUNK, LANES)))
        m_new_rows.append(jnp.broadcast_to(m_new, (ML_CHUNK, LANES)))
        m = m_new
    m_ref[...] = m
    log_inter = b_col + jnp.concatenate(m_prev_rows, axis=0)
    wk_scale = jnp.exp(w_end - jnp.concatenate(m_new_rows, axis=0))
    r_rows = (b_col - log_i).T

    for h in range(N_HEADS):
        ahead.emit(3)
        lo, hi = h * D_HEAD, (h + 1) * D_HEAD
        q = p_ref[:, lo:hi] * (D_HEAD ** -0.5)
        k = p_ref[:, GROUP_W + lo:GROUP_W + hi]
        v = p_ref[:, 2 * GROUP_W + lo:2 * GROUP_W + hi]
        og = p_ref[:, 3 * GROUP_W + lo:3 * GROUP_W + hi]
        q16, k16, v16 = q.astype(BF16), k.astype(BF16), v.astype(BF16)

        d_log = jnp.where(tril, b_col[:, h:h + 1] - r_rows[h:h + 1, :], -jnp.inf)
        linter = log_inter[:, h:h + 1]
        m_t = jnp.maximum(linter, jnp.max(d_log, axis=-1, keepdims=True))
        w = jnp.exp(d_log - m_t) * _dot_nt(q16, k16)
        a_in = jnp.exp(linter - m_t)
        num = _dot(w.astype(BF16), v16)
        den = jnp.sum(w, axis=-1, keepdims=True)

        wk = wk_scale[:, h:h + 1] * k
        v_exp = jnp.where(emask, jnp.tile(v16, (1, n_c)), jnp.zeros((), BF16))
        dc_all = _dot(wk.T.astype(BF16), v_exp)

        c_mat = c_ref[h]
        n_vec = n_ref[h]
        c_prev, n_prev = [], []
        for c in range(n_c):
            r0 = c * ML_CHUNK
            c_prev.append(c_mat)
            n_prev.append(jnp.broadcast_to(n_vec, (ML_CHUNK, D_HEAD)))
            a = a_rows[c][:, h:h + 1]
            c_mat = a * c_mat + dc_all[:, c * D_HEAD:(c + 1) * D_HEAD]
            n_vec = a * n_vec + jnp.sum(wk[r0:r0 + ML_CHUNK, :], axis=0, keepdims=True)
        c_ref[h] = c_mat
        n_ref[h] = n_vec

        q_exp = jnp.where(emask, jnp.tile(q16, (1, n_c)), jnp.zeros((), BF16))
        q_c = _dot(q_exp, jnp.concatenate(c_prev, axis=0).astype(BF16))
        q_n = jnp.sum(q * jnp.concatenate(n_prev, axis=0), axis=-1, keepdims=True)
        num = num + a_in * q_c
        den = den + a_in * q_n
        hh = num / jnp.maximum(jnp.abs(den), jnp.exp(-m_t))
        o_ref[:, lo:hi] = (_rms(hh, mln_ref[:, lo:hi]) * jax.nn.sigmoid(og)).astype(BF16)


def _mix_cd(x, gain, w_ml, w_mb, w_gate, bias_rows, ml_norm, cos_t, sin_t):
    bsz, seq = x.shape[0], x.shape[1]
    n_blk, n_tile = seq // MB_BLOCK, seq // MB_TILE
    assert SEQ_T == MB_BLOCK and MB_TILE == 2 * MB_BLOCK
    row = lambda b, s: (0, 0)
    stat = pl.BlockSpec((None, None, 1, GROUP_W), lambda b, s: (b, s, 0, 0))
    return pl.pallas_call(
        _mix_cd_body,
        out_shape=[
            jax.ShapeDtypeStruct((bsz, seq, GROUP_W), BF16),
            jax.ShapeDtypeStruct((bsz, seq, GROUP_W), BF16),
            jax.ShapeDtypeStruct((bsz, N_HEADS, n_blk, MB_BLOCK, D_HEAD), BF16),
            jax.ShapeDtypeStruct((bsz, N_HEADS, n_tile, D_HEAD, MB_TILE), BF16),
            jax.ShapeDtypeStruct((bsz, n_blk, 1, GROUP_W), F32),
            jax.ShapeDtypeStruct((bsz, n_blk, 1, GROUP_W), F32),
        ],
        grid=(bsz, seq // SEQ_T),
        in_specs=[
            pl.BlockSpec((None, SEQ_T, D_MODEL), lambda b, s: (b, s, 0)),
            pl.BlockSpec((None, SEQ_T, D_MODEL), _next_tile(seq // SEQ_T)),
            pl.BlockSpec((1, D_MODEL), row),
            pl.BlockSpec(w_ml.shape, row, pipeline_mode=pl.Buffered(1)),
            pl.BlockSpec(w_mb.shape, row, pipeline_mode=pl.Buffered(1)),
            pl.BlockSpec(w_gate.shape, row, pipeline_mode=pl.Buffered(1)),
            pl.BlockSpec((2, LANES), row),
            pl.BlockSpec((1, GROUP_W), row),
            pl.BlockSpec((SEQ_T, LANES), lambda b, s: (s, 0)),
            pl.BlockSpec((SEQ_T, LANES), lambda b, s: (s, 0)),
        ],
        out_specs=[
            pl.BlockSpec((None, SEQ_T, GROUP_W), lambda b, s: (b, s, 0)),
            pl.BlockSpec((None, SEQ_T, GROUP_W), lambda b, s: (b, s, 0)),
            pl.BlockSpec((None, N_HEADS, None, MB_BLOCK, D_HEAD), lambda b, s: (b, 0, s, 0, 0)),
            pl.BlockSpec((None, N_HEADS, None, D_HEAD, MB_BLOCK), lambda b, s: (b, 0, s // 2, 0, s % 2)),
            stat, stat,
        ],
        scratch_shapes=[pltpu.VMEM((SEQ_T, CD_COLS), F32), pltpu.VMEM((SEQ_T, CD_COLS), F32),
                        pltpu.VMEM((N_HEADS, D_HEAD, D_HEAD), F32),
                        pltpu.VMEM((N_HEADS, 1, D_HEAD), F32),
                        pltpu.VMEM((1, LANES), F32)],
        compiler_params=pltpu.CompilerParams(
            dimension_semantics=("arbitrary", "arbitrary"), vmem_limit_bytes=VMEM_LIMIT),
        name="mix_cd",
    )(x, x, gain.reshape(1, D_MODEL), w_ml, w_mb, w_gate, bias_rows, ml_norm.reshape(1, GROUP_W), cos_t, sin_t)


def _rope(x, cos_t, sin_t):
    lane = lax.broadcasted_iota(jnp.int32, x.shape, 1)
    half = ROPE_DIM // 2
    swapped = jnp.where(lane < half, pltpu.roll(x, LANES - half, axis=1), pltpu.roll(x, half, axis=1))
    return x * cos_t + swapped * sin_t


def _moba_body(q_ref, k_ref, vt_ref, km_ref, ka_ref, x_ref, ya_ref, wo_ref, o_ref,
               sel_ref, sd_ref, acc_ref, yb_ref, *, n_blk):
    t_own = pl.program_id(1)
    heads = [(h, h * D_HEAD, (h + 1) * D_HEAD) for h in range(N_HEADS)]
    bk = MB_BLOCK
    o_ref[...] = x_ref[...] + _dot(ya_ref[...], wo_ref[:GROUP_W, :])

    blk = lax.broadcasted_iota(jnp.int32, (n_blk, MB_TILE), 0)
    lane = lax.broadcasted_iota(jnp.int32, (n_blk, MB_TILE), 1)
    own = 2 * t_own + lane // bk
    bound_past = []
    for h, lo, hi in heads:
        q = q_ref[:, lo:hi]
        km = km_ref[:, lo:hi]
        km_hi = km.astype(BF16)
        km_lo = (km - km_hi.astype(F32)).astype(BF16)
        gate = _dot_nt(km_hi, q) + _dot_nt(km_lo, q)
        gate = jnp.where(blk < own, gate, -jnp.inf)
        sel = jnp.zeros(gate.shape, F32)
        for _ in range(MB_TOPK):
            mx = jnp.max(gate, axis=0, keepdims=True)
            idx = jnp.min(jnp.where(gate == mx, blk, n_blk), axis=0, keepdims=True)
            pick = blk == jnp.where(mx > -jnp.inf, idx, -1)
            sel = jnp.where(pick, 1.0, sel)
            gate = jnp.where(pick, -jnp.inf, gate)
        sel_ref[h] = sel
        bound = _dot_nt(ka_ref[:, lo:hi].astype(BF16), jnp.abs(q)) * (1.0 + 2.0 ** -6)
        bound_past.append(jnp.max(jnp.where(sel > 0.0, bound, -jnp.inf), axis=0, keepdims=True))

    def live_rows(h, t):
        return sel_ref[h, pl.ds(2 * t, 1), :] > 0.0, sel_ref[h, pl.ds(2 * t + 1, 1), :] > 0.0

    kpos = lax.broadcasted_iota(jnp.int32, (MB_TILE, MB_TILE), 0)
    qpos = lax.broadcasted_iota(jnp.int32, (MB_TILE, MB_TILE), 1)
    kb, qb = kpos // bk, qpos // bk
    causal = jnp.where(kb == qb, jnp.where(kpos <= qpos, 1.0, 0.0), 0.0)
    cross = jnp.where(kb < qb, 1.0, 0.0)

    def visible(h):
        return (causal + cross * sel_ref[h, pl.ds(2 * t_own, 1), :]) > 0.0

    m_diag = []
    for h, lo, hi in heads:
        s = _dot_nt(k_ref[h, t_own], q_ref[:, lo:hi])
        sd_ref[h] = s
        m_diag.append(jnp.max(jnp.where(visible(h), s, -jnp.inf), axis=0, keepdims=True))

    def exact_past_max():
        def tile_max(t, ms):
            out = []
            for h, lo, hi in heads:
                s = _dot_nt(k_ref[h, t], q_ref[:, lo:hi])
                live_a, live_b = live_rows(h, t)
                m_a = jnp.where(live_a, jnp.max(s[:bk], axis=0, keepdims=True), -jnp.inf)
                m_b = jnp.where(live_b, jnp.max(s[bk:], axis=0, keepdims=True), -jnp.inf)
                out.append(jnp.maximum(ms[h], jnp.maximum(m_a, m_b)))
            return tuple(out)

        return lax.fori_loop(0, t_own, tile_max, tuple(jnp.full((1, MB_TILE), -jnp.inf, F32) for _ in heads))

    slack = functools.reduce(jnp.maximum, [jnp.max(bound_past[h] - m_diag[h]) for h, _, _ in heads])
    m_past = lax.cond(slack > MB_BOUND_WINDOW, exact_past_max, lambda: tuple(bound_past))

    m_row, l0 = [], []
    for h, lo, hi in heads:
        m_h = jnp.maximum(m_diag[h], m_past[h])
        p = jnp.where(visible(h), jnp.exp2(sd_ref[h] - m_h), 0.0)
        m_row.append(m_h)
        l0.append(jnp.sum(p, axis=0, keepdims=True))
        acc_ref[h] = _dot(vt_ref[h, t_own], p.astype(BF16))

    def tile_accumulate(t, ls):
        out = []
        scores = _dot_nt(k_ref[0, t], q_ref[:, :D_HEAD])
        for h, lo, hi in heads:
            s = scores
            if h + 1 < N_HEADS:
                scores = _dot_nt(k_ref[h + 1, t], q_ref[:, hi:hi + D_HEAD])
            e = jnp.exp2(s - m_row[h])
            live_a, live_b = live_rows(h, t)
            p = jnp.concatenate([jnp.where(live_a, e[:bk], 0.0), jnp.where(live_b, e[bk:], 0.0)], axis=0)
            out.append(ls[h] + jnp.sum(p, axis=0, keepdims=True))
            acc_ref[h] += _dot(vt_ref[h, t], p.astype(BF16))
        return tuple(out)

    l_f = lax.fori_loop(0, t_own, tile_accumulate, tuple(l0))
    for h, lo, hi in heads:
        yb_ref[:, lo:hi] = (acc_ref[h] / l_f[h]).T.astype(BF16)

    o_ref[...] += _dot(yb_ref[...], wo_ref[GROUP_W:, :])


def _moba(q, k_tiles, vt_tiles, k_mean, k_absmax, x, ya, w_out):
    bsz, seq = q.shape[0], q.shape[1]
    n_blk, n_tile = seq // MB_BLOCK, seq // MB_TILE
    once = pl.Buffered(1)
    tile = lambda b, i: (b, i, 0)
    return pl.pallas_call(
        functools.partial(_moba_body, n_blk=n_blk),
        out_shape=jax.ShapeDtypeStruct((bsz, seq, D_MODEL), F32),
        grid=(bsz, n_tile),
        in_specs=[
            pl.BlockSpec((None, MB_TILE, GROUP_W), tile),
            pl.BlockSpec((None, N_HEADS, n_tile, MB_TILE, D_HEAD), lambda b, i: (b, 0, 0, 0, 0),
                         pipeline_mode=once),
            pl.BlockSpec((None, N_HEADS, n_tile, D_HEAD, MB_TILE), lambda b, i: (b, 0, 0, 0, 0),
                         pipeline_mode=once),
            pl.BlockSpec((None, n_blk, GROUP_W), lambda b, i: (b, 0, 0)),
            pl.BlockSpec((None, n_blk, GROUP_W), lambda b, i: (b, 0, 0)),
            pl.BlockSpec((None, MB_TILE, D_MODEL), tile),
            pl.BlockSpec((None, MB_TILE, GROUP_W), tile),
            pl.BlockSpec((D_MODEL, D_MODEL), lambda b, i: (0, 0), pipeline_mode=once),
        ],
        out_specs=pl.BlockSpec((None, MB_TILE, D_MODEL), tile),
        scratch_shapes=[pltpu.VMEM((N_HEADS, n_blk, MB_TILE), F32),
                        pltpu.VMEM((N_HEADS, MB_TILE, MB_TILE), F32),
                        pltpu.VMEM((N_HEADS, D_HEAD, MB_TILE), F32),
                        pltpu.VMEM((MB_TILE, GROUP_W), BF16)],
        compiler_params=pltpu.CompilerParams(
            dimension_semantics=("parallel", "arbitrary"), vmem_limit_bytes=VMEM_LIMIT),
        name="moba",
    )(q, k_tiles, vt_tiles, k_mean, k_absmax, x, ya, w_out)


def _rope_tables(seq):
    half = ROPE_DIM // 2
    inv_freq = jnp.float32(ROPE_THETA) ** (-jnp.arange(half, dtype=F32) * 2.0 / ROPE_DIM)
    ang = jnp.arange(seq).astype(F32)[:, None] * inv_freq[None, :]
    cos, sin = jnp.cos(ang), jnp.sin(ang)
    rest = LANES - ROPE_DIM
    cos_t = jnp.concatenate([cos, cos, jnp.ones((seq, rest), F32)], axis=1)
    sin_t = jnp.concatenate([-sin, sin, jnp.zeros((seq, rest), F32)], axis=1)
    return cos_t, sin_t


def kernel(x, ffn_norm, ffn_w_in, ffn_w_out, mix_norm, ab_w_in, ab_w_out, hgrn_lb_logits, hgrn_out_norm,
           conv_w, conv_b, cd_w_in, cd_w_out, mlstm_gate_bias, mlstm_out_norm, final_norm):
    bsz, seq, d = x.shape
    depth = ffn_norm.shape[0]
    t = bsz * seq
    xt = x.reshape(t, d)
    cos_t, sin_t = _rope_tables(seq)

    ffn_w_in16, ffn_w_out16 = ffn_w_in.astype(BF16), ffn_w_out.astype(BF16)

    for layer in range(depth):
        xt = _ffn(xt, ffn_norm[layer, 0], ffn_w_in16, ffn_w_out16, layer, 0, final_norm, False)
        x3 = xt.reshape(bsz, seq, d)
        if layer % 2 == 0:
            e = layer // 2
            x3 = _mix_ab(x3, mix_norm[layer], ab_w_in[e].astype(BF16), hgrn_lb_logits, hgrn_out_norm[e],
                         conv_w[e], conv_b[e], ab_w_out[e].astype(BF16), layer)
        else:
            o = layer // 2
            w = cd_w_in[o]
            n_ml = 4 * GROUP_W
            pad = jnp.zeros((d, LANES - N_HEADS), BF16)
            w_ml = w[:, :n_ml].astype(BF16)
            w_mb = w[:, n_ml + 2 * N_HEADS:].astype(BF16)
            w_gate = jnp.concatenate([w[:, n_ml:n_ml + N_HEADS].astype(BF16), pad,
                                      w[:, n_ml + N_HEADS:n_ml + 2 * N_HEADS].astype(BF16), pad], axis=1)
            bias_rows = jnp.pad(mlstm_gate_bias[o], ((0, 0), (0, LANES - N_HEADS)))
            ya, q_r, k_blocks, vt_tiles, k_mean, k_absmax = _mix_cd(
                x3, mix_norm[layer], w_ml, w_mb, w_gate, bias_rows, mlstm_out_norm[o], cos_t, sin_t)
            n_blk = seq // MB_BLOCK
            x3 = _moba(q_r, k_blocks.reshape(bsz, N_HEADS, seq // MB_TILE, MB_TILE, D_HEAD), vt_tiles,
                       k_mean.reshape(bsz, n_blk, GROUP_W), k_absmax.reshape(bsz, n_blk, GROUP_W),
                       x3, ya, cd_w_out[o].astype(BF16))
        xt = x3.reshape(t, d)
        xt = _ffn(xt, ffn_norm[layer, 1], ffn_w_in16, ffn_w_out16, layer, 1, final_norm, layer == depth - 1)
    return xt.reshape(bsz, seq, d)
```

```python
import functools

import jax
import jax.numpy as jnp
from jax import lax
from jax.experimental import pallas as pl
from jax.experimental.pallas import tpu as pltpu

F32 = jnp.float32
BF16 = jnp.bfloat16

D_MODEL = 1024
D_FF = 2816
GROUP_W = 512
N_HEADS = 4
D_HEAD = 128
RMS_EPS = 1e-6
HG_CHUNK = 32
ML_CHUNK = 64
MB_BLOCK = 256
MB_TOPK = 3
CONV_W = 3
ROPE_THETA = 500000.0
ROPE_DIM = D_HEAD // 4

LANES = 128
VMEM_LIMIT = 48 * 1024 * 1024

FFN_TM = 1024
FFN_SUB = 512
FFN_CH = 256
PROJ_CH = 256
SEQ_T = 256
MB_TILE = 2 * MB_BLOCK
MB_BOUND_WINDOW = 80.0
LOG2E = 1.4426950408889634

AB_COLS = 7 * GROUP_W
CD_COLS = 7 * GROUP_W + 2 * LANES


def _rms(x, gain):
    return x * lax.rsqrt(jnp.mean(x * x, axis=-1, keepdims=True) + RMS_EPS) * gain


def _dot(a, b):
    return jnp.dot(a, b, preferred_element_type=F32)


def _dot_nt(a, b):
    return lax.dot_general(a, b, (((1,), (1,)), ((), ())), preferred_element_type=F32)


def _dot_mask(mask01, x):
    hi = x.astype(BF16)
    lo = (x - hi.astype(F32)).astype(BF16)
    return _dot(mask01, hi) + _dot(mask01, lo)


def _chunk_last(x, chunk):
    rows = x.shape[0]
    return jnp.concatenate([jnp.broadcast_to(x[r0 + chunk - 1:r0 + chunk, :], (chunk, x.shape[1]))
                            for r0 in range(0, rows, chunk)], axis=0)


def _chunk_masks(n, chunk):
    r = lax.broadcasted_iota(jnp.int32, (n, n), 0)
    c = lax.broadcasted_iota(jnp.int32, (n, n), 1)
    same = (r // chunk) == (c // chunk)
    return jnp.where(same, jnp.where(c <= r, 1, 0), 0) > 0, same


def _expand_mask(n, chunk, width):
    n_c = n // chunk
    r = lax.broadcasted_iota(jnp.int32, (n, n_c * width), 0)
    c = lax.broadcasted_iota(jnp.int32, (n, n_c * width), 1)
    return (r // chunk) == (c // width)


def _ffn_body(x_ref, g_ref, wg_ref, wu_ref, wo_ref, fg_ref, o_ref, act_ref, *, final_norm):
    for r0 in range(0, FFN_TM, FFN_SUB):
        rows = pl.ds(r0, FFN_SUB)
        h = _rms(x_ref[rows, :], g_ref[...]).astype(BF16)
        for c0 in range(0, D_FF, FFN_CH):
            gate = _dot(h, wg_ref[:, c0:c0 + FFN_CH])
            up = _dot(h, wu_ref[:, c0:c0 + FFN_CH])
            act_ref[rows, c0:c0 + FFN_CH] = (gate * jax.nn.sigmoid(gate) * up).astype(BF16)
        y = x_ref[rows, :] + 0.5 * _dot(act_ref[rows, :], wo_ref[...])
        if final_norm:
            y = _rms(y, fg_ref[...])
        o_ref[rows, :] = y


def _ffn(x, gain, w_in, w_out, layer, which, final_gain, final_norm):
    t = x.shape[0]
    once = pl.Buffered(1)
    return pl.pallas_call(
        functools.partial(_ffn_body, final_norm=final_norm),
        out_shape=jax.ShapeDtypeStruct((t, D_MODEL), F32),
        grid=(t // FFN_TM,),
        in_specs=[
            pl.BlockSpec((FFN_TM, D_MODEL), lambda i: (i, 0)),
            pl.BlockSpec((1, D_MODEL), lambda i: (0, 0)),
            pl.BlockSpec((None, None, D_MODEL, D_FF), lambda i: (layer, which, 0, 0), pipeline_mode=once),
            pl.BlockSpec((None, None, D_MODEL, D_FF), lambda i: (layer, which, 0, 1), pipeline_mode=once),
            pl.BlockSpec((None, None, D_FF, D_MODEL), lambda i: (layer, which, 0, 0), pipeline_mode=once),
            pl.BlockSpec((1, D_MODEL), lambda i: (0, 0)),
        ],
        out_specs=pl.BlockSpec((FFN_TM, D_MODEL), lambda i: (i, 0)),
        scratch_shapes=[pltpu.VMEM((FFN_TM, D_FF), BF16)],
        compiler_params=pltpu.CompilerParams(
            dimension_semantics=("parallel",), vmem_limit_bytes=VMEM_LIMIT),
        name="ffn",
    )(x, gain.reshape(1, D_MODEL), w_in, w_in, w_out, final_gain.reshape(1, D_MODEL))


class _ChunkedProjection:
    def __init__(self, x_ref, g_ref, w_refs, p_ref):
        self.h = _rms(x_ref[...], g_ref[...]).astype(BF16)
        self.p_ref = p_ref
        self.todo = [(w_ref, c0) for w_ref in w_refs for c0 in range(0, w_ref.shape[1], PROJ_CH)]
        self.done = 0

    def emit(self, n_chunks):
        end = len(self.todo) if n_chunks is None else min(len(self.todo), self.done + n_chunks)
        for i in range(self.done, end):
            w_ref, c0 = self.todo[i]
            self.p_ref[:, i * PROJ_CH:(i + 1) * PROJ_CH] = _dot(self.h, w_ref[:, c0:c0 + PROJ_CH])
        self.done = end


def _project_in(x_ref, g_ref, w_refs, p_ref):
    _ChunkedProjection(x_ref, g_ref, w_refs, p_ref).emit(None)


def _with_pipelined_projection(x_ref, xn_ref, g_ref, w_refs, pa_ref, pb_ref, tile_fn):
    s_idx = pl.program_id(1)

    @pl.when(s_idx == 0)
    def _():
        _project_in(x_ref, g_ref, w_refs, pa_ref)

    def branch(parity, p_cur, p_nxt):
        @pl.when(s_idx % 2 == parity)
        def _():
            ahead = _ChunkedProjection(xn_ref, g_ref, w_refs, p_nxt)
            tile_fn(p_cur, ahead)
            ahead.emit(None)

    branch(0, pa_ref, pb_ref)
    branch(1, pb_ref, pa_ref)


def _next_tile(n_s):
    return lambda b, s: (b, jnp.minimum(s + 1, n_s - 1), 0)


def _mix_ab_body(x_ref, xn_ref, g_ref, w_ref, lbl_ref, hgn_ref, cw_ref, cb_ref, wo_ref, o_ref,
                 pa_ref, pb_ref, y_ref, st_ref, zb_ref, *, layer):
    tile = functools.partial(_mix_ab_tile, x_ref=x_ref, lbl_ref=lbl_ref, hgn_ref=hgn_ref, cw_ref=cw_ref,
                             cb_ref=cb_ref, wo_ref=wo_ref, o_ref=o_ref, y_ref=y_ref, st_ref=st_ref,
                             zb_ref=zb_ref, layer=layer)
    _with_pipelined_projection(x_ref, xn_ref, g_ref, (w_ref,), pa_ref, pb_ref, tile)


def _mix_ab_tile(p_ref, ahead, *, x_ref, lbl_ref, hgn_ref, cw_ref, cb_ref, wo_ref, o_ref, y_ref, st_ref, zb_ref, layer):
    s_idx = pl.program_id(1)
    n_c = SEQ_T // HG_CHUNK

    @pl.when(s_idx == 0)
    def _():
        st_ref[...] = jnp.zeros_like(st_ref)
        zb_ref[0:8, :] = jnp.zeros((8, GROUP_W), F32)

    lg = lbl_ref[...]
    ex = jnp.exp(lg - jnp.max(lg, axis=0, keepdims=True))
    sm = ex / jnp.sum(ex, axis=0, keepdims=True)
    lb = jnp.sum(sm[0:layer + 1, :], axis=0, keepdims=True)

    tril, _ = _chunk_masks(SEQ_T, HG_CHUNK)
    tril01 = jnp.where(tril, 1.0, 0.0).astype(BF16)
    emask = _expand_mask(SEQ_T, HG_CHUNK, D_HEAD)

    f = lb + (1.0 - lb) * jax.nn.sigmoid(p_ref[:, GROUP_W:2 * GROUP_W])
    logf = jnp.log(f)
    b_all = _dot_mask(tril01, logf)
    e_all = _chunk_last(b_all, HG_CHUNK)
    ahead.emit(2)

    for h in range(N_HEADS):
        ahead.emit(1)
        lo, hi = h * D_HEAD, (h + 1) * D_HEAD
        q = p_ref[:, lo:hi]
        v = p_ref[:, 2 * GROUP_W + lo:2 * GROUP_W + hi]
        g = p_ref[:, 3 * GROUP_W + lo:3 * GROUP_W + hi]
        b = b_all[:, lo:hi]
        e = e_all[:, lo:hi]
        kk = 1.0 - f[:, lo:hi]
        q_dec = (q * jax.nn.sigmoid(q) * jnp.exp(b)).astype(BF16)
        k_dec = (kk * jnp.exp(-b)).astype(BF16)
        k_end = (kk * jnp.exp(e - b)).astype(BF16)
        v_t = v.T.astype(BF16)

        attn = jnp.where(tril, _dot_nt(q_dec, k_dec), 0.0)
        ahead.emit(1)
        o_t = _dot_nt(v_t, attn.astype(BF16))

        k_exp = jnp.where(emask, jnp.tile(k_end, (1, n_c)), jnp.zeros((), BF16))
        d_all = _dot(v_t, k_exp)
        ahead.emit(1)

        st = st_ref[h]
        prev = []
        for c in range(n_c):
            prev.append(st)
            decay = jnp.exp(e[c * HG_CHUNK:c * HG_CHUNK + 1, :])
            st = decay * st + d_all[:, c * D_HEAD:(c + 1) * D_HEAD]
        st_ref[h] = st
        s_prev = jnp.concatenate(prev, axis=1).astype(BF16)
        q_exp = jnp.where(emask, jnp.tile(q_dec, (1, n_c)), jnp.zeros((), BF16))
        o = (o_t + _dot_nt(s_prev, q_exp)).T

        o = _rms(o, hgn_ref[:, lo:hi]) * (g * jax.nn.sigmoid(g))
        y_ref[:, lo:hi] = o.astype(BF16)

    z = p_ref[:, 5 * GROUP_W:6 * GROUP_W] * p_ref[:, 6 * GROUP_W:7 * GROUP_W]
    zb_ref[8:SEQ_T + 8, :] = z
    z1 = zb_ref[7:SEQ_T + 7, :]
    z2 = zb_ref[6:SEQ_T + 6, :]
    y = cb_ref[...] + cw_ref[0:1, :] * z2 + cw_ref[1:2, :] * z1 + cw_ref[2:3, :] * z
    y_ref[:, GROUP_W:] = (p_ref[:, 4 * GROUP_W:5 * GROUP_W] * y).astype(BF16)
    zb_ref[0:8, :] = zb_ref[SEQ_T:SEQ_T + 8, :]

    o_ref[...] = x_ref[...] + _dot(y_ref[...], wo_ref[...])


def _mix_ab(x, gain, w_in, lb_logits, hg_norm, conv_w, conv_b, w_out, layer):
    bsz, seq = x.shape[0], x.shape[1]
    n_l = lb_logits.shape[0]
    row = lambda b, s: (0, 0)
    once = pl.Buffered(1)
    return pl.pallas_call(
        functools.partial(_mix_ab_body, layer=layer),
        out_shape=jax.ShapeDtypeStruct((bsz, seq, D_MODEL), F32),
        grid=(bsz, seq // SEQ_T),
        in_specs=[
            pl.BlockSpec((None, SEQ_T, D_MODEL), lambda b, s: (b, s, 0)),
            pl.BlockSpec((None, SEQ_T, D_MODEL), _next_tile(seq // SEQ_T)),
            pl.BlockSpec((1, D_MODEL), row),
            pl.BlockSpec((D_MODEL, AB_COLS), row, pipeline_mode=once),
            pl.BlockSpec((n_l, GROUP_W), row),
            pl.BlockSpec((1, GROUP_W), row),
            pl.BlockSpec((CONV_W, GROUP_W), row),
            pl.BlockSpec((1, GROUP_W), row),
            pl.BlockSpec((D_MODEL, D_MODEL), row, pipeline_mode=once),
        ],
        out_specs=pl.BlockSpec((None, SEQ_T, D_MODEL), lambda b, s: (b, s, 0)),
        scratch_shapes=[pltpu.VMEM((SEQ_T, AB_COLS), F32), pltpu.VMEM((SEQ_T, AB_COLS), F32),
                        pltpu.VMEM((SEQ_T, D_MODEL), BF16),
                        pltpu.VMEM((N_HEADS, D_HEAD, D_HEAD), F32), pltpu.VMEM((SEQ_T + 8, GROUP_W), F32)],
        compiler_params=pltpu.CompilerParams(
            dimension_semantics=("arbitrary", "arbitrary"), vmem_limit_bytes=VMEM_LIMIT),
        name="mix_ab",
    )(x, x, gain.reshape(1, D_MODEL), w_in, lb_logits, hg_norm.reshape(1, GROUP_W), conv_w,
      conv_b.reshape(1, GROUP_W), w_out)


def _mix_cd_body(x_ref, xn_ref, g_ref, w_ml_ref, w_mb_ref, w_gate_ref, bias_ref, mln_ref, cos_ref, sin_ref,
                 o_ref, qo_ref, ko_ref, vt_ref, km_ref, ka_ref, pa_ref, pb_ref, c_ref, n_ref, m_ref):
    tile = functools.partial(_mix_cd_tile, bias_ref=bias_ref, mln_ref=mln_ref, cos_ref=cos_ref, sin_ref=sin_ref,
                             o_ref=o_ref, qo_ref=qo_ref, ko_ref=ko_ref, vt_ref=vt_ref, km_ref=km_ref,
                             ka_ref=ka_ref, c_ref=c_ref, n_ref=n_ref, m_ref=m_ref)
    _with_pipelined_projection(x_ref, xn_ref, g_ref, (w_ml_ref, w_mb_ref, w_gate_ref), pa_ref, pb_ref, tile)


def _mix_cd_tile(p_ref, ahead, *, bias_ref, mln_ref, cos_ref, sin_ref, o_ref, qo_ref, ko_ref, vt_ref, km_ref, ka_ref,
                 c_ref, n_ref, m_ref):
    s_idx = pl.program_id(1)
    n_c = SEQ_T // ML_CHUNK
    gi_ref = p_ref.at[:, 7 * GROUP_W:7 * GROUP_W + LANES]
    gf_ref = p_ref.at[:, 7 * GROUP_W + LANES:7 * GROUP_W + 2 * LANES]

    @pl.when(s_idx == 0)
    def _():
        c_ref[...] = jnp.zeros_like(c_ref)
        n_ref[...] = jnp.zeros_like(n_ref)
        m_ref[...] = jnp.zeros_like(m_ref)

    cos_t, sin_t = cos_ref[...], sin_ref[...]
    for h in range(N_HEADS):
        ahead.emit(1)
        lo, hi = h * D_HEAD, (h + 1) * D_HEAD
        q = _rope(p_ref[:, 4 * GROUP_W + lo:4 * GROUP_W + hi], cos_t, sin_t) * (D_HEAD ** -0.5 * LOG2E)
        k = _rope(p_ref[:, 5 * GROUP_W + lo:5 * GROUP_W + hi], cos_t, sin_t)
        qo_ref[:, lo:hi] = q.astype(BF16)
        ko_ref[h] = k.astype(BF16)
        vt_ref[h] = p_ref[:, 6 * GROUP_W + lo:6 * GROUP_W + hi].T.astype(BF16)
        km_ref[:, lo:hi] = jnp.mean(k, axis=0, keepdims=True)
        ka_ref[:, lo:hi] = jnp.max(jnp.abs(k), axis=0, keepdims=True)

    tril, same = _chunk_masks(SEQ_T, ML_CHUNK)
    tril01 = jnp.where(tril, 1.0, 0.0).astype(BF16)
    same01 = jnp.where(same, 1.0, 0.0).astype(BF16)
    emask = _expand_mask(SEQ_T, ML_CHUNK, D_HEAD)

    log_i = gi_ref[...] + bias_ref[0:1, :]
    log_f = jax.nn.log_sigmoid(gf_ref[...] + bias_ref[1:2, :])
    b_col = _dot_mask(tril01, log_f)
    e_col = _dot_mask(same01, log_f)
    w_end = e_col - b_col + log_i

    m = m_ref[...]
    m_prev_rows, m_new_rows, a_rows = [], [], []
    for c in range(n_c):
        r0 = c * ML_CHUNK
        be = e_col[r0:r0 + 1, :]
        m_end = jnp.max(w_end[r0:r0 + ML_CHUNK, :], axis=0, keepdims=True)
        m_new = jnp.maximum(be + m, m_end)
        a_rows.append(jnp.exp(be + m - m_new))
        m_prev_rows.append(jnp.broadcast_to(m, (ML_CHUNK, LANES)))
        m_new_rows.append(jnp.broadcast_to(m_new, (ML_CHUNK, LANES)))
        m = m_new
    m_ref[...] = m
    log_inter = b_col + jnp.concatenate(m_prev_rows, axis=0)
    wk_scale = jnp.exp(w_end - jnp.concatenate(m_new_rows, axis=0))
    r_rows = (b_col - log_i).T

    for h in range(N_HEADS):
        ahead.emit(3)
        lo, hi = h * D_HEAD, (h + 1) * D_HEAD
        q = p_ref[:, lo:hi] * (D_HEAD ** -0.5)
        k = p_ref[:, GROUP_W + lo:GROUP_W + hi]
        v = p_ref[:, 2 * GROUP_W + lo:2 * GROUP_W + hi]
        og = p_ref[:, 3 * GROUP_W + lo:3 * GROUP_W + hi]
        q16, k16, v16 = q.astype(BF16), k.astype(BF16), v.astype(BF16)

        d_log = jnp.where(tril, b_col[:, h:h + 1] - r_rows[h:h + 1, :], -jnp.inf)
        linter = log_inter[:, h:h + 1]
        m_t = jnp.maximum(linter, jnp.max(d_log, axis=-1, keepdims=True))
        w = jnp.exp(d_log - m_t) * _dot_nt(q16, k16)
        a_in = jnp.exp(linter - m_t)
        num = _dot(w.astype(BF16), v16)
        den = jnp.sum(w, axis=-1, keepdims=True)

        wk = wk_scale[:, h:h + 1] * k
        v_exp = jnp.where(emask, jnp.tile(v16, (1, n_c)), jnp.zeros((), BF16))
        dc_all = _dot(wk.T.astype(BF16), v_exp)

        c_mat = c_ref[h]
        n_vec = n_ref[h]
        c_prev, n_prev = [], []
        for c in range(n_c):
            r0 = c * ML_CHUNK
            c_prev.append(c_mat)
            n_prev.append(jnp.broadcast_to(n_vec, (ML_CHUNK, D_HEAD)))
            a = a_rows[c][:, h:h + 1]
            c_mat = a * c_mat + dc_all[:, c * D_HEAD:(c + 1) * D_HEAD]
            n_vec = a * n_vec + jnp.sum(wk[r0:r0 + ML_CHUNK, :], axis=0, keepdims=True)
        c_ref[h] = c_mat
        n_ref[h] = n_vec

        q_exp = jnp.where(emask, jnp.tile(q16, (1, n_c)), jnp.zeros((), BF16))
        q_c = _dot(q_exp, jnp.concatenate(c_prev, axis=0).astype(BF16))
        q_n = jnp.sum(q * jnp.concatenate(n_prev, axis=0), axis=-1, keepdims=True)
        num = num + a_in * q_c
        den = den + a_in * q_n
        hh = num / jnp.maximum(jnp.abs(den), jnp.exp(-m_t))
        o_ref[:, lo:hi] = (_rms(hh, mln_ref[:, lo:hi]) * jax.nn.sigmoid(og)).astype(BF16)


def _mix_cd(x, gain, w_ml, w_mb, w_gate, bias_rows, ml_norm, cos_t, sin_t):
    bsz, seq = x.shape[0], x.shape[1]
    n_blk, n_tile = seq // MB_BLOCK, seq // MB_TILE
    assert SEQ_T == MB_BLOCK and MB_TILE == 2 * MB_BLOCK
    row = lambda b, s: (0, 0)
    stat = pl.BlockSpec((None, None, 1, GROUP_W), lambda b, s: (b, s, 0, 0))
    return pl.pallas_call(
        _mix_cd_body,
        out_shape=[
            jax.ShapeDtypeStruct((bsz, seq, GROUP_W), BF16),
            jax.ShapeDtypeStruct((bsz, seq, GROUP_W), BF16),
            jax.ShapeDtypeStruct((bsz, N_HEADS, n_tile, MB_TILE, D_HEAD), BF16),
            jax.ShapeDtypeStruct((bsz, N_HEADS, n_tile, D_HEAD, MB_TILE), BF16),
            jax.ShapeDtypeStruct((bsz, n_blk, 1, GROUP_W), F32),
            jax.ShapeDtypeStruct((bsz, n_blk, 1, GROUP_W), F32),
        ],
        grid=(bsz, seq // SEQ_T),
        in_specs=[
            pl.BlockSpec((None, SEQ_T, D_MODEL), lambda b, s: (b, s, 0)),
            pl.BlockSpec((None, SEQ_T, D_MODEL), _next_tile(seq // SEQ_T)),
            pl.BlockSpec((1, D_MODEL), row),
            pl.BlockSpec(w_ml.shape, row, pipeline_mode=pl.Buffered(1)),
            pl.BlockSpec(w_mb.shape, row, pipeline_mode=pl.Buffered(1)),
            pl.BlockSpec(w_gate.shape, row, pipeline_mode=pl.Buffered(1)),
            pl.BlockSpec((2, LANES), row),
            pl.BlockSpec((1, GROUP_W), row),
            pl.BlockSpec((SEQ_T, LANES), lambda b, s: (s, 0)),
            pl.BlockSpec((SEQ_T, LANES), lambda b, s: (s, 0)),
        ],
        out_specs=[
            pl.BlockSpec((None, SEQ_T, GROUP_W), lambda b, s: (b, s, 0)),
            pl.BlockSpec((None, SEQ_T, GROUP_W), lambda b, s: (b, s, 0)),
            pl.BlockSpec((None, N_HEADS, None, MB_BLOCK, D_HEAD), lambda b, s: (b, 0, s // 2, s % 2, 0)),
            pl.BlockSpec((None, N_HEADS, None, D_HEAD, MB_BLOCK), lambda b, s: (b, 0, s // 2, 0, s % 2)),
            stat, stat,
        ],
        scratch_shapes=[pltpu.VMEM((SEQ_T, CD_COLS), F32), pltpu.VMEM((SEQ_T, CD_COLS), F32),
                        pltpu.VMEM((N_HEADS, D_HEAD, D_HEAD), F32),
                        pltpu.VMEM((N_HEADS, 1, D_HEAD), F32),
                        pltpu.VMEM((1, LANES), F32)],
        compiler_params=pltpu.CompilerParams(
            dimension_semantics=("arbitrary", "arbitrary"), vmem_limit_bytes=VMEM_LIMIT),
        name="mix_cd",
    )(x, x, gain.reshape(1, D_MODEL), w_ml, w_mb, w_gate, bias_rows, ml_norm.reshape(1, GROUP_W), cos_t, sin_t)


def _rope(x, cos_t, sin_t):
    lane = lax.broadcasted_iota(jnp.int32, x.shape, 1)
    half = ROPE_DIM // 2
    swapped = jnp.where(lane < half, pltpu.roll(x, LANES - half, axis=1), pltpu.roll(x, half, axis=1))
    return x * cos_t + swapped * sin_t


def _moba_body(q_ref, k_ref, vt_ref, km_ref, ka_ref, x_ref, ya_ref, wo_ref, o_ref,
               sel_ref, sd_ref, acc_ref, yb_ref, *, n_blk):
    t_own = pl.program_id(1)
    heads = [(h, h * D_HEAD, (h + 1) * D_HEAD) for h in range(N_HEADS)]
    bk = MB_BLOCK
    o_ref[...] = x_ref[...] + _dot(ya_ref[...], wo_ref[:GROUP_W, :])

    blk = lax.broadcasted_iota(jnp.int32, (n_blk, MB_TILE), 0)
    lane = lax.broadcasted_iota(jnp.int32, (n_blk, MB_TILE), 1)
    own = 2 * t_own + lane // bk
    bound_past = []
    for h, lo, hi in heads:
        q = q_ref[:, lo:hi]
        km = km_ref[:, lo:hi]
        km_hi = km.astype(BF16)
        km_lo = (km - km_hi.astype(F32)).astype(BF16)
        gate = _dot_nt(km_hi, q) + _dot_nt(km_lo, q)
        gate = jnp.where(blk < own, gate, -jnp.inf)
        sel = jnp.zeros(gate.shape, F32)
        for _ in range(MB_TOPK):
            mx = jnp.max(gate, axis=0, keepdims=True)
            idx = jnp.min(jnp.where(gate == mx, blk, n_blk), axis=0, keepdims=True)
            pick = blk == jnp.where(mx > -jnp.inf, idx, -1)
            sel = jnp.where(pick, 1.0, sel)
            gate = jnp.where(pick, -jnp.inf, gate)
        sel_ref[h] = sel
        bound = _dot_nt(ka_ref[:, lo:hi].astype(BF16), jnp.abs(q)) * (1.0 + 2.0 ** -6)
        bound_past.append(jnp.max(jnp.where(sel > 0.0, bound, -jnp.inf), axis=0, keepdims=True))

    def live_rows(h, t):
        return sel_ref[h, pl.ds(2 * t, 1), :] > 0.0, sel_ref[h, pl.ds(2 * t + 1, 1), :] > 0.0

    kpos = lax.broadcasted_iota(jnp.int32, (MB_TILE, MB_TILE), 0)
    qpos = lax.broadcasted_iota(jnp.int32, (MB_TILE, MB_TILE), 1)
    kb, qb = kpos // bk, qpos // bk
    causal = jnp.where(kb == qb, jnp.where(kpos <= qpos, 1.0, 0.0), 0.0)
    cross = jnp.where(kb < qb, 1.0, 0.0)

    def visible(h):
        return (causal + cross * sel_ref[h, pl.ds(2 * t_own, 1), :]) > 0.0

    m_diag = []
    for h, lo, hi in heads:
        s = _dot_nt(k_ref[h, t_own], q_ref[:, lo:hi])
        sd_ref[h] = s
        m_diag.append(jnp.max(jnp.where(visible(h), s, -jnp.inf), axis=0, keepdims=True))

    def exact_past_max():
        def tile_max(t, ms):
            out = []
            for h, lo, hi in heads:
                s = _dot_nt(k_ref[h, t], q_ref[:, lo:hi])
                live_a, live_b = live_rows(h, t)
                m_a = jnp.where(live_a, jnp.max(s[:bk], axis=0, keepdims=True), -jnp.inf)
                m_b = jnp.where(live_b, jnp.max(s[bk:], axis=0, keepdims=True), -jnp.inf)
                out.append(jnp.maximum(ms[h], jnp.maximum(m_a, m_b)))
            return tuple(out)

        return lax.fori_loop(0, t_own, tile_max, tuple(jnp.full((1, MB_TILE), -jnp.inf, F32) for _ in heads))

    slack = functools.reduce(jnp.maximum, [jnp.max(bound_past[h] - m_diag[h]) for h, _, _ in heads])
    m_past = lax.cond(slack > MB_BOUND_WINDOW, exact_past_max, lambda: tuple(bound_past))

    m_row, l0 = [], []
    for h, lo, hi in heads:
        m_h = jnp.maximum(m_diag[h], m_past[h])
        p = jnp.where(visible(h), jnp.exp2(sd_ref[h] - m_h), 0.0)
        m_row.append(m_h)
        l0.append(jnp.sum(p, axis=0, keepdims=True))
        acc_ref[h] = _dot(vt_ref[h, t_own], p.astype(BF16))

    def accumulate(tiles, ls):
        work = [(t, h, lo, hi) for t in tiles for h, lo, hi in heads]
        score = lambda t, h, lo, hi: _dot_nt(k_ref[h, t], q_ref[:, lo:hi])
        ls = list(ls)
        scores = score(*work[0])
        for i, (t, h, lo, hi) in enumerate(work):
            s = scores
            if i + 1 < len(work):
                scores = score(*work[i + 1])
            e = jnp.exp2(s - m_row[h])
            live_a, live_b = live_rows(h, t)
            p = jnp.concatenate([jnp.where(live_a, e[:bk], 0.0), jnp.where(live_b, e[bk:], 0.0)], axis=0)
            ls[h] = ls[h] + jnp.sum(p, axis=0, keepdims=True)
            acc_ref[h] += _dot(vt_ref[h, t], p.astype(BF16))
        return tuple(ls)

    l_f = lax.fori_loop(0, t_own // 2, lambda i, ls: accumulate([2 * i, 2 * i + 1], ls), tuple(l0))
    l_f = lax.cond(t_own % 2 == 1, lambda ls: accumulate([t_own - 1], ls), lambda ls: ls, l_f)
    for h, lo, hi in heads:
        yb_ref[:, lo:hi] = (acc_ref[h] / l_f[h]).T.astype(BF16)

    o_ref[...] += _dot(yb_ref[...], wo_ref[GROUP_W:, :])


def _moba(q, k_tiles, vt_tiles, k_mean, k_absmax, x, ya, w_out):
    bsz, seq = q.shape[0], q.shape[1]
    n_blk, n_tile = seq // MB_BLOCK, seq // MB_TILE
    once = pl.Buffered(1)
    tile = lambda b, i: (b, i, 0)
    return pl.pallas_call(
        functools.partial(_moba_body, n_blk=n_blk),
        out_shape=jax.ShapeDtypeStruct((bsz, seq, D_MODEL), F32),
        grid=(bsz, n_tile),
        in_specs=[
            pl.BlockSpec((None, MB_TILE, GROUP_W), tile),
            pl.BlockSpec((None, N_HEADS, n_tile, MB_TILE, D_HEAD), lambda b, i: (b, 0, 0, 0, 0),
                         pipeline_mode=once),
            pl.BlockSpec((None, N_HEADS, n_tile, D_HEAD, MB_TILE), lambda b, i: (b, 0, 0, 0, 0),
                         pipeline_mode=once),
            pl.BlockSpec((None, n_blk, GROUP_W), lambda b, i: (b, 0, 0)),
            pl.BlockSpec((None, n_blk, GROUP_W), lambda b, i: (b, 0, 0)),
            pl.BlockSpec((None, MB_TILE, D_MODEL), tile),
            pl.BlockSpec((None, MB_TILE, GROUP_W), tile),
            pl.BlockSpec((D_MODEL, D_MODEL), lambda b, i: (0, 0), pipeline_mode=once),
        ],
        out_specs=pl.BlockSpec((None, MB_TILE, D_MODEL), tile),
        scratch_shapes=[pltpu.VMEM((N_HEADS, n_blk, MB_TILE), F32),
                        pltpu.VMEM((N_HEADS, MB_TILE, MB_TILE), F32),
                        pltpu.VMEM((N_HEADS, D_HEAD, MB_TILE), F32),
                        pltpu.VMEM((MB_TILE, GROUP_W), BF16)],
        compiler_params=pltpu.CompilerParams(
            dimension_semantics=("parallel", "arbitrary"), vmem_limit_bytes=VMEM_LIMIT),
        name="moba",
    )(q, k_tiles, vt_tiles, k_mean, k_absmax, x, ya, w_out)


def _rope_tables(seq):
    half = ROPE_DIM // 2
    inv_freq = jnp.float32(ROPE_THETA) ** (-jnp.arange(half, dtype=F32) * 2.0 / ROPE_DIM)
    ang = jnp.arange(seq).astype(F32)[:, None] * inv_freq[None, :]
    cos, sin = jnp.cos(ang), jnp.sin(ang)
    rest = LANES - ROPE_DIM
    cos_t = jnp.concatenate([cos, cos, jnp.ones((seq, rest), F32)], axis=1)
    sin_t = jnp.concatenate([-sin, sin, jnp.zeros((seq, rest), F32)], axis=1)
    return cos_t, sin_t


def kernel(x, ffn_norm, ffn_w_in, ffn_w_out, mix_norm, ab_w_in, ab_w_out, hgrn_lb_logits, hgrn_out_norm,
           conv_w, conv_b, cd_w_in, cd_w_out, mlstm_gate_bias, mlstm_out_norm, final_norm):
    bsz, seq, d = x.shape
    depth = ffn_norm.shape[0]
    t = bsz * seq
    xt = x.reshape(t, d)
    cos_t, sin_t = _rope_tables(seq)

    ffn_w_in16, ffn_w_out16 = ffn_w_in.astype(BF16), ffn_w_out.astype(BF16)

    for layer in range(depth):
        xt = _ffn(xt, ffn_norm[layer, 0], ffn_w_in16, ffn_w_out16, layer, 0, final_norm, False)
        x3 = xt.reshape(bsz, seq, d)
        if layer % 2 == 0:
            e = layer // 2
            x3 = _mix_ab(x3, mix_norm[layer], ab_w_in[e].astype(BF16), hgrn_lb_logits, hgrn_out_norm[e],
                         conv_w[e], conv_b[e], ab_w_out[e].astype(BF16), layer)
        else:
            o = layer // 2
            w = cd_w_in[o]
            n_ml = 4 * GROUP_W
            pad = jnp.zeros((d, LANES - N_HEADS), BF16)
            w_ml = w[:, :n_ml].astype(BF16)
            w_mb = w[:, n_ml + 2 * N_HEADS:].astype(BF16)
            w_gate = jnp.concatenate([w[:, n_ml:n_ml + N_HEADS].astype(BF16), pad,
                                      w[:, n_ml + N_HEADS:n_ml + 2 * N_HEADS].astype(BF16), pad], axis=1)
            bias_rows = jnp.pad(mlstm_gate_bias[o], ((0, 0), (0, LANES - N_HEADS)))
            ya, q_r, k_tiles, vt_tiles, k_mean, k_absmax = _mix_cd(
                x3, mix_norm[layer], w_ml, w_mb, w_gate, bias_rows, mlstm_out_norm[o], cos_t, sin_t)
            n_blk = seq // MB_BLOCK
            x3 = _moba(q_r, k_tiles, vt_tiles,
                       k_mean.reshape(bsz, n_blk, GROUP_W), k_absmax.reshape(bsz, n_blk, GROUP_W),
                       x3, ya, cd_w_out[o].astype(BF16))
        xt = x3.reshape(t, d)
        xt = _ffn(xt, ffn_norm[layer, 1], ffn_w_in16, ffn_w_out16, layer, 1, final_norm, layer == depth - 1)
    return xt.reshape(bsz, seq, d)
```

```python
import functools

import jax
import jax.numpy as jnp
import numpy as np
from jax import lax
from jax.experimental import pallas as pl
from jax.experimental.pallas import tpu as pltpu

F32 = jnp.float32
BF16 = jnp.bfloat16

D_MODEL = 1024
D_FF = 2816
GROUP_W = 512
N_HEADS = 4
D_HEAD = 128
RMS_EPS = 1e-6
HG_CHUNK = 32
ML_CHUNK = 64
MB_BLOCK = 256
MB_TOPK = 3
CONV_W = 3
ROPE_THETA = 500000.0
ROPE_DIM = D_HEAD // 4

LANES = 128
VMEM_LIMIT = 48 * 1024 * 1024

FFN_TM = 1024
FFN_SUB = 512
FFN_CH = 256
PROJ_CH = 256
SEQ_T = 256
MB_TILE = 2 * MB_BLOCK
MB_BOUND_WINDOW = 80.0
LOG2E = 1.4426950408889634

AB_COLS = 7 * GROUP_W
CD_COLS = 7 * GROUP_W + 2 * LANES


def _rms(x, gain):
    return x * lax.rsqrt(jnp.mean(x * x, axis=-1, keepdims=True) + RMS_EPS) * gain


def _dot(a, b):
    return jnp.dot(a, b, preferred_element_type=F32)


def _dot_nt(a, b):
    return lax.dot_general(a, b, (((1,), (1,)), ((), ())), preferred_element_type=F32)


def _dot_mask(mask01, x):
    hi = x.astype(BF16)
    lo = (x - hi.astype(F32)).astype(BF16)
    return _dot(mask01, hi) + _dot(mask01, lo)


def _chunk_last(x, chunk):
    rows = x.shape[0]
    return jnp.concatenate([jnp.broadcast_to(x[r0 + chunk - 1:r0 + chunk, :], (chunk, x.shape[1]))
                            for r0 in range(0, rows, chunk)], axis=0)


def _chunk_masks(n, chunk):
    r = lax.broadcasted_iota(jnp.int32, (n, n), 0)
    c = lax.broadcasted_iota(jnp.int32, (n, n), 1)
    same = (r // chunk) == (c // chunk)
    return jnp.where(same, jnp.where(c <= r, 1, 0), 0) > 0, same


def _expand_mask(n, chunk, width):
    n_c = n // chunk
    r = lax.broadcasted_iota(jnp.int32, (n, n_c * width), 0)
    c = lax.broadcasted_iota(jnp.int32, (n, n_c * width), 1)
    return (r // chunk) == (c // width)


def _stream_cast(src_chunks, dst_chunks, stage_ref, sem):
    def copy(k):
        return pltpu.make_async_copy(src_chunks[k], stage_ref.at[k % 2], sem.at[k % 2])

    copy(0).start()
    for k in range(len(src_chunks)):
        if k + 1 < len(src_chunks):
            copy(k + 1).start()
        copy(k).wait()
        dst_chunks[k][...] = stage_ref[k % 2].astype(BF16)


def _ffn_body(x_ref, g_ref, w_in_hbm, w_out_hbm, fg_ref, o_ref, wg_ref, wu_ref, wo_ref, act_ref,
              stage_in, stage_out, sem_in, sem_out, *, layer, which, final_norm):
    @pl.when(pl.program_id(0) == 0)
    def _():
        w_in, w_out = w_in_hbm.at[layer, which], w_out_hbm.at[layer, which]
        starts = range(0, D_FF, FFN_CH)
        _stream_cast([w_in.at[:, pl.ds(half * D_FF + c0, FFN_CH)] for half in (0, 1) for c0 in starts],
                     [w.at[:, pl.ds(c0, FFN_CH)] for w in (wg_ref, wu_ref) for c0 in starts], stage_in, sem_in)
        _stream_cast([w_out.at[pl.ds(c0, FFN_CH), :] for c0 in starts],
                     [wo_ref.at[pl.ds(c0, FFN_CH), :] for c0 in starts], stage_out, sem_out)

    for r0 in range(0, FFN_TM, FFN_SUB):
        rows = pl.ds(r0, FFN_SUB)
        h = _rms(x_ref[rows, :], g_ref[...]).astype(BF16)
        for c0 in range(0, D_FF, FFN_CH):
            gate = _dot(h, wg_ref[:, c0:c0 + FFN_CH])
            up = _dot(h, wu_ref[:, c0:c0 + FFN_CH])
            act_ref[rows, c0:c0 + FFN_CH] = (gate * jax.nn.sigmoid(gate) * up).astype(BF16)
        y = x_ref[rows, :] + 0.5 * _dot(act_ref[rows, :], wo_ref[...])
        if final_norm:
            y = _rms(y, fg_ref[...])
        o_ref[rows, :] = y


def _ffn(x, gain, w_in, w_out, layer, which, final_gain, final_norm):
    t = x.shape[0]
    return pl.pallas_call(
        functools.partial(_ffn_body, layer=layer, which=which, final_norm=final_norm),
        out_shape=jax.ShapeDtypeStruct((t, D_MODEL), F32),
        grid=(t // FFN_TM,),
        in_specs=[
            pl.BlockSpec((FFN_TM, D_MODEL), lambda i: (i, 0)),
            pl.BlockSpec((1, D_MODEL), lambda i: (0, 0)),
            pl.BlockSpec(memory_space=pl.ANY),
            pl.BlockSpec(memory_space=pl.ANY),
            pl.BlockSpec((1, D_MODEL), lambda i: (0, 0)),
        ],
        out_specs=pl.BlockSpec((FFN_TM, D_MODEL), lambda i: (i, 0)),
        scratch_shapes=[pltpu.VMEM((D_MODEL, D_FF), BF16), pltpu.VMEM((D_MODEL, D_FF), BF16),
                        pltpu.VMEM((D_FF, D_MODEL), BF16), pltpu.VMEM((FFN_TM, D_FF), BF16),
                        pltpu.VMEM((2, D_MODEL, FFN_CH), F32), pltpu.VMEM((2, FFN_CH, D_MODEL), F32),
                        pltpu.SemaphoreType.DMA((2,)), pltpu.SemaphoreType.DMA((2,))],
        compiler_params=pltpu.CompilerParams(
            dimension_semantics=("arbitrary",), vmem_limit_bytes=VMEM_LIMIT),
        name="ffn",
    )(x, gain.reshape(1, D_MODEL), w_in, w_out, final_gain.reshape(1, D_MODEL))


class _ChunkedProjection:
    def __init__(self, x_ref, g_ref, w_refs, p_ref):
        self.h = _rms(x_ref[...], g_ref[...]).astype(BF16)
        self.p_ref = p_ref
        self.todo = [(w_ref, c0) for w_ref in w_refs for c0 in range(0, w_ref.shape[1], PROJ_CH)]
        self.done = 0

    def emit(self, n_chunks):
        end = len(self.todo) if n_chunks is None else min(len(self.todo), self.done + n_chunks)
        for i in range(self.done, end):
            w_ref, c0 = self.todo[i]
            self.p_ref[:, i * PROJ_CH:(i + 1) * PROJ_CH] = _dot(self.h, w_ref[:, c0:c0 + PROJ_CH])
        self.done = end


def _project_in(x_ref, g_ref, w_refs, p_ref):
    _ChunkedProjection(x_ref, g_ref, w_refs, p_ref).emit(None)


def _with_pipelined_projection(x_ref, xn_ref, g_ref, w_refs, pa_ref, pb_ref, tile_fn):
    s_idx = pl.program_id(1)

    @pl.when(s_idx == 0)
    def _():
        _project_in(x_ref, g_ref, w_refs, pa_ref)

    def branch(parity, p_cur, p_nxt):
        @pl.when(s_idx % 2 == parity)
        def _():
            ahead = _ChunkedProjection(xn_ref, g_ref, w_refs, p_nxt)
            tile_fn(p_cur, ahead)
            ahead.emit(None)

    branch(0, pa_ref, pb_ref)
    branch(1, pb_ref, pa_ref)


def _next_tile(n_s):
    return lambda b, s: (b, jnp.minimum(s + 1, n_s - 1), 0)


def _mix_ab_body(x_ref, xn_ref, g_ref, w_ref, lbl_ref, hgn_ref, cw_ref, cb_ref, wo_ref, o_ref,
                 pa_ref, pb_ref, y_ref, st_ref, zb_ref, *, layer):
    tile = functools.partial(_mix_ab_tile, x_ref=x_ref, lbl_ref=lbl_ref, hgn_ref=hgn_ref, cw_ref=cw_ref,
                             cb_ref=cb_ref, wo_ref=wo_ref, o_ref=o_ref, y_ref=y_ref, st_ref=st_ref,
                             zb_ref=zb_ref, layer=layer)
    _with_pipelined_projection(x_ref, xn_ref, g_ref, (w_ref,), pa_ref, pb_ref, tile)


def _mix_ab_tile(p_ref, ahead, *, x_ref, lbl_ref, hgn_ref, cw_ref, cb_ref, wo_ref, o_ref, y_ref, st_ref, zb_ref, layer):
    s_idx = pl.program_id(1)
    n_c = SEQ_T // HG_CHUNK

    @pl.when(s_idx == 0)
    def _():
        st_ref[...] = jnp.zeros_like(st_ref)
        zb_ref[0:8, :] = jnp.zeros((8, GROUP_W), F32)

    lg = lbl_ref[...]
    ex = jnp.exp(lg - jnp.max(lg, axis=0, keepdims=True))
    sm = ex / jnp.sum(ex, axis=0, keepdims=True)
    lb = jnp.sum(sm[0:layer + 1, :], axis=0, keepdims=True)

    tril, _ = _chunk_masks(SEQ_T, HG_CHUNK)
    tril01 = jnp.where(tril, 1.0, 0.0).astype(BF16)
    emask = _expand_mask(SEQ_T, HG_CHUNK, D_HEAD)

    f = lb + (1.0 - lb) * jax.nn.sigmoid(p_ref[:, GROUP_W:2 * GROUP_W])
    logf = jnp.log(f)
    b_all = _dot_mask(tril01, logf)
    e_all = _chunk_last(b_all, HG_CHUNK)
    ahead.emit(2)

    for h in range(N_HEADS):
        ahead.emit(1)
        lo, hi = h * D_HEAD, (h + 1) * D_HEAD
        q = p_ref[:, lo:hi]
        v = p_ref[:, 2 * GROUP_W + lo:2 * GROUP_W + hi]
        g = p_ref[:, 3 * GROUP_W + lo:3 * GROUP_W + hi]
        b = b_all[:, lo:hi]
        e = e_all[:, lo:hi]
        kk = 1.0 - f[:, lo:hi]
        q_dec = (q * jax.nn.sigmoid(q) * jnp.exp(b)).astype(BF16)
        k_dec = (kk * jnp.exp(-b)).astype(BF16)
        k_end = (kk * jnp.exp(e - b)).astype(BF16)
        v_t = v.T.astype(BF16)

        attn = jnp.where(tril, _dot_nt(q_dec, k_dec), 0.0)
        ahead.emit(1)
        o_t = _dot_nt(v_t, attn.astype(BF16))

        k_exp = jnp.where(emask, jnp.tile(k_end, (1, n_c)), jnp.zeros((), BF16))
        d_all = _dot(v_t, k_exp)
        ahead.emit(1)

        st = st_ref[h]
        prev = []
        for c in range(n_c):
            prev.append(st)
            decay = jnp.exp(e[c * HG_CHUNK:c * HG_CHUNK + 1, :])
            st = decay * st + d_all[:, c * D_HEAD:(c + 1) * D_HEAD]
        st_ref[h] = st
        s_prev = jnp.concatenate(prev, axis=1).astype(BF16)
        q_exp = jnp.where(emask, jnp.tile(q_dec, (1, n_c)), jnp.zeros((), BF16))
        o = (o_t + _dot_nt(s_prev, q_exp)).T

        o = _rms(o, hgn_ref[:, lo:hi]) * (g * jax.nn.sigmoid(g))
        y_ref[:, lo:hi] = o.astype(BF16)

    z = p_ref[:, 5 * GROUP_W:6 * GROUP_W] * p_ref[:, 6 * GROUP_W:7 * GROUP_W]
    zb_ref[8:SEQ_T + 8, :] = z
    z1 = zb_ref[7:SEQ_T + 7, :]
    z2 = zb_ref[6:SEQ_T + 6, :]
    y = cb_ref[...] + cw_ref[0:1, :] * z2 + cw_ref[1:2, :] * z1 + cw_ref[2:3, :] * z
    y_ref[:, GROUP_W:] = (p_ref[:, 4 * GROUP_W:5 * GROUP_W] * y).astype(BF16)
    zb_ref[0:8, :] = zb_ref[SEQ_T:SEQ_T + 8, :]

    o_ref[...] = x_ref[...] + _dot(y_ref[...], wo_ref[...])


def _mix_ab(x, gain, w_in, lb_logits, hg_norm, conv_w, conv_b, w_out, layer):
    bsz, seq = x.shape[0], x.shape[1]
    n_l = lb_logits.shape[0]
    row = lambda b, s: (0, 0)
    once = pl.Buffered(1)
    return pl.pallas_call(
        functools.partial(_mix_ab_body, layer=layer),
        out_shape=jax.ShapeDtypeStruct((bsz, seq, D_MODEL), F32),
        grid=(bsz, seq // SEQ_T),
        in_specs=[
            pl.BlockSpec((None, SEQ_T, D_MODEL), lambda b, s: (b, s, 0)),
            pl.BlockSpec((None, SEQ_T, D_MODEL), _next_tile(seq // SEQ_T)),
            pl.BlockSpec((1, D_MODEL), row),
            pl.BlockSpec((D_MODEL, AB_COLS), row, pipeline_mode=once),
            pl.BlockSpec((n_l, GROUP_W), row),
            pl.BlockSpec((1, GROUP_W), row),
            pl.BlockSpec((CONV_W, GROUP_W), row),
            pl.BlockSpec((1, GROUP_W), row),
            pl.BlockSpec((D_MODEL, D_MODEL), row, pipeline_mode=once),
        ],
        out_specs=pl.BlockSpec((None, SEQ_T, D_MODEL), lambda b, s: (b, s, 0)),
        scratch_shapes=[pltpu.VMEM((SEQ_T, AB_COLS), F32), pltpu.VMEM((SEQ_T, AB_COLS), F32),
                        pltpu.VMEM((SEQ_T, D_MODEL), BF16),
                        pltpu.VMEM((N_HEADS, D_HEAD, D_HEAD), F32), pltpu.VMEM((SEQ_T + 8, GROUP_W), F32)],
        compiler_params=pltpu.CompilerParams(
            dimension_semantics=("arbitrary", "arbitrary"), vmem_limit_bytes=VMEM_LIMIT),
        name="mix_ab",
    )(x, x, gain.reshape(1, D_MODEL), w_in, lb_logits, hg_norm.reshape(1, GROUP_W), conv_w,
      conv_b.reshape(1, GROUP_W), w_out)


def _mix_cd_body(x_ref, xn_ref, g_ref, w_ml_ref, w_mb_ref, w_gate_ref, bias_ref, mln_ref, cos_ref, sin_ref,
                 o_ref, qo_ref, ko_ref, vt_ref, km_ref, ka_ref, pa_ref, pb_ref, c_ref, n_ref, m_ref):
    tile = functools.partial(_mix_cd_tile, bias_ref=bias_ref, mln_ref=mln_ref, cos_ref=cos_ref, sin_ref=sin_ref,
                             o_ref=o_ref, qo_ref=qo_ref, ko_ref=ko_ref, vt_ref=vt_ref, km_ref=km_ref,
                             ka_ref=ka_ref, c_ref=c_ref, n_ref=n_ref, m_ref=m_ref)
    _with_pipelined_projection(x_ref, xn_ref, g_ref, (w_ml_ref, w_mb_ref, w_gate_ref), pa_ref, pb_ref, tile)


def _mix_cd_tile(p_ref, ahead, *, bias_ref, mln_ref, cos_ref, sin_ref, o_ref, qo_ref, ko_ref, vt_ref, km_ref, ka_ref,
                 c_ref, n_ref, m_ref):
    s_idx = pl.program_id(1)
    n_c = SEQ_T // ML_CHUNK
    gi_ref = p_ref.at[:, 7 * GROUP_W:7 * GROUP_W + LANES]
    gf_ref = p_ref.at[:, 7 * GROUP_W + LANES:7 * GROUP_W + 2 * LANES]

    @pl.when(s_idx == 0)
    def _():
        c_ref[...] = jnp.zeros_like(c_ref)
        n_ref[...] = jnp.zeros_like(n_ref)
        m_ref[...] = jnp.zeros_like(m_ref)

    cos_t, sin_t = cos_ref[...], sin_ref[...]
    for h in range(N_HEADS):
        ahead.emit(1)
        lo, hi = h * D_HEAD, (h + 1) * D_HEAD
        q = _rope(p_ref[:, 4 * GROUP_W + lo:4 * GROUP_W + hi], cos_t, sin_t) * (D_HEAD ** -0.5 * LOG2E)
        k = _rope(p_ref[:, 5 * GROUP_W + lo:5 * GROUP_W + hi], cos_t, sin_t)
        qo_ref[:, lo:hi] = q.astype(BF16)
        ko_ref[h] = k.astype(BF16)
        vt_ref[h] = p_ref[:, 6 * GROUP_W + lo:6 * GROUP_W + hi].T.astype(BF16)
        km_ref[:, lo:hi] = jnp.mean(k, axis=0, keepdims=True)
        ka_ref[:, lo:hi] = jnp.max(jnp.abs(k), axis=0, keepdims=True)

    tril, same = _chunk_masks(SEQ_T, ML_CHUNK)
    tril01 = jnp.where(tril, 1.0, 0.0).astype(BF16)
    same01 = jnp.where(same, 1.0, 0.0).astype(BF16)
    emask = _expand_mask(SEQ_T, ML_CHUNK, D_HEAD)

    log_i = gi_ref[...] + bias_ref[0:1, :]
    log_f = jax.nn.log_sigmoid(gf_ref[...] + bias_ref[1:2, :])
    b_col = _dot_mask(tril01, log_f)
    e_col = _dot_mask(same01, log_f)
    w_end = e_col - b_col + log_i

    m = m_ref[...]
    m_prev_rows, m_new_rows, a_rows = [], [], []
    for c in range(n_c):
        r0 = c * ML_CHUNK
        be = e_col[r0:r0 + 1, :]
        m_end = jnp.max(w_end[r0:r0 + ML_CHUNK, :], axis=0, keepdims=True)
        m_new = jnp.maximum(be + m, m_end)
        a_rows.append(jnp.exp(be + m - m_new))
        m_prev_rows.append(jnp.broadcast_to(m, (ML_CHUNK, LANES)))
        m_new_rows.append(jnp.broadcast_to(m_new, (ML_CHUNK, LANES)))
        m = m_new
    m_ref[...] = m
    log_inter = b_col + jnp.concatenate(m_prev_rows, axis=0)
    wk_scale = jnp.exp(w_end - jnp.concatenate(m_new_rows, axis=0))
    r_rows = (b_col - log_i).T

    for h in range(N_HEADS):
        ahead.emit(3)
        lo, hi = h * D_HEAD, (h + 1) * D_HEAD
        q = p_ref[:, lo:hi] * (D_HEAD ** -0.5)
        k = p_ref[:, GROUP_W + lo:GROUP_W + hi]
        v = p_ref[:, 2 * GROUP_W + lo:2 * GROUP_W + hi]
        og = p_ref[:, 3 * GROUP_W + lo:3 * GROUP_W + hi]
        q16, k16, v16 = q.astype(BF16), k.astype(BF16), v.astype(BF16)

        d_log = jnp.where(tril, b_col[:, h:h + 1] - r_rows[h:h + 1, :], -jnp.inf)
        linter = log_inter[:, h:h + 1]
        m_t = jnp.maximum(linter, jnp.max(d_log, axis=-1, keepdims=True))
        w = jnp.exp(d_log - m_t) * _dot_nt(q16, k16)
        a_in = jnp.exp(linter - m_t)
        num = _dot(w.astype(BF16), v16)
        den = jnp.sum(w, axis=-1, keepdims=True)

        wk = wk_scale[:, h:h + 1] * k
        v_exp = jnp.where(emask, jnp.tile(v16, (1, n_c)), jnp.zeros((), BF16))
        dc_all = _dot(wk.T.astype(BF16), v_exp)

        c_mat = c_ref[h]
        n_vec = n_ref[h]
        c_prev, n_prev = [], []
        for c in range(n_c):
            r0 = c * ML_CHUNK
            c_prev.append(c_mat)
            n_prev.append(jnp.broadcast_to(n_vec, (ML_CHUNK, D_HEAD)))
            a = a_rows[c][:, h:h + 1]
            c_mat = a * c_mat + dc_all[:, c * D_HEAD:(c + 1) * D_HEAD]
            n_vec = a * n_vec + jnp.sum(wk[r0:r0 + ML_CHUNK, :], axis=0, keepdims=True)
        c_ref[h] = c_mat
        n_ref[h] = n_vec

        q_exp = jnp.where(emask, jnp.tile(q16, (1, n_c)), jnp.zeros((), BF16))
        q_c = _dot(q_exp, jnp.concatenate(c_prev, axis=0).astype(BF16))
        q_n = jnp.sum(q * jnp.concatenate(n_prev, axis=0), axis=-1, keepdims=True)
        num = num + a_in * q_c
        den = den + a_in * q_n
        hh = num / jnp.maximum(jnp.abs(den), jnp.exp(-m_t))
        o_ref[:, lo:hi] = (_rms(hh, mln_ref[:, lo:hi]) * jax.nn.sigmoid(og)).astype(BF16)


def _mix_cd(x, gain, w_ml, w_mb, w_gate, bias_rows, ml_norm, cos_t, sin_t):
    bsz, seq = x.shape[0], x.shape[1]
    n_blk, n_tile = seq // MB_BLOCK, seq // MB_TILE
    assert SEQ_T == MB_BLOCK and MB_TILE == 2 * MB_BLOCK
    row = lambda b, s: (0, 0)
    stat = pl.BlockSpec((None, None, 1, GROUP_W), lambda b, s: (b, s, 0, 0))
    return pl.pallas_call(
        _mix_cd_body,
        out_shape=[
            jax.ShapeDtypeStruct((bsz, seq, GROUP_W), BF16),
            jax.ShapeDtypeStruct((bsz, seq, GROUP_W), BF16),
            jax.ShapeDtypeStruct((bsz, N_HEADS, n_tile, MB_TILE, D_HEAD), BF16),
            jax.ShapeDtypeStruct((bsz, N_HEADS, n_tile, D_HEAD, MB_TILE), BF16),
            jax.ShapeDtypeStruct((bsz, n_blk, 1, GROUP_W), F32),
            jax.ShapeDtypeStruct((bsz, n_blk, 1, GROUP_W), F32),
        ],
        grid=(bsz, seq // SEQ_T),
        in_specs=[
            pl.BlockSpec((None, SEQ_T, D_MODEL), lambda b, s: (b, s, 0)),
            pl.BlockSpec((None, SEQ_T, D_MODEL), _next_tile(seq // SEQ_T)),
            pl.BlockSpec((1, D_MODEL), row),
            pl.BlockSpec(w_ml.shape, row, pipeline_mode=pl.Buffered(1)),
            pl.BlockSpec(w_mb.shape, row, pipeline_mode=pl.Buffered(1)),
            pl.BlockSpec(w_gate.shape, row, pipeline_mode=pl.Buffered(1)),
            pl.BlockSpec((2, LANES), row),
            pl.BlockSpec((1, GROUP_W), row),
            pl.BlockSpec((SEQ_T, LANES), lambda b, s: (s, 0)),
            pl.BlockSpec((SEQ_T, LANES), lambda b, s: (s, 0)),
        ],
        out_specs=[
            pl.BlockSpec((None, SEQ_T, GROUP_W), lambda b, s: (b, s, 0)),
            pl.BlockSpec((None, SEQ_T, GROUP_W), lambda b, s: (b, s, 0)),
            pl.BlockSpec((None, N_HEADS, None, MB_BLOCK, D_HEAD), lambda b, s: (b, 0, s // 2, s % 2, 0)),
            pl.BlockSpec((None, N_HEADS, None, D_HEAD, MB_BLOCK), lambda b, s: (b, 0, s // 2, 0, s % 2)),
            stat, stat,
        ],
        scratch_shapes=[pltpu.VMEM((SEQ_T, CD_COLS), F32), pltpu.VMEM((SEQ_T, CD_COLS), F32),
                        pltpu.VMEM((N_HEADS, D_HEAD, D_HEAD), F32),
                        pltpu.VMEM((N_HEADS, 1, D_HEAD), F32),
                        pltpu.VMEM((1, LANES), F32)],
        compiler_params=pltpu.CompilerParams(
            dimension_semantics=("arbitrary", "arbitrary"), vmem_limit_bytes=VMEM_LIMIT),
        name="mix_cd",
    )(x, x, gain.reshape(1, D_MODEL), w_ml, w_mb, w_gate, bias_rows, ml_norm.reshape(1, GROUP_W), cos_t, sin_t)


def _rope(x, cos_t, sin_t):
    lane = lax.broadcasted_iota(jnp.int32, x.shape, 1)
    half = ROPE_DIM // 2
    swapped = jnp.where(lane < half, pltpu.roll(x, LANES - half, axis=1), pltpu.roll(x, half, axis=1))
    return x * cos_t + swapped * sin_t


def _moba_body(q_ref, k_ref, vt_ref, km_ref, ka_ref, x_ref, ya_ref, wo_ref, o_ref,
               sel_ref, sd_ref, acc_ref, yb_ref, *, n_blk):
    t_own = pl.program_id(1)
    heads = [(h, h * D_HEAD, (h + 1) * D_HEAD) for h in range(N_HEADS)]
    bk = MB_BLOCK
    o_ref[...] = x_ref[...] + _dot(ya_ref[...], wo_ref[:GROUP_W, :])

    blk = lax.broadcasted_iota(jnp.int32, (n_blk, MB_TILE), 0)
    lane = lax.broadcasted_iota(jnp.int32, (n_blk, MB_TILE), 1)
    own = 2 * t_own + lane // bk
    bound_past = []
    for h, lo, hi in heads:
        q = q_ref[:, lo:hi]
        km = km_ref[:, lo:hi]
        km_hi = km.astype(BF16)
        km_lo = (km - km_hi.astype(F32)).astype(BF16)
        gate = _dot_nt(km_hi, q) + _dot_nt(km_lo, q)
        gate = jnp.where(blk < own, gate, -jnp.inf)
        sel = jnp.zeros(gate.shape, F32)
        for _ in range(MB_TOPK):
            mx = jnp.max(gate, axis=0, keepdims=True)
            idx = jnp.min(jnp.where(gate == mx, blk, n_blk), axis=0, keepdims=True)
            pick = blk == jnp.where(mx > -jnp.inf, idx, -1)
            sel = jnp.where(pick, 1.0, sel)
            gate = jnp.where(pick, -jnp.inf, gate)
        sel_ref[h] = sel
        bound = _dot_nt(ka_ref[:, lo:hi].astype(BF16), jnp.abs(q)) * (1.0 + 2.0 ** -6)
        bound_past.append(jnp.max(jnp.where(sel > 0.0, bound, -jnp.inf), axis=0, keepdims=True))

    def live_rows(h, t):
        return sel_ref[h, pl.ds(2 * t, 1), :] > 0.0, sel_ref[h, pl.ds(2 * t + 1, 1), :] > 0.0

    kpos = lax.broadcasted_iota(jnp.int32, (MB_TILE, MB_TILE), 0)
    qpos = lax.broadcasted_iota(jnp.int32, (MB_TILE, MB_TILE), 1)
    kb, qb = kpos // bk, qpos // bk
    causal = jnp.where(kb == qb, jnp.where(kpos <= qpos, 1.0, 0.0), 0.0)
    cross = jnp.where(kb < qb, 1.0, 0.0)

    def visible(h):
        return (causal + cross * sel_ref[h, pl.ds(2 * t_own, 1), :]) > 0.0

    m_diag = []
    for h, lo, hi in heads:
        s = _dot_nt(k_ref[h, t_own], q_ref[:, lo:hi])
        sd_ref[h] = s
        m_diag.append(jnp.max(jnp.where(visible(h), s, -jnp.inf), axis=0, keepdims=True))

    def exact_past_max():
        def tile_max(t, ms):
            out = []
            for h, lo, hi in heads:
                s = _dot_nt(k_ref[h, t], q_ref[:, lo:hi])
                live_a, live_b = live_rows(h, t)
                m_a = jnp.where(live_a, jnp.max(s[:bk], axis=0, keepdims=True), -jnp.inf)
                m_b = jnp.where(live_b, jnp.max(s[bk:], axis=0, keepdims=True), -jnp.inf)
                out.append(jnp.maximum(ms[h], jnp.maximum(m_a, m_b)))
            return tuple(out)

        return lax.fori_loop(0, t_own, tile_max, tuple(jnp.full((1, MB_TILE), -jnp.inf, F32) for _ in heads))

    slack = functools.reduce(jnp.maximum, [jnp.max(bound_past[h] - m_diag[h]) for h, _, _ in heads])
    m_past = lax.cond(slack > MB_BOUND_WINDOW, exact_past_max, lambda: tuple(bound_past))

    m_row, l0 = [], []
    for h, lo, hi in heads:
        m_h = jnp.maximum(m_diag[h], m_past[h])
        p = jnp.where(visible(h), jnp.exp2(sd_ref[h] - m_h), 0.0)
        m_row.append(m_h)
        l0.append(jnp.sum(p, axis=0, keepdims=True))
        acc_ref[h] = _dot(vt_ref[h, t_own], p.astype(BF16))

    def accumulate(tiles, ls):
        work = [(t, h, lo, hi) for t in tiles for h, lo, hi in heads]
        score = lambda t, h, lo, hi: _dot_nt(k_ref[h, t], q_ref[:, lo:hi])
        ls = list(ls)
        scores = score(*work[0])
        for i, (t, h, lo, hi) in enumerate(work):
            s = scores
            if i + 1 < len(work):
                scores = score(*work[i + 1])
            e = jnp.exp2(s - m_row[h])
            live_a, live_b = live_rows(h, t)
            p = jnp.concatenate([jnp.where(live_a, e[:bk], 0.0), jnp.where(live_b, e[bk:], 0.0)], axis=0)
            ls[h] = ls[h] + jnp.sum(p, axis=0, keepdims=True)
            acc_ref[h] += _dot(vt_ref[h, t], p.astype(BF16))
        return tuple(ls)

    l_f = lax.fori_loop(0, t_own // 2, lambda i, ls: accumulate([2 * i, 2 * i + 1], ls), tuple(l0))
    l_f = lax.cond(t_own % 2 == 1, lambda ls: accumulate([t_own - 1], ls), lambda ls: ls, l_f)
    for h, lo, hi in heads:
        yb_ref[:, lo:hi] = (acc_ref[h] / l_f[h]).T.astype(BF16)

    o_ref[...] += _dot(yb_ref[...], wo_ref[GROUP_W:, :])


def _moba(q, k_tiles, vt_tiles, k_mean, k_absmax, x, ya, w_out):
    bsz, seq = q.shape[0], q.shape[1]
    n_blk, n_tile = seq // MB_BLOCK, seq // MB_TILE
    once = pl.Buffered(1)
    tile = lambda b, i: (b, i, 0)
    return pl.pallas_call(
        functools.partial(_moba_body, n_blk=n_blk),
        out_shape=jax.ShapeDtypeStruct((bsz, seq, D_MODEL), F32),
        grid=(bsz, n_tile),
        in_specs=[
            pl.BlockSpec((None, MB_TILE, GROUP_W), tile),
            pl.BlockSpec((None, N_HEADS, n_tile, MB_TILE, D_HEAD), lambda b, i: (b, 0, 0, 0, 0),
                         pipeline_mode=once),
            pl.BlockSpec((None, N_HEADS, n_tile, D_HEAD, MB_TILE), lambda b, i: (b, 0, 0, 0, 0),
                         pipeline_mode=once),
            pl.BlockSpec((None, n_blk, GROUP_W), lambda b, i: (b, 0, 0)),
            pl.BlockSpec((None, n_blk, GROUP_W), lambda b, i: (b, 0, 0)),
            pl.BlockSpec((None, MB_TILE, D_MODEL), tile),
            pl.BlockSpec((None, MB_TILE, GROUP_W), tile),
            pl.BlockSpec((D_MODEL, D_MODEL), lambda b, i: (0, 0), pipeline_mode=once),
        ],
        out_specs=pl.BlockSpec((None, MB_TILE, D_MODEL), tile),
        scratch_shapes=[pltpu.VMEM((N_HEADS, n_blk, MB_TILE), F32),
                        pltpu.VMEM((N_HEADS, MB_TILE, MB_TILE), F32),
                        pltpu.VMEM((N_HEADS, D_HEAD, MB_TILE), F32),
                        pltpu.VMEM((MB_TILE, GROUP_W), BF16)],
        compiler_params=pltpu.CompilerParams(
            dimension_semantics=("parallel", "arbitrary"), vmem_limit_bytes=VMEM_LIMIT),
        name="moba",
    )(q, k_tiles, vt_tiles, k_mean, k_absmax, x, ya, w_out)


def _rope_tables(seq):
    half = ROPE_DIM // 2
    inv_freq = np.float64(ROPE_THETA) ** (-np.arange(half, dtype=np.float64) * 2.0 / ROPE_DIM)
    ang = np.arange(seq, dtype=np.float64)[:, None] * inv_freq[None, :]
    cos, sin = np.cos(ang).astype(np.float32), np.sin(ang).astype(np.float32)
    rest = LANES - ROPE_DIM
    cos_t = np.concatenate([cos, cos, np.ones((seq, rest), np.float32)], axis=1)
    sin_t = np.concatenate([-sin, sin, np.zeros((seq, rest), np.float32)], axis=1)
    return jnp.asarray(cos_t), jnp.asarray(sin_t)


def kernel(x, ffn_norm, ffn_w_in, ffn_w_out, mix_norm, ab_w_in, ab_w_out, hgrn_lb_logits, hgrn_out_norm,
           conv_w, conv_b, cd_w_in, cd_w_out, mlstm_gate_bias, mlstm_out_norm, final_norm):
    bsz, seq, d = x.shape
    depth = ffn_norm.shape[0]
    t = bsz * seq
    xt = x.reshape(t, d)
    cos_t, sin_t = _rope_tables(seq)

    for layer in range(depth):
        xt = _ffn(xt, ffn_norm[layer, 0], ffn_w_in, ffn_w_out, layer, 0, final_norm, False)
        x3 = xt.reshape(bsz, seq, d)
        if layer % 2 == 0:
            e = layer // 2
            x3 = _mix_ab(x3, mix_norm[layer], ab_w_in[e].astype(BF16), hgrn_lb_logits, hgrn_out_norm[e],
                         conv_w[e], conv_b[e], ab_w_out[e].astype(BF16), layer)
        else:
            o = layer // 2
            w = cd_w_in[o]
            n_ml = 4 * GROUP_W
            pad = jnp.zeros((d, LANES - N_HEADS), BF16)
            w_ml = w[:, :n_ml].astype(BF16)
            w_mb = w[:, n_ml + 2 * N_HEADS:].astype(BF16)
            w_gate = jnp.concatenate([w[:, n_ml:n_ml + N_HEADS].astype(BF16), pad,
                                      w[:, n_ml + N_HEADS:n_ml + 2 * N_HEADS].astype(BF16), pad], axis=1)
            bias_rows = jnp.pad(mlstm_gate_bias[o], ((0, 0), (0, LANES - N_HEADS)))
            ya, q_r, k_tiles, vt_tiles, k_mean, k_absmax = _mix_cd(
                x3, mix_norm[layer], w_ml, w_mb, w_gate, bias_rows, mlstm_out_norm[o], cos_t, sin_t)
            n_blk = seq // MB_BLOCK
            x3 = _moba(q_r, k_tiles, vt_tiles,
                       k_mean.reshape(bsz, n_blk, GROUP_W), k_absmax.reshape(bsz, n_blk, GROUP_W),
                       x3, ya, cd_w_out[o].astype(BF16))
        xt = x3.reshape(t, d)
        xt = _ffn(xt, ffn_norm[layer, 1], ffn_w_in, ffn_w_out, layer, 1, final_norm, layer == depth - 1)
    return xt.reshape(bsz, seq, d)
```

```python
import functools

import jax
import jax.numpy as jnp
import numpy as np
from jax import lax
from jax.experimental import pallas as pl
from jax.experimental.pallas import tpu as pltpu

F32 = jnp.float32
BF16 = jnp.bfloat16

D_MODEL = 1024
D_FF = 2816
GROUP_W = 512
N_HEADS = 4
D_HEAD = 128
RMS_EPS = 1e-6
HG_CHUNK = 32
ML_CHUNK = 64
MB_BLOCK = 256
MB_TOPK = 3
CONV_W = 3
ROPE_THETA = 500000.0
ROPE_DIM = D_HEAD // 4

LANES = 128
VMEM_LIMIT = 48 * 1024 * 1024

FFN_TM = 1024
FFN_SUB = 512
FFN_STAGE_IN = 64
FFN_STAGE_OUT = 256
FFN_CH = 256
PROJ_CH = 256
SEQ_T = 256
MB_TILE = 2 * MB_BLOCK
MB_BOUND_WINDOW = 80.0
LOG2E = 1.4426950408889634

AB_COLS = 7 * GROUP_W
CD_COLS = 7 * GROUP_W + 2 * LANES


def _rms(x, gain):
    return x * lax.rsqrt(jnp.mean(x * x, axis=-1, keepdims=True) + RMS_EPS) * gain


def _dot(a, b):
    return jnp.dot(a, b, preferred_element_type=F32)


def _dot_nt(a, b):
    return lax.dot_general(a, b, (((1,), (1,)), ((), ())), preferred_element_type=F32)


def _dot_mask(mask01, x):
    hi = x.astype(BF16)
    lo = (x - hi.astype(F32)).astype(BF16)
    return _dot(mask01, hi) + _dot(mask01, lo)


def _chunk_last(x, chunk):
    rows = x.shape[0]
    return jnp.concatenate([jnp.broadcast_to(x[r0 + chunk - 1:r0 + chunk, :], (chunk, x.shape[1]))
                            for r0 in range(0, rows, chunk)], axis=0)


def _chunk_masks(n, chunk):
    r = lax.broadcasted_iota(jnp.int32, (n, n), 0)
    c = lax.broadcasted_iota(jnp.int32, (n, n), 1)
    same = (r // chunk) == (c // chunk)
    return jnp.where(same, jnp.where(c <= r, 1, 0), 0) > 0, same


def _expand_mask(n, chunk, width):
    n_c = n // chunk
    r = lax.broadcasted_iota(jnp.int32, (n, n_c * width), 0)
    c = lax.broadcasted_iota(jnp.int32, (n, n_c * width), 1)
    return (r // chunk) == (c // width)


def _stream_rows(src_hbm, stage_ref, sem, consume):
    rows = stage_ref.shape[1]
    n = src_hbm.shape[0] // rows

    def copy(k):
        return pltpu.make_async_copy(src_hbm.at[pl.ds(k * rows, rows), :], stage_ref.at[k % 2], sem.at[k % 2])

    copy(0).start()
    for k in range(n):
        if k + 1 < n:
            copy(k + 1).start()
        copy(k).wait()
        consume(k * rows, stage_ref[k % 2])


def _ffn_body(x_ref, g_ref, w_in_hbm, w_out_hbm, fg_ref, o_ref, wg_ref, wu_ref, wo_ref, act_ref,
              stage_in, stage_out, sem_in, sem_out, *, layer, which, final_norm):
    @pl.when(pl.program_id(0) == 0)
    def _():
        def keep_in(r0, chunk):
            wg_ref[pl.ds(r0, chunk.shape[0]), :] = chunk[:, :D_FF].astype(BF16)
            wu_ref[pl.ds(r0, chunk.shape[0]), :] = chunk[:, D_FF:].astype(BF16)

        def keep_out(r0, chunk):
            wo_ref[pl.ds(r0, chunk.shape[0]), :] = chunk.astype(BF16)

        _stream_rows(w_in_hbm.at[layer, which], stage_in, sem_in, keep_in)
        _stream_rows(w_out_hbm.at[layer, which], stage_out, sem_out, keep_out)

    for r0 in range(0, FFN_TM, FFN_SUB):
        rows = pl.ds(r0, FFN_SUB)
        h = _rms(x_ref[rows, :], g_ref[...]).astype(BF16)
        for c0 in range(0, D_FF, FFN_CH):
            gate = _dot(h, wg_ref[:, c0:c0 + FFN_CH])
            up = _dot(h, wu_ref[:, c0:c0 + FFN_CH])
            act_ref[rows, c0:c0 + FFN_CH] = (gate * jax.nn.sigmoid(gate) * up).astype(BF16)
        y = x_ref[rows, :] + 0.5 * _dot(act_ref[rows, :], wo_ref[...])
        if final_norm:
            y = _rms(y, fg_ref[...])
        o_ref[rows, :] = y


def _ffn(x, gain, w_in, w_out, layer, which, final_gain, final_norm):
    t = x.shape[0]
    return pl.pallas_call(
        functools.partial(_ffn_body, layer=layer, which=which, final_norm=final_norm),
        out_shape=jax.ShapeDtypeStruct((t, D_MODEL), F32),
        grid=(t // FFN_TM,),
        in_specs=[
            pl.BlockSpec((FFN_TM, D_MODEL), lambda i: (i, 0)),
            pl.BlockSpec((1, D_MODEL), lambda i: (0, 0)),
            pl.BlockSpec(memory_space=pl.ANY),
            pl.BlockSpec(memory_space=pl.ANY),
            pl.BlockSpec((1, D_MODEL), lambda i: (0, 0)),
        ],
        out_specs=pl.BlockSpec((FFN_TM, D_MODEL), lambda i: (i, 0)),
        scratch_shapes=[pltpu.VMEM((D_MODEL, D_FF), BF16), pltpu.VMEM((D_MODEL, D_FF), BF16),
                        pltpu.VMEM((D_FF, D_MODEL), BF16), pltpu.VMEM((FFN_TM, D_FF), BF16),
                        pltpu.VMEM((2, FFN_STAGE_IN, 2 * D_FF), F32), pltpu.VMEM((2, FFN_STAGE_OUT, D_MODEL), F32),
                        pltpu.SemaphoreType.DMA((2,)), pltpu.SemaphoreType.DMA((2,))],
        compiler_params=pltpu.CompilerParams(
            dimension_semantics=("arbitrary",), vmem_limit_bytes=VMEM_LIMIT),
        name="ffn",
    )(x, gain.reshape(1, D_MODEL), w_in, w_out, final_gain.reshape(1, D_MODEL))


class _ChunkedProjection:
    def __init__(self, x_ref, g_ref, w_refs, p_ref):
        self.h = _rms(x_ref[...], g_ref[...]).astype(BF16)
        self.p_ref = p_ref
        self.todo = [(w_ref, c0) for w_ref in w_refs for c0 in range(0, w_ref.shape[1], PROJ_CH)]
        self.done = 0

    def emit(self, n_chunks):
        end = len(self.todo) if n_chunks is None else min(len(self.todo), self.done + n_chunks)
        for i in range(self.done, end):
            w_ref, c0 = self.todo[i]
            self.p_ref[:, i * PROJ_CH:(i + 1) * PROJ_CH] = _dot(self.h, w_ref[:, c0:c0 + PROJ_CH])
        self.done = end


def _project_in(x_ref, g_ref, w_refs, p_ref):
    _ChunkedProjection(x_ref, g_ref, w_refs, p_ref).emit(None)


def _with_pipelined_projection(x_ref, xn_ref, g_ref, w_refs, pa_ref, pb_ref, tile_fn):
    s_idx = pl.program_id(1)

    @pl.when(s_idx == 0)
    def _():
        _project_in(x_ref, g_ref, w_refs, pa_ref)

    def branch(parity, p_cur, p_nxt):
        @pl.when(s_idx % 2 == parity)
        def _():
            ahead = _ChunkedProjection(xn_ref, g_ref, w_refs, p_nxt)
            tile_fn(p_cur, ahead)
            ahead.emit(None)

    branch(0, pa_ref, pb_ref)
    branch(1, pb_ref, pa_ref)


def _next_tile(n_s):
    return lambda b, s: (b, jnp.minimum(s + 1, n_s - 1), 0)


def _mix_ab_body(x_ref, xn_ref, g_ref, w_ref, lbl_ref, hgn_ref, cw_ref, cb_ref, wo_ref, o_ref,
                 pa_ref, pb_ref, y_ref, st_ref, zb_ref, *, layer):
    tile = functools.partial(_mix_ab_tile, x_ref=x_ref, lbl_ref=lbl_ref, hgn_ref=hgn_ref, cw_ref=cw_ref,
                             cb_ref=cb_ref, wo_ref=wo_ref, o_ref=o_ref, y_ref=y_ref, st_ref=st_ref,
                             zb_ref=zb_ref, layer=layer)
    _with_pipelined_projection(x_ref, xn_ref, g_ref, (w_ref,), pa_ref, pb_ref, tile)


def _mix_ab_tile(p_ref, ahead, *, x_ref, lbl_ref, hgn_ref, cw_ref, cb_ref, wo_ref, o_ref, y_ref, st_ref, zb_ref, layer):
    s_idx = pl.program_id(1)
    n_c = SEQ_T // HG_CHUNK

    @pl.when(s_idx == 0)
    def _():
        st_ref[...] = jnp.zeros_like(st_ref)
        zb_ref[0:8, :] = jnp.zeros((8, GROUP_W), F32)

    lg = lbl_ref[...]
    ex = jnp.exp(lg - jnp.max(lg, axis=0, keepdims=True))
    sm = ex / jnp.sum(ex, axis=0, keepdims=True)
    lb = jnp.sum(sm[0:layer + 1, :], axis=0, keepdims=True)

    tril, _ = _chunk_masks(SEQ_T, HG_CHUNK)
    tril01 = jnp.where(tril, 1.0, 0.0).astype(BF16)
    emask = _expand_mask(SEQ_T, HG_CHUNK, D_HEAD)

    f = lb + (1.0 - lb) * jax.nn.sigmoid(p_ref[:, GROUP_W:2 * GROUP_W])
    logf = jnp.log(f)
    b_all = _dot_mask(tril01, logf)
    e_all = _chunk_last(b_all, HG_CHUNK)
    ahead.emit(2)

    for h in range(N_HEADS):
        ahead.emit(1)
        lo, hi = h * D_HEAD, (h + 1) * D_HEAD
        q = p_ref[:, lo:hi]
        v = p_ref[:, 2 * GROUP_W + lo:2 * GROUP_W + hi]
        g = p_ref[:, 3 * GROUP_W + lo:3 * GROUP_W + hi]
        b = b_all[:, lo:hi]
        e = e_all[:, lo:hi]
        kk = 1.0 - f[:, lo:hi]
        q_dec = (q * jax.nn.sigmoid(q) * jnp.exp(b)).astype(BF16)
        k_dec = (kk * jnp.exp(-b)).astype(BF16)
        k_end = (kk * jnp.exp(e - b)).astype(BF16)
        v_t = v.T.astype(BF16)

        attn = jnp.where(tril, _dot_nt(q_dec, k_dec), 0.0)
        ahead.emit(1)
        o_t = _dot_nt(v_t, attn.astype(BF16))

        k_exp = jnp.where(emask, jnp.tile(k_end, (1, n_c)), jnp.zeros((), BF16))
        d_all = _dot(v_t, k_exp)
        ahead.emit(1)

        st = st_ref[h]
        prev = []
        for c in range(n_c):
            prev.append(st)
            decay = jnp.exp(e[c * HG_CHUNK:c * HG_CHUNK + 1, :])
            st = decay * st + d_all[:, c * D_HEAD:(c + 1) * D_HEAD]
        st_ref[h] = st
        s_prev = jnp.concatenate(prev, axis=1).astype(BF16)
        q_exp = jnp.where(emask, jnp.tile(q_dec, (1, n_c)), jnp.zeros((), BF16))
        o = (o_t + _dot_nt(s_prev, q_exp)).T

        o = _rms(o, hgn_ref[:, lo:hi]) * (g * jax.nn.sigmoid(g))
        y_ref[:, lo:hi] = o.astype(BF16)

    z = p_ref[:, 5 * GROUP_W:6 * GROUP_W] * p_ref[:, 6 * GROUP_W:7 * GROUP_W]
    zb_ref[8:SEQ_T + 8, :] = z
    z1 = zb_ref[7:SEQ_T + 7, :]
    z2 = zb_ref[6:SEQ_T + 6, :]
    y = cb_ref[...] + cw_ref[0:1, :] * z2 + cw_ref[1:2, :] * z1 + cw_ref[2:3, :] * z
    y_ref[:, GROUP_W:] = (p_ref[:, 4 * GROUP_W:5 * GROUP_W] * y).astype(BF16)
    zb_ref[0:8, :] = zb_ref[SEQ_T:SEQ_T + 8, :]

    o_ref[...] = x_ref[...] + _dot(y_ref[...], wo_ref[...])


def _mix_ab(x, gain, w_in, lb_logits, hg_norm, conv_w, conv_b, w_out, layer):
    bsz, seq = x.shape[0], x.shape[1]
    n_l = lb_logits.shape[0]
    row = lambda b, s: (0, 0)
    once = pl.Buffered(1)
    return pl.pallas_call(
        functools.partial(_mix_ab_body, layer=layer),
        out_shape=jax.ShapeDtypeStruct((bsz, seq, D_MODEL), F32),
        grid=(bsz, seq // SEQ_T),
        in_specs=[
            pl.BlockSpec((None, SEQ_T, D_MODEL), lambda b, s: (b, s, 0)),
            pl.BlockSpec((None, SEQ_T, D_MODEL), _next_tile(seq // SEQ_T)),
            pl.BlockSpec((1, D_MODEL), row),
            pl.BlockSpec((D_MODEL, AB_COLS), row, pipeline_mode=once),
            pl.BlockSpec((n_l, GROUP_W), row),
            pl.BlockSpec((1, GROUP_W), row),
            pl.BlockSpec((CONV_W, GROUP_W), row),
            pl.BlockSpec((1, GROUP_W), row),
            pl.BlockSpec((D_MODEL, D_MODEL), row, pipeline_mode=once),
        ],
        out_specs=pl.BlockSpec((None, SEQ_T, D_MODEL), lambda b, s: (b, s, 0)),
        scratch_shapes=[pltpu.VMEM((SEQ_T, AB_COLS), F32), pltpu.VMEM((SEQ_T, AB_COLS), F32),
                        pltpu.VMEM((SEQ_T, D_MODEL), BF16),
                        pltpu.VMEM((N_HEADS, D_HEAD, D_HEAD), F32), pltpu.VMEM((SEQ_T + 8, GROUP_W), F32)],
        compiler_params=pltpu.CompilerParams(
            dimension_semantics=("arbitrary", "arbitrary"), vmem_limit_bytes=VMEM_LIMIT),
        name="mix_ab",
    )(x, x, gain.reshape(1, D_MODEL), w_in, lb_logits, hg_norm.reshape(1, GROUP_W), conv_w,
      conv_b.reshape(1, GROUP_W), w_out)


def _mix_cd_body(x_ref, xn_ref, g_ref, w_ml_ref, w_mb_ref, w_gate_ref, bias_ref, mln_ref, cos_ref, sin_ref,
                 o_ref, qo_ref, ko_ref, vt_ref, km_ref, ka_ref, pa_ref, pb_ref, c_ref, n_ref, m_ref):
    tile = functools.partial(_mix_cd_tile, bias_ref=bias_ref, mln_ref=mln_ref, cos_ref=cos_ref, sin_ref=sin_ref,
                             o_ref=o_ref, qo_ref=qo_ref, ko_ref=ko_ref, vt_ref=vt_ref, km_ref=km_ref,
                             ka_ref=ka_ref, c_ref=c_ref, n_ref=n_ref, m_ref=m_ref)
    _with_pipelined_projection(x_ref, xn_ref, g_ref, (w_ml_ref, w_mb_ref, w_gate_ref), pa_ref, pb_ref, tile)


def _mix_cd_tile(p_ref, ahead, *, bias_ref, mln_ref, cos_ref, sin_ref, o_ref, qo_ref, ko_ref, vt_ref, km_ref, ka_ref,
                 c_ref, n_ref, m_ref):
    s_idx = pl.program_id(1)
    n_c = SEQ_T // ML_CHUNK
    gi_ref = p_ref.at[:, 7 * GROUP_W:7 * GROUP_W + LANES]
    gf_ref = p_ref.at[:, 7 * GROUP_W + LANES:7 * GROUP_W + 2 * LANES]

    @pl.when(s_idx == 0)
    def _():
        c_ref[...] = jnp.zeros_like(c_ref)
        n_ref[...] = jnp.zeros_like(n_ref)
        m_ref[...] = jnp.zeros_like(m_ref)

    cos_t, sin_t = cos_ref[...], sin_ref[...]
    for h in range(N_HEADS):
        ahead.emit(1)
        lo, hi = h * D_HEAD, (h + 1) * D_HEAD
        q = _rope(p_ref[:, 4 * GROUP_W + lo:4 * GROUP_W + hi], cos_t, sin_t) * (D_HEAD ** -0.5 * LOG2E)
        k = _rope(p_ref[:, 5 * GROUP_W + lo:5 * GROUP_W + hi], cos_t, sin_t)
        qo_ref[:, lo:hi] = q.astype(BF16)
        ko_ref[h] = k.astype(BF16)
        vt_ref[h] = p_ref[:, 6 * GROUP_W + lo:6 * GROUP_W + hi].T.astype(BF16)
        km_ref[:, lo:hi] = jnp.mean(k, axis=0, keepdims=True)
        ka_ref[:, lo:hi] = jnp.max(jnp.abs(k), axis=0, keepdims=True)

    tril, same = _chunk_masks(SEQ_T, ML_CHUNK)
    tril01 = jnp.where(tril, 1.0, 0.0).astype(BF16)
    same01 = jnp.where(same, 1.0, 0.0).astype(BF16)
    emask = _expand_mask(SEQ_T, ML_CHUNK, D_HEAD)

    log_i = gi_ref[...] + bias_ref[0:1, :]
    log_f = jax.nn.log_sigmoid(gf_ref[...] + bias_ref[1:2, :])
    b_col = _dot_mask(tril01, log_f)
    e_col = _dot_mask(same01, log_f)
    w_end = e_col - b_col + log_i

    m = m_ref[...]
    m_prev_rows, m_new_rows, a_rows = [], [], []
    for c in range(n_c):
        r0 = c * ML_CHUNK
        be = e_col[r0:r0 + 1, :]
        m_end = jnp.max(w_end[r0:r0 + ML_CHUNK, :], axis=0, keepdims=True)
        m_new = jnp.maximum(be + m, m_end)
        a_rows.append(jnp.exp(be + m - m_new))
        m_prev_rows.append(jnp.broadcast_to(m, (ML_CHUNK, LANES)))
        m_new_rows.append(jnp.broadcast_to(m_new, (ML_CHUNK, LANES)))
        m = m_new
    m_ref[...] = m
    log_inter = b_col + jnp.concatenate(m_prev_rows, axis=0)
    wk_scale = jnp.exp(w_end - jnp.concatenate(m_new_rows, axis=0))
    r_rows = (b_col - log_i).T

    for h in range(N_HEADS):
        ahead.emit(3)
        lo, hi = h * D_HEAD, (h + 1) * D_HEAD
        q = p_ref[:, lo:hi] * (D_HEAD ** -0.5)
        k = p_ref[:, GROUP_W + lo:GROUP_W + hi]
        v = p_ref[:, 2 * GROUP_W + lo:2 * GROUP_W + hi]
        og = p_ref[:, 3 * GROUP_W + lo:3 * GROUP_W + hi]
        q16, k16, v16 = q.astype(BF16), k.astype(BF16), v.astype(BF16)

        d_log = jnp.where(tril, b_col[:, h:h + 1] - r_rows[h:h + 1, :], -jnp.inf)
        linter = log_inter[:, h:h + 1]
        m_t = jnp.maximum(linter, jnp.max(d_log, axis=-1, keepdims=True))
        w = jnp.exp(d_log - m_t) * _dot_nt(q16, k16)
        a_in = jnp.exp(linter - m_t)
        num = _dot(w.astype(BF16), v16)
        den = jnp.sum(w, axis=-1, keepdims=True)

        wk = wk_scale[:, h:h + 1] * k
        v_exp = jnp.where(emask, jnp.tile(v16, (1, n_c)), jnp.zeros((), BF16))
        dc_all = _dot(wk.T.astype(BF16), v_exp)

        c_mat = c_ref[h]
        n_vec = n_ref[h]
        c_prev, n_prev = [], []
        for c in range(n_c):
            r0 = c * ML_CHUNK
            c_prev.append(c_mat)
            n_prev.append(jnp.broadcast_to(n_vec, (ML_CHUNK, D_HEAD)))
            a = a_rows[c][:, h:h + 1]
            c_mat = a * c_mat + dc_all[:, c * D_HEAD:(c + 1) * D_HEAD]
            n_vec = a * n_vec + jnp.sum(wk[r0:r0 + ML_CHUNK, :], axis=0, keepdims=True)
        c_ref[h] = c_mat
        n_ref[h] = n_vec

        q_exp = jnp.where(emask, jnp.tile(q16, (1, n_c)), jnp.zeros((), BF16))
        q_c = _dot(q_exp, jnp.concatenate(c_prev, axis=0).astype(BF16))
        q_n = jnp.sum(q * jnp.concatenate(n_prev, axis=0), axis=-1, keepdims=True)
        num = num + a_in * q_c
        den = den + a_in * q_n
        hh = num / jnp.maximum(jnp.abs(den), jnp.exp(-m_t))
        o_ref[:, lo:hi] = (_rms(hh, mln_ref[:, lo:hi]) * jax.nn.sigmoid(og)).astype(BF16)


def _mix_cd(x, gain, w_ml, w_mb, w_gate, bias_rows, ml_norm, cos_t, sin_t):
    bsz, seq = x.shape[0], x.shape[1]
    n_blk, n_tile = seq // MB_BLOCK, seq // MB_TILE
    assert SEQ_T == MB_BLOCK and MB_TILE == 2 * MB_BLOCK
    row = lambda b, s: (0, 0)
    stat = pl.BlockSpec((None, None, 1, GROUP_W), lambda b, s: (b, s, 0, 0))
    return pl.pallas_call(
        _mix_cd_body,
        out_shape=[
            jax.ShapeDtypeStruct((bsz, seq, GROUP_W), BF16),
            jax.ShapeDtypeStruct((bsz, seq, GROUP_W), BF16),
            jax.ShapeDtypeStruct((bsz, N_HEADS, n_tile, MB_TILE, D_HEAD), BF16),
            jax.ShapeDtypeStruct((bsz, N_HEADS, n_tile, D_HEAD, MB_TILE), BF16),
            jax.ShapeDtypeStruct((bsz, n_blk, 1, GROUP_W), F32),
            jax.ShapeDtypeStruct((bsz, n_blk, 1, GROUP_W), F32),
        ],
        grid=(bsz, seq // SEQ_T),
        in_specs=[
            pl.BlockSpec((None, SEQ_T, D_MODEL), lambda b, s: (b, s, 0)),
            pl.BlockSpec((None, SEQ_T, D_MODEL), _next_tile(seq // SEQ_T)),
            pl.BlockSpec((1, D_MODEL), row),
            pl.BlockSpec(w_ml.shape, row, pipeline_mode=pl.Buffered(1)),
            pl.BlockSpec(w_mb.shape, row, pipeline_mode=pl.Buffered(1)),
            pl.BlockSpec(w_gate.shape, row, pipeline_mode=pl.Buffered(1)),
            pl.BlockSpec((2, LANES), row),
            pl.BlockSpec((1, GROUP_W), row),
            pl.BlockSpec((SEQ_T, LANES), lambda b, s: (s, 0)),
            pl.BlockSpec((SEQ_T, LANES), lambda b, s: (s, 0)),
        ],
        out_specs=[
            pl.BlockSpec((None, SEQ_T, GROUP_W), lambda b, s: (b, s, 0)),
            pl.BlockSpec((None, SEQ_T, GROUP_W), lambda b, s: (b, s, 0)),
            pl.BlockSpec((None, N_HEADS, None, MB_BLOCK, D_HEAD), lambda b, s: (b, 0, s // 2, s % 2, 0)),
            pl.BlockSpec((None, N_HEADS, None, D_HEAD, MB_BLOCK), lambda b, s: (b, 0, s // 2, 0, s % 2)),
            stat, stat,
        ],
        scratch_shapes=[pltpu.VMEM((SEQ_T, CD_COLS), F32), pltpu.VMEM((SEQ_T, CD_COLS), F32),
                        pltpu.VMEM((N_HEADS, D_HEAD, D_HEAD), F32),
                        pltpu.VMEM((N_HEADS, 1, D_HEAD), F32),
                        pltpu.VMEM((1, LANES), F32)],
        compiler_params=pltpu.CompilerParams(
            dimension_semantics=("arbitrary", "arbitrary"), vmem_limit_bytes=VMEM_LIMIT),
        name="mix_cd",
    )(x, x, gain.reshape(1, D_MODEL), w_ml, w_mb, w_gate, bias_rows, ml_norm.reshape(1, GROUP_W), cos_t, sin_t)


def _rope(x, cos_t, sin_t):
    lane = lax.broadcasted_iota(jnp.int32, x.shape, 1)
    half = ROPE_DIM // 2
    swapped = jnp.where(lane < half, pltpu.roll(x, LANES - half, axis=1), pltpu.roll(x, half, axis=1))
    return x * cos_t + swapped * sin_t


def _moba_body(q_ref, k_ref, vt_ref, km_ref, ka_ref, x_ref, ya_ref, wo_ref, o_ref,
               sel_ref, sd_ref, acc_ref, yb_ref, *, n_blk):
    t_own = pl.program_id(1)
    heads = [(h, h * D_HEAD, (h + 1) * D_HEAD) for h in range(N_HEADS)]
    bk = MB_BLOCK
    o_ref[...] = x_ref[...] + _dot(ya_ref[...], wo_ref[:GROUP_W, :])

    blk = lax.broadcasted_iota(jnp.int32, (n_blk, MB_TILE), 0)
    lane = lax.broadcasted_iota(jnp.int32, (n_blk, MB_TILE), 1)
    own = 2 * t_own + lane // bk
    bound_past = []
    for h, lo, hi in heads:
        q = q_ref[:, lo:hi]
        km = km_ref[:, lo:hi]
        km_hi = km.astype(BF16)
        km_lo = (km - km_hi.astype(F32)).astype(BF16)
        gate = _dot_nt(km_hi, q) + _dot_nt(km_lo, q)
        gate = jnp.where(blk < own, gate, -jnp.inf)
        sel = jnp.zeros(gate.shape, F32)
        for _ in range(MB_TOPK):
            mx = jnp.max(gate, axis=0, keepdims=True)
            idx = jnp.min(jnp.where(gate == mx, blk, n_blk), axis=0, keepdims=True)
            pick = blk == jnp.where(mx > -jnp.inf, idx, -1)
            sel = jnp.where(pick, 1.0, sel)
            gate = jnp.where(pick, -jnp.inf, gate)
        sel_ref[h] = sel
        bound = _dot_nt(ka_ref[:, lo:hi].astype(BF16), jnp.abs(q)) * (1.0 + 2.0 ** -6)
        bound_past.append(jnp.max(jnp.where(sel > 0.0, bound, -jnp.inf), axis=0, keepdims=True))

    def live_rows(h, t):
        return sel_ref[h, pl.ds(2 * t, 1), :] > 0.0, sel_ref[h, pl.ds(2 * t + 1, 1), :] > 0.0

    kpos = lax.broadcasted_iota(jnp.int32, (MB_TILE, MB_TILE), 0)
    qpos = lax.broadcasted_iota(jnp.int32, (MB_TILE, MB_TILE), 1)
    kb, qb = kpos // bk, qpos // bk
    causal = jnp.where(kb == qb, jnp.where(kpos <= qpos, 1.0, 0.0), 0.0)
    cross = jnp.where(kb < qb, 1.0, 0.0)

    def visible(h):
        return (causal + cross * sel_ref[h, pl.ds(2 * t_own, 1), :]) > 0.0

    m_diag = []
    for h, lo, hi in heads:
        s = _dot_nt(k_ref[h, t_own], q_ref[:, lo:hi])
        sd_ref[h] = s
        m_diag.append(jnp.max(jnp.where(visible(h), s, -jnp.inf), axis=0, keepdims=True))

    def exact_past_max():
        def tile_max(t, ms):
            out = []
            for h, lo, hi in heads:
                s = _dot_nt(k_ref[h, t], q_ref[:, lo:hi])
                live_a, live_b = live_rows(h, t)
                m_a = jnp.where(live_a, jnp.max(s[:bk], axis=0, keepdims=True), -jnp.inf)
                m_b = jnp.where(live_b, jnp.max(s[bk:], axis=0, keepdims=True), -jnp.inf)
                out.append(jnp.maximum(ms[h], jnp.maximum(m_a, m_b)))
            return tuple(out)

        return lax.fori_loop(0, t_own, tile_max, tuple(jnp.full((1, MB_TILE), -jnp.inf, F32) for _ in heads))

    slack = functools.reduce(jnp.maximum, [jnp.max(bound_past[h] - m_diag[h]) for h, _, _ in heads])
    m_past = lax.cond(slack > MB_BOUND_WINDOW, exact_past_max, lambda: tuple(bound_past))

    m_row, l0 = [], []
    for h, lo, hi in heads:
        m_h = jnp.maximum(m_diag[h], m_past[h])
        p = jnp.where(visible(h), jnp.exp2(sd_ref[h] - m_h), 0.0)
        m_row.append(m_h)
        l0.append(jnp.sum(p, axis=0, keepdims=True))
        acc_ref[h] = _dot(vt_ref[h, t_own], p.astype(BF16))

    def accumulate(tiles, ls):
        work = [(t, h, lo, hi) for t in tiles for h, lo, hi in heads]
        score = lambda t, h, lo, hi: _dot_nt(k_ref[h, t], q_ref[:, lo:hi])
        ls = list(ls)
        scores = score(*work[0])
        for i, (t, h, lo, hi) in enumerate(work):
            s = scores
            if i + 1 < len(work):
                scores = score(*work[i + 1])
            e = jnp.exp2(s - m_row[h])
            live_a, live_b = live_rows(h, t)
            p = jnp.concatenate([jnp.where(live_a, e[:bk], 0.0), jnp.where(live_b, e[bk:], 0.0)], axis=0)
            ls[h] = ls[h] + jnp.sum(p, axis=0, keepdims=True)
            acc_ref[h] += _dot(vt_ref[h, t], p.astype(BF16))
        return tuple(ls)

    l_f = lax.fori_loop(0, t_own // 2, lambda i, ls: accumulate([2 * i, 2 * i + 1], ls), tuple(l0))
    l_f = lax.cond(t_own % 2 == 1, lambda ls: accumulate([t_own - 1], ls), lambda ls: ls, l_f)
    for h, lo, hi in heads:
        yb_ref[:, lo:hi] = (acc_ref[h] / l_f[h]).T.astype(BF16)

    o_ref[...] += _dot(yb_ref[...], wo_ref[GROUP_W:, :])


def _moba(q, k_tiles, vt_tiles, k_mean, k_absmax, x, ya, w_out):
    bsz, seq = q.shape[0], q.shape[1]
    n_blk, n_tile = seq // MB_BLOCK, seq // MB_TILE
    once = pl.Buffered(1)
    tile = lambda b, i: (b, i, 0)
    return pl.pallas_call(
        functools.partial(_moba_body, n_blk=n_blk),
        out_shape=jax.ShapeDtypeStruct((bsz, seq, D_MODEL), F32),
        grid=(bsz, n_tile),
        in_specs=[
            pl.BlockSpec((None, MB_TILE, GROUP_W), tile),
            pl.BlockSpec((None, N_HEADS, n_tile, MB_TILE, D_HEAD), lambda b, i: (b, 0, 0, 0, 0),
                         pipeline_mode=once),
            pl.BlockSpec((None, N_HEADS, n_tile, D_HEAD, MB_TILE), lambda b, i: (b, 0, 0, 0, 0),
                         pipeline_mode=once),
            pl.BlockSpec((None, n_blk, GROUP_W), lambda b, i: (b, 0, 0)),
            pl.BlockSpec((None, n_blk, GROUP_W), lambda b, i: (b, 0, 0)),
            pl.BlockSpec((None, MB_TILE, D_MODEL), tile),
            pl.BlockSpec((None, MB_TILE, GROUP_W), tile),
            pl.BlockSpec((D_MODEL, D_MODEL), lambda b, i: (0, 0), pipeline_mode=once),
        ],
        out_specs=pl.BlockSpec((None, MB_TILE, D_MODEL), tile),
        scratch_shapes=[pltpu.VMEM((N_HEADS, n_blk, MB_TILE), F32),
                        pltpu.VMEM((N_HEADS, MB_TILE, MB_TILE), F32),
                        pltpu.VMEM((N_HEADS, D_HEAD, MB_TILE), F32),
                        pltpu.VMEM((MB_TILE, GROUP_W), BF16)],
        compiler_params=pltpu.CompilerParams(
            dimension_semantics=("parallel", "arbitrary"), vmem_limit_bytes=VMEM_LIMIT),
        name="moba",
    )(q, k_tiles, vt_tiles, k_mean, k_absmax, x, ya, w_out)


def _rope_tables(seq):
    half = ROPE_DIM // 2
    inv_freq = np.float64(ROPE_THETA) ** (-np.arange(half, dtype=np.float64) * 2.0 / ROPE_DIM)
    ang = np.arange(seq, dtype=np.float64)[:, None] * inv_freq[None, :]
    cos, sin = np.cos(ang).astype(np.float32), np.sin(ang).astype(np.float32)
    rest = LANES - ROPE_DIM
    cos_t = np.concatenate([cos, cos, np.ones((seq, rest), np.float32)], axis=1)
    sin_t = np.concatenate([-sin, sin, np.zeros((seq, rest), np.float32)], axis=1)
    return jnp.asarray(cos_t), jnp.asarray(sin_t)


def kernel(x, ffn_norm, ffn_w_in, ffn_w_out, mix_norm, ab_w_in, ab_w_out, hgrn_lb_logits, hgrn_out_norm,
           conv_w, conv_b, cd_w_in, cd_w_out, mlstm_gate_bias, mlstm_out_norm, final_norm):
    bsz, seq, d = x.shape
    depth = ffn_norm.shape[0]
    t = bsz * seq
    xt = x.reshape(t, d)
    cos_t, sin_t = _rope_tables(seq)

    for layer in range(depth):
        xt = _ffn(xt, ffn_norm[layer, 0], ffn_w_in, ffn_w_out, layer, 0, final_norm, False)
        x3 = xt.reshape(bsz, seq, d)
        if layer % 2 == 0:
            e = layer // 2
            x3 = _mix_ab(x3, mix_norm[layer], ab_w_in[e].astype(BF16), hgrn_lb_logits, hgrn_out_norm[e],
                         conv_w[e], conv_b[e], ab_w_out[e].astype(BF16), layer)
        else:
            o = layer // 2
            w = cd_w_in[o]
            n_ml = 4 * GROUP_W
            pad = jnp.zeros((d, LANES - N_HEADS), BF16)
            w_ml = w[:, :n_ml].astype(BF16)
            w_mb = w[:, n_ml + 2 * N_HEADS:].astype(BF16)
            w_gate = jnp.concatenate([w[:, n_ml:n_ml + N_HEADS].astype(BF16), pad,
                                      w[:, n_ml + N_HEADS:n_ml + 2 * N_HEADS].astype(BF16), pad], axis=1)
            bias_rows = jnp.pad(mlstm_gate_bias[o], ((0, 0), (0, LANES - N_HEADS)))
            ya, q_r, k_tiles, vt_tiles, k_mean, k_absmax = _mix_cd(
                x3, mix_norm[layer], w_ml, w_mb, w_gate, bias_rows, mlstm_out_norm[o], cos_t, sin_t)
            n_blk = seq // MB_BLOCK
            x3 = _moba(q_r, k_tiles, vt_tiles,
                       k_mean.reshape(bsz, n_blk, GROUP_W), k_absmax.reshape(bsz, n_blk, GROUP_W),
                       x3, ya, cd_w_out[o].astype(BF16))
        xt = x3.reshape(t, d)
        xt = _ffn(xt, ffn_norm[layer, 1], ffn_w_in, ffn_w_out, layer, 1, final_norm, layer == depth - 1)
    return xt.reshape(bsz, seq, d)
```

```python
import functools

import jax
import jax.numpy as jnp
import numpy as np
from jax import lax
from jax.experimental import pallas as pl
from jax.experimental.pallas import tpu as pltpu

F32 = jnp.float32
BF16 = jnp.bfloat16

D_MODEL = 1024
D_FF = 2816
GROUP_W = 512
N_HEADS = 4
D_HEAD = 128
RMS_EPS = 1e-6
HG_CHUNK = 32
ML_CHUNK = 64
MB_BLOCK = 256
MB_TOPK = 3
CONV_W = 3
ROPE_THETA = 500000.0
ROPE_DIM = D_HEAD // 4

LANES = 128
VMEM_LIMIT = 48 * 1024 * 1024

FFN_TM = 1024
FFN_SUB = 512
FFN_STAGE_IN = 32
FFN_STAGE_OUT = 128
FFN_STAGE_SLOTS = 4
FFN_CH = 256
PROJ_CH = 256
SEQ_T = 256
MB_TILE = 2 * MB_BLOCK
MB_BOUND_WINDOW = 80.0
LOG2E = 1.4426950408889634

AB_COLS = 7 * GROUP_W
CD_COLS = 7 * GROUP_W + 2 * LANES


def _rms(x, gain):
    return x * lax.rsqrt(jnp.mean(x * x, axis=-1, keepdims=True) + RMS_EPS) * gain


def _dot(a, b):
    return jnp.dot(a, b, preferred_element_type=F32)


def _dot_nt(a, b):
    return lax.dot_general(a, b, (((1,), (1,)), ((), ())), preferred_element_type=F32)


def _dot_mask(mask01, x):
    hi = x.astype(BF16)
    lo = (x - hi.astype(F32)).astype(BF16)
    return _dot(mask01, hi) + _dot(mask01, lo)


def _chunk_last(x, chunk):
    rows = x.shape[0]
    return jnp.concatenate([jnp.broadcast_to(x[r0 + chunk - 1:r0 + chunk, :], (chunk, x.shape[1]))
                            for r0 in range(0, rows, chunk)], axis=0)


def _chunk_masks(n, chunk):
    r = lax.broadcasted_iota(jnp.int32, (n, n), 0)
    c = lax.broadcasted_iota(jnp.int32, (n, n), 1)
    same = (r // chunk) == (c // chunk)
    return jnp.where(same, jnp.where(c <= r, 1, 0), 0) > 0, same


def _expand_mask(n, chunk, width):
    n_c = n // chunk
    r = lax.broadcasted_iota(jnp.int32, (n, n_c * width), 0)
    c = lax.broadcasted_iota(jnp.int32, (n, n_c * width), 1)
    return (r // chunk) == (c // width)


def _stream_rows(src_hbm, stage_ref, sem, consume):
    slots, rows = stage_ref.shape[0], stage_ref.shape[1]
    n = src_hbm.shape[0] // rows

    def copy(k):
        return pltpu.make_async_copy(src_hbm.at[pl.ds(k * rows, rows), :], stage_ref.at[k % slots],
                                     sem.at[k % slots])

    for k in range(min(slots - 1, n)):
        copy(k).start()
    for k in range(n):
        if k + slots - 1 < n:
            copy(k + slots - 1).start()
        copy(k).wait()
        consume(k * rows, stage_ref[k % slots])


def _ffn_body(x_ref, g_ref, w_in_hbm, w_out_hbm, fg_ref, o_ref, wg_ref, wu_ref, wo_ref, act_ref,
              stage_in, stage_out, sem_in, sem_out, *, layer, which, final_norm):
    @pl.when(pl.program_id(0) == 0)
    def _():
        def keep_in(r0, chunk):
            wg_ref[pl.ds(r0, chunk.shape[0]), :] = chunk[:, :D_FF].astype(BF16)
            wu_ref[pl.ds(r0, chunk.shape[0]), :] = chunk[:, D_FF:].astype(BF16)

        def keep_out(r0, chunk):
            wo_ref[pl.ds(r0, chunk.shape[0]), :] = chunk.astype(BF16)

        _stream_rows(w_in_hbm.at[layer, which], stage_in, sem_in, keep_in)
        _stream_rows(w_out_hbm.at[layer, which], stage_out, sem_out, keep_out)

    for r0 in range(0, FFN_TM, FFN_SUB):
        rows = pl.ds(r0, FFN_SUB)
        h = _rms(x_ref[rows, :], g_ref[...]).astype(BF16)
        for c0 in range(0, D_FF, FFN_CH):
            gate = _dot(h, wg_ref[:, c0:c0 + FFN_CH])
            up = _dot(h, wu_ref[:, c0:c0 + FFN_CH])
            act_ref[rows, c0:c0 + FFN_CH] = (gate * jax.nn.sigmoid(gate) * up).astype(BF16)
        y = x_ref[rows, :] + 0.5 * _dot(act_ref[rows, :], wo_ref[...])
        if final_norm:
            y = _rms(y, fg_ref[...])
        o_ref[rows, :] = y


def _ffn(x, gain, w_in, w_out, layer, which, final_gain, final_norm):
    t = x.shape[0]
    return pl.pallas_call(
        functools.partial(_ffn_body, layer=layer, which=which, final_norm=final_norm),
        out_shape=jax.ShapeDtypeStruct((t, D_MODEL), F32),
        grid=(t // FFN_TM,),
        in_specs=[
            pl.BlockSpec((FFN_TM, D_MODEL), lambda i: (i, 0)),
            pl.BlockSpec((1, D_MODEL), lambda i: (0, 0)),
            pl.BlockSpec(memory_space=pl.ANY),
            pl.BlockSpec(memory_space=pl.ANY),
            pl.BlockSpec((1, D_MODEL), lambda i: (0, 0)),
        ],
        out_specs=pl.BlockSpec((FFN_TM, D_MODEL), lambda i: (i, 0)),
        scratch_shapes=[pltpu.VMEM((D_MODEL, D_FF), BF16), pltpu.VMEM((D_MODEL, D_FF), BF16),
                        pltpu.VMEM((D_FF, D_MODEL), BF16), pltpu.VMEM((FFN_TM, D_FF), BF16),
                        pltpu.VMEM((FFN_STAGE_SLOTS, FFN_STAGE_IN, 2 * D_FF), F32),
                        pltpu.VMEM((FFN_STAGE_SLOTS, FFN_STAGE_OUT, D_MODEL), F32),
                        pltpu.SemaphoreType.DMA((FFN_STAGE_SLOTS,)), pltpu.SemaphoreType.DMA((FFN_STAGE_SLOTS,))],
        compiler_params=pltpu.CompilerParams(
            dimension_semantics=("arbitrary",), vmem_limit_bytes=VMEM_LIMIT),
        name="ffn",
    )(x, gain.reshape(1, D_MODEL), w_in, w_out, final_gain.reshape(1, D_MODEL))


class _ChunkedProjection:
    def __init__(self, x_ref, g_ref, w_refs, p_ref):
        self.h = _rms(x_ref[...], g_ref[...]).astype(BF16)
        self.p_ref = p_ref
        self.todo = [(w_ref, c0) for w_ref in w_refs for c0 in range(0, w_ref.shape[1], PROJ_CH)]
        self.done = 0

    def emit(self, n_chunks):
        end = len(self.todo) if n_chunks is None else min(len(self.todo), self.done + n_chunks)
        for i in range(self.done, end):
            w_ref, c0 = self.todo[i]
            self.p_ref[:, i * PROJ_CH:(i + 1) * PROJ_CH] = _dot(self.h, w_ref[:, c0:c0 + PROJ_CH])
        self.done = end


def _project_in(x_ref, g_ref, w_refs, p_ref):
    _ChunkedProjection(x_ref, g_ref, w_refs, p_ref).emit(None)


def _with_pipelined_projection(x_ref, xn_ref, g_ref, w_refs, pa_ref, pb_ref, tile_fn):
    s_idx = pl.program_id(1)

    @pl.when(s_idx == 0)
    def _():
        _project_in(x_ref, g_ref, w_refs, pa_ref)

    def branch(parity, p_cur, p_nxt):
        @pl.when(s_idx % 2 == parity)
        def _():
            ahead = _ChunkedProjection(xn_ref, g_ref, w_refs, p_nxt)
            tile_fn(p_cur, ahead)
            ahead.emit(None)

    branch(0, pa_ref, pb_ref)
    branch(1, pb_ref, pa_ref)


def _next_tile(n_s):
    return lambda b, s: (b, jnp.minimum(s + 1, n_s - 1), 0)


def _mix_ab_body(x_ref, xn_ref, g_ref, w_ref, lbl_ref, hgn_ref, cw_ref, cb_ref, wo_ref, o_ref,
                 pa_ref, pb_ref, y_ref, st_ref, zb_ref, *, layer):
    tile = functools.partial(_mix_ab_tile, x_ref=x_ref, lbl_ref=lbl_ref, hgn_ref=hgn_ref, cw_ref=cw_ref,
                             cb_ref=cb_ref, wo_ref=wo_ref, o_ref=o_ref, y_ref=y_ref, st_ref=st_ref,
                             zb_ref=zb_ref, layer=layer)
    _with_pipelined_projection(x_ref, xn_ref, g_ref, (w_ref,), pa_ref, pb_ref, tile)


def _mix_ab_tile(p_ref, ahead, *, x_ref, lbl_ref, hgn_ref, cw_ref, cb_ref, wo_ref, o_ref, y_ref, st_ref, zb_ref, layer):
    s_idx = pl.program_id(1)
    n_c = SEQ_T // HG_CHUNK

    @pl.when(s_idx == 0)
    def _():
        st_ref[...] = jnp.zeros_like(st_ref)
        zb_ref[0:8, :] = jnp.zeros((8, GROUP_W), F32)

    lg = lbl_ref[...]
    ex = jnp.exp(lg - jnp.max(lg, axis=0, keepdims=True))
    sm = ex / jnp.sum(ex, axis=0, keepdims=True)
    lb = jnp.sum(sm[0:layer + 1, :], axis=0, keepdims=True)

    tril, _ = _chunk_masks(SEQ_T, HG_CHUNK)
    tril01 = jnp.where(tril, 1.0, 0.0).astype(BF16)
    emask = _expand_mask(SEQ_T, HG_CHUNK, D_HEAD)

    f = lb + (1.0 - lb) * jax.nn.sigmoid(p_ref[:, GROUP_W:2 * GROUP_W])
    logf = jnp.log(f)
    b_all = _dot_mask(tril01, logf)
    e_all = _chunk_last(b_all, HG_CHUNK)
    ahead.emit(2)

    for h in range(N_HEADS):
        ahead.emit(1)
        lo, hi = h * D_HEAD, (h + 1) * D_HEAD
        q = p_ref[:, lo:hi]
        v = p_ref[:, 2 * GROUP_W + lo:2 * GROUP_W + hi]
        g = p_ref[:, 3 * GROUP_W + lo:3 * GROUP_W + hi]
        b = b_all[:, lo:hi]
        e = e_all[:, lo:hi]
        kk = 1.0 - f[:, lo:hi]
        q_dec = (q * jax.nn.sigmoid(q) * jnp.exp(b)).astype(BF16)
        k_dec = (kk * jnp.exp(-b)).astype(BF16)
        k_end = (kk * jnp.exp(e - b)).astype(BF16)
        v_t = v.T.astype(BF16)

        attn = jnp.where(tril, _dot_nt(q_dec, k_dec), 0.0)
        ahead.emit(1)
        o_t = _dot_nt(v_t, attn.astype(BF16))

        k_exp = jnp.where(emask, jnp.tile(k_end, (1, n_c)), jnp.zeros((), BF16))
        d_all = _dot(v_t, k_exp)
        ahead.emit(1)

        st = st_ref[h]
        prev = []
        for c in range(n_c):
            prev.append(st)
            decay = jnp.exp(e[c * HG_CHUNK:c * HG_CHUNK + 1, :])
            st = decay * st + d_all[:, c * D_HEAD:(c + 1) * D_HEAD]
        st_ref[h] = st
        s_prev = jnp.concatenate(prev, axis=1).astype(BF16)
        q_exp = jnp.where(emask, jnp.tile(q_dec, (1, n_c)), jnp.zeros((), BF16))
        o = (o_t + _dot_nt(s_prev, q_exp)).T

        o = _rms(o, hgn_ref[:, lo:hi]) * (g * jax.nn.sigmoid(g))
        y_ref[:, lo:hi] = o.astype(BF16)

    z = p_ref[:, 5 * GROUP_W:6 * GROUP_W] * p_ref[:, 6 * GROUP_W:7 * GROUP_W]
    zb_ref[8:SEQ_T + 8, :] = z
    z1 = zb_ref[7:SEQ_T + 7, :]
    z2 = zb_ref[6:SEQ_T + 6, :]
    y = cb_ref[...] + cw_ref[0:1, :] * z2 + cw_ref[1:2, :] * z1 + cw_ref[2:3, :] * z
    y_ref[:, GROUP_W:] = (p_ref[:, 4 * GROUP_W:5 * GROUP_W] * y).astype(BF16)
    zb_ref[0:8, :] = zb_ref[SEQ_T:SEQ_T + 8, :]

    o_ref[...] = x_ref[...] + _dot(y_ref[...], wo_ref[...])


def _mix_ab(x, gain, w_in, lb_logits, hg_norm, conv_w, conv_b, w_out, layer):
    bsz, seq = x.shape[0], x.shape[1]
    n_l = lb_logits.shape[0]
    row = lambda b, s: (0, 0)
    once = pl.Buffered(1)
    return pl.pallas_call(
        functools.partial(_mix_ab_body, layer=layer),
        out_shape=jax.ShapeDtypeStruct((bsz, seq, D_MODEL), F32),
        grid=(bsz, seq // SEQ_T),
        in_specs=[
            pl.BlockSpec((None, SEQ_T, D_MODEL), lambda b, s: (b, s, 0)),
            pl.BlockSpec((None, SEQ_T, D_MODEL), _next_tile(seq // SEQ_T)),
            pl.BlockSpec((1, D_MODEL), row),
            pl.BlockSpec((D_MODEL, AB_COLS), row, pipeline_mode=once),
            pl.BlockSpec((n_l, GROUP_W), row),
            pl.BlockSpec((1, GROUP_W), row),
            pl.BlockSpec((CONV_W, GROUP_W), row),
            pl.BlockSpec((1, GROUP_W), row),
            pl.BlockSpec((D_MODEL, D_MODEL), row, pipeline_mode=once),
        ],
        out_specs=pl.BlockSpec((None, SEQ_T, D_MODEL), lambda b, s: (b, s, 0)),
        scratch_shapes=[pltpu.VMEM((SEQ_T, AB_COLS), F32), pltpu.VMEM((SEQ_T, AB_COLS), F32),
                        pltpu.VMEM((SEQ_T, D_MODEL), BF16),
                        pltpu.VMEM((N_HEADS, D_HEAD, D_HEAD), F32), pltpu.VMEM((SEQ_T + 8, GROUP_W), F32)],
        compiler_params=pltpu.CompilerParams(
            dimension_semantics=("arbitrary", "arbitrary"), vmem_limit_bytes=VMEM_LIMIT),
        name="mix_ab",
    )(x, x, gain.reshape(1, D_MODEL), w_in, lb_logits, hg_norm.reshape(1, GROUP_W), conv_w,
      conv_b.reshape(1, GROUP_W), w_out)


def _mix_cd_body(x_ref, xn_ref, g_ref, w_ml_ref, w_mb_ref, w_gate_ref, bias_ref, mln_ref, cos_ref, sin_ref,
                 o_ref, qo_ref, ko_ref, vt_ref, km_ref, ka_ref, pa_ref, pb_ref, c_ref, n_ref, m_ref):
    tile = functools.partial(_mix_cd_tile, bias_ref=bias_ref, mln_ref=mln_ref, cos_ref=cos_ref, sin_ref=sin_ref,
                             o_ref=o_ref, qo_ref=qo_ref, ko_ref=ko_ref, vt_ref=vt_ref, km_ref=km_ref,
                             ka_ref=ka_ref, c_ref=c_ref, n_ref=n_ref, m_ref=m_ref)
    _with_pipelined_projection(x_ref, xn_ref, g_ref, (w_ml_ref, w_mb_ref, w_gate_ref), pa_ref, pb_ref, tile)


def _mix_cd_tile(p_ref, ahead, *, bias_ref, mln_ref, cos_ref, sin_ref, o_ref, qo_ref, ko_ref, vt_ref, km_ref, ka_ref,
                 c_ref, n_ref, m_ref):
    s_idx = pl.program_id(1)
    n_c = SEQ_T // ML_CHUNK
    gi_ref = p_ref.at[:, 7 * GROUP_W:7 * GROUP_W + LANES]
    gf_ref = p_ref.at[:, 7 * GROUP_W + LANES:7 * GROUP_W + 2 * LANES]

    @pl.when(s_idx == 0)
    def _():
        c_ref[...] = jnp.zeros_like(c_ref)
        n_ref[...] = jnp.zeros_like(n_ref)
        m_ref[...] = jnp.zeros_like(m_ref)

    cos_t, sin_t = cos_ref[...], sin_ref[...]
    for h in range(N_HEADS):
        ahead.emit(1)
        lo, hi = h * D_HEAD, (h + 1) * D_HEAD
        q = _rope(p_ref[:, 4 * GROUP_W + lo:4 * GROUP_W + hi], cos_t, sin_t) * (D_HEAD ** -0.5 * LOG2E)
        k = _rope(p_ref[:, 5 * GROUP_W + lo:5 * GROUP_W + hi], cos_t, sin_t)
        qo_ref[:, lo:hi] = q.astype(BF16)
        ko_ref[h] = k.astype(BF16)
        vt_ref[h] = p_ref[:, 6 * GROUP_W + lo:6 * GROUP_W + hi].T.astype(BF16)
        km_ref[:, lo:hi] = jnp.mean(k, axis=0, keepdims=True)
        ka_ref[:, lo:hi] = jnp.max(jnp.abs(k), axis=0, keepdims=True)

    tril, same = _chunk_masks(SEQ_T, ML_CHUNK)
    tril01 = jnp.where(tril, 1.0, 0.0).astype(BF16)
    same01 = jnp.where(same, 1.0, 0.0).astype(BF16)
    emask = _expand_mask(SEQ_T, ML_CHUNK, D_HEAD)

    log_i = gi_ref[...] + bias_ref[0:1, :]
    log_f = jax.nn.log_sigmoid(gf_ref[...] + bias_ref[1:2, :])
    b_col = _dot_mask(tril01, log_f)
    e_col = _dot_mask(same01, log_f)
    w_end = e_col - b_col + log_i

    m = m_ref[...]
    m_prev_rows, m_new_rows, a_rows = [], [], []
    for c in range(n_c):
        r0 = c * ML_CHUNK
        be = e_col[r0:r0 + 1, :]
        m_end = jnp.max(w_end[r0:r0 + ML_CHUNK, :], axis=0, keepdims=True)
        m_new = jnp.maximum(be + m, m_end)
        a_rows.append(jnp.exp(be + m - m_new))
        m_prev_rows.append(jnp.broadcast_to(m, (ML_CHUNK, LANES)))
        m_new_rows.append(jnp.broadcast_to(m_new, (ML_CHUNK, LANES)))
        m = m_new
    m_ref[...] = m
    log_inter = b_col + jnp.concatenate(m_prev_rows, axis=0)
    wk_scale = jnp.exp(w_end - jnp.concatenate(m_new_rows, axis=0))
    r_rows = (b_col - log_i).T

    for h in range(N_HEADS):
        ahead.emit(3)
        lo, hi = h * D_HEAD, (h + 1) * D_HEAD
        q = p_ref[:, lo:hi] * (D_HEAD ** -0.5)
        k = p_ref[:, GROUP_W + lo:GROUP_W + hi]
        v = p_ref[:, 2 * GROUP_W + lo:2 * GROUP_W + hi]
        og = p_ref[:, 3 * GROUP_W + lo:3 * GROUP_W + hi]
        q16, k16, v16 = q.astype(BF16), k.astype(BF16), v.astype(BF16)

        d_log = jnp.where(tril, b_col[:, h:h + 1] - r_rows[h:h + 1, :], -jnp.inf)
        linter = log_inter[:, h:h + 1]
        m_t = jnp.maximum(linter, jnp.max(d_log, axis=-1, keepdims=True))
        w = jnp.exp(d_log - m_t) * _dot_nt(q16, k16)
        a_in = jnp.exp(linter - m_t)
        num = _dot(w.astype(BF16), v16)
        den = jnp.sum(w, axis=-1, keepdims=True)

        wk = wk_scale[:, h:h + 1] * k
        v_exp = jnp.where(emask, jnp.tile(v16, (1, n_c)), jnp.zeros((), BF16))
        dc_all = _dot(wk.T.astype(BF16), v_exp)

        c_mat = c_ref[h]
        n_vec = n_ref[h]
        c_prev, n_prev = [], []
        for c in range(n_c):
            r0 = c * ML_CHUNK
            c_prev.append(c_mat)
            n_prev.append(jnp.broadcast_to(n_vec, (ML_CHUNK, D_HEAD)))
            a = a_rows[c][:, h:h + 1]
            c_mat = a * c_mat + dc_all[:, c * D_HEAD:(c + 1) * D_HEAD]
            n_vec = a * n_vec + jnp.sum(wk[r0:r0 + ML_CHUNK, :], axis=0, keepdims=True)
        c_ref[h] = c_mat
        n_ref[h] = n_vec

        q_exp = jnp.where(emask, jnp.tile(q16, (1, n_c)), jnp.zeros((), BF16))
        q_c = _dot(q_exp, jnp.concatenate(c_prev, axis=0).astype(BF16))
        q_n = jnp.sum(q * jnp.concatenate(n_prev, axis=0), axis=-1, keepdims=True)
        num = num + a_in * q_c
        den = den + a_in * q_n
        hh = num / jnp.maximum(jnp.abs(den), jnp.exp(-m_t))
        o_ref[:, lo:hi] = (_rms(hh, mln_ref[:, lo:hi]) * jax.nn.sigmoid(og)).astype(BF16)


def _mix_cd(x, gain, w_ml, w_mb, w_gate, bias_rows, ml_norm, cos_t, sin_t):
    bsz, seq = x.shape[0], x.shape[1]
    n_blk, n_tile = seq // MB_BLOCK, seq // MB_TILE
    assert SEQ_T == MB_BLOCK and MB_TILE == 2 * MB_BLOCK
    row = lambda b, s: (0, 0)
    stat = pl.BlockSpec((None, None, 1, GROUP_W), lambda b, s: (b, s, 0, 0))
    return pl.pallas_call(
        _mix_cd_body,
        out_shape=[
            jax.ShapeDtypeStruct((bsz, seq, GROUP_W), BF16),
            jax.ShapeDtypeStruct((bsz, seq, GROUP_W), BF16),
            jax.ShapeDtypeStruct((bsz, N_HEADS, n_tile, MB_TILE, D_HEAD), BF16),
            jax.ShapeDtypeStruct((bsz, N_HEADS, n_tile, D_HEAD, MB_TILE), BF16),
            jax.ShapeDtypeStruct((bsz, n_blk, 1, GROUP_W), F32),
            jax.ShapeDtypeStruct((bsz, n_blk, 1, GROUP_W), F32),
        ],
        grid=(bsz, seq // SEQ_T),
        in_specs=[
            pl.BlockSpec((None, SEQ_T, D_MODEL), lambda b, s: (b, s, 0)),
            pl.BlockSpec((None, SEQ_T, D_MODEL), _next_tile(seq // SEQ_T)),
            pl.BlockSpec((1, D_MODEL), row),
            pl.BlockSpec(w_ml.shape, row, pipeline_mode=pl.Buffered(1)),
            pl.BlockSpec(w_mb.shape, row, pipeline_mode=pl.Buffered(1)),
            pl.BlockSpec(w_gate.shape, row, pipeline_mode=pl.Buffered(1)),
            pl.BlockSpec((2, LANES), row),
            pl.BlockSpec((1, GROUP_W), row),
            pl.BlockSpec((SEQ_T, LANES), lambda b, s: (s, 0)),
            pl.BlockSpec((SEQ_T, LANES), lambda b, s: (s, 0)),
        ],
        out_specs=[
            pl.BlockSpec((None, SEQ_T, GROUP_W), lambda b, s: (b, s, 0)),
            pl.BlockSpec((None, SEQ_T, GROUP_W), lambda b, s: (b, s, 0)),
            pl.BlockSpec((None, N_HEADS, None, MB_BLOCK, D_HEAD), lambda b, s: (b, 0, s // 2, s % 2, 0)),
            pl.BlockSpec((None, N_HEADS, None, D_HEAD, MB_BLOCK), lambda b, s: (b, 0, s // 2, 0, s % 2)),
            stat, stat,
        ],
        scratch_shapes=[pltpu.VMEM((SEQ_T, CD_COLS), F32), pltpu.VMEM((SEQ_T, CD_COLS), F32),
                        pltpu.VMEM((N_HEADS, D_HEAD, D_HEAD), F32),
                        pltpu.VMEM((N_HEADS, 1, D_HEAD), F32),
                        pltpu.VMEM((1, LANES), F32)],
        compiler_params=pltpu.CompilerParams(
            dimension_semantics=("arbitrary", "arbitrary"), vmem_limit_bytes=VMEM_LIMIT),
        name="mix_cd",
    )(x, x, gain.reshape(1, D_MODEL), w_ml, w_mb, w_gate, bias_rows, ml_norm.reshape(1, GROUP_W), cos_t, sin_t)


def _rope(x, cos_t, sin_t):
    lane = lax.broadcasted_iota(jnp.int32, x.shape, 1)
    half = ROPE_DIM // 2
    swapped = jnp.where(lane < half, pltpu.roll(x, LANES - half, axis=1), pltpu.roll(x, half, axis=1))
    return x * cos_t + swapped * sin_t


def _moba_body(q_ref, k_ref, vt_ref, km_ref, ka_ref, x_ref, ya_ref, wo_ref, o_ref,
               sel_ref, sd_ref, acc_ref, yb_ref, *, n_blk):
    t_own = pl.program_id(1)
    heads = [(h, h * D_HEAD, (h + 1) * D_HEAD) for h in range(N_HEADS)]
    bk = MB_BLOCK
    o_ref[...] = x_ref[...] + _dot(ya_ref[...], wo_ref[:GROUP_W, :])

    blk = lax.broadcasted_iota(jnp.int32, (n_blk, MB_TILE), 0)
    lane = lax.broadcasted_iota(jnp.int32, (n_blk, MB_TILE), 1)
    own = 2 * t_own + lane // bk
    bound_past = []
    for h, lo, hi in heads:
        q = q_ref[:, lo:hi]
        km = km_ref[:, lo:hi]
        km_hi = km.astype(BF16)
        km_lo = (km - km_hi.astype(F32)).astype(BF16)
        gate = _dot_nt(km_hi, q) + _dot_nt(km_lo, q)
        gate = jnp.where(blk < own, gate, -jnp.inf)
        sel = jnp.zeros(gate.shape, F32)
        for _ in range(MB_TOPK):
            mx = jnp.max(gate, axis=0, keepdims=True)
            idx = jnp.min(jnp.where(gate == mx, blk, n_blk), axis=0, keepdims=True)
            pick = blk == jnp.where(mx > -jnp.inf, idx, -1)
            sel = jnp.where(pick, 1.0, sel)
            gate = jnp.where(pick, -jnp.inf, gate)
        sel_ref[h] = sel
        bound = _dot_nt(ka_ref[:, lo:hi].astype(BF16), jnp.abs(q)) * (1.0 + 2.0 ** -6)
        bound_past.append(jnp.max(jnp.where(sel > 0.0, bound, -jnp.inf), axis=0, keepdims=True))

    def live_rows(h, t):
        return sel_ref[h, pl.ds(2 * t, 1), :] > 0.0, sel_ref[h, pl.ds(2 * t + 1, 1), :] > 0.0

    kpos = lax.broadcasted_iota(jnp.int32, (MB_TILE, MB_TILE), 0)
    qpos = lax.broadcasted_iota(jnp.int32, (MB_TILE, MB_TILE), 1)
    kb, qb = kpos // bk, qpos // bk
    causal = jnp.where(kb == qb, jnp.where(kpos <= qpos, 1.0, 0.0), 0.0)
    cross = jnp.where(kb < qb, 1.0, 0.0)

    def visible(h):
        return (causal + cross * sel_ref[h, pl.ds(2 * t_own, 1), :]) > 0.0

    m_diag = []
    for h, lo, hi in heads:
        s = _dot_nt(k_ref[h, t_own], q_ref[:, lo:hi])
        sd_ref[h] = s
        m_diag.append(jnp.max(jnp.where(visible(h), s, -jnp.inf), axis=0, keepdims=True))

    def exact_past_max():
        def tile_max(t, ms):
            out = []
            for h, lo, hi in heads:
                s = _dot_nt(k_ref[h, t], q_ref[:, lo:hi])
                live_a, live_b = live_rows(h, t)
                m_a = jnp.where(live_a, jnp.max(s[:bk], axis=0, keepdims=True), -jnp.inf)
                m_b = jnp.where(live_b, jnp.max(s[bk:], axis=0, keepdims=True), -jnp.inf)
                out.append(jnp.maximum(ms[h], jnp.maximum(m_a, m_b)))
            return tuple(out)

        return lax.fori_loop(0, t_own, tile_max, tuple(jnp.full((1, MB_TILE), -jnp.inf, F32) for _ in heads))

    slack = functools.reduce(jnp.maximum, [jnp.max(bound_past[h] - m_diag[h]) for h, _, _ in heads])
    m_past = lax.cond(slack > MB_BOUND_WINDOW, exact_past_max, lambda: tuple(bound_past))

    m_row, l0 = [], []
    for h, lo, hi in heads:
        m_h = jnp.maximum(m_diag[h], m_past[h])
        p = jnp.where(visible(h), jnp.exp2(sd_ref[h] - m_h), 0.0)
        m_row.append(m_h)
        l0.append(jnp.sum(p, axis=0, keepdims=True))
        acc_ref[h] = _dot(vt_ref[h, t_own], p.astype(BF16))

    def accumulate(tiles, ls):
        work = [(t, h, lo, hi) for t in tiles for h, lo, hi in heads]
        score = lambda t, h, lo, hi: _dot_nt(k_ref[h, t], q_ref[:, lo:hi])
        ls = list(ls)
        scores = score(*work[0])
        for i, (t, h, lo, hi) in enumerate(work):
            s = scores
            if i + 1 < len(work):
                scores = score(*work[i + 1])
            e = jnp.exp2(s - m_row[h])
            live_a, live_b = live_rows(h, t)
            p = jnp.concatenate([jnp.where(live_a, e[:bk], 0.0), jnp.where(live_b, e[bk:], 0.0)], axis=0)
            ls[h] = ls[h] + jnp.sum(p, axis=0, keepdims=True)
            acc_ref[h] += _dot(vt_ref[h, t], p.astype(BF16))
        return tuple(ls)

    l_f = lax.fori_loop(0, t_own // 2, lambda i, ls: accumulate([2 * i, 2 * i + 1], ls), tuple(l0))
    l_f = lax.cond(t_own % 2 == 1, lambda ls: accumulate([t_own - 1], ls), lambda ls: ls, l_f)
    for h, lo, hi in heads:
        yb_ref[:, lo:hi] = (acc_ref[h] / l_f[h]).T.astype(BF16)

    o_ref[...] += _dot(yb_ref[...], wo_ref[GROUP_W:, :])


def _moba(q, k_tiles, vt_tiles, k_mean, k_absmax, x, ya, w_out):
    bsz, seq = q.shape[0], q.shape[1]
    n_blk, n_tile = seq // MB_BLOCK, seq // MB_TILE
    once = pl.Buffered(1)
    tile = lambda b, i: (b, i, 0)
    return pl.pallas_call(
        functools.partial(_moba_body, n_blk=n_blk),
        out_shape=jax.ShapeDtypeStruct((bsz, seq, D_MODEL), F32),
        grid=(bsz, n_tile),
        in_specs=[
            pl.BlockSpec((None, MB_TILE, GROUP_W), tile),
            pl.BlockSpec((None, N_HEADS, n_tile, MB_TILE, D_HEAD), lambda b, i: (b, 0, 0, 0, 0),
                         pipeline_mode=once),
            pl.BlockSpec((None, N_HEADS, n_tile, D_HEAD, MB_TILE), lambda b, i: (b, 0, 0, 0, 0),
                         pipeline_mode=once),
            pl.BlockSpec((None, n_blk, GROUP_W), lambda b, i: (b, 0, 0)),
            pl.BlockSpec((None, n_blk, GROUP_W), lambda b, i: (b, 0, 0)),
            pl.BlockSpec((None, MB_TILE, D_MODEL), tile),
            pl.BlockSpec((None, MB_TILE, GROUP_W), tile),
            pl.BlockSpec((D_MODEL, D_MODEL), lambda b, i: (0, 0), pipeline_mode=once),
        ],
        out_specs=pl.BlockSpec((None, MB_TILE, D_MODEL), tile),
        scratch_shapes=[pltpu.VMEM((N_HEADS, n_blk, MB_TILE), F32),
                        pltpu.VMEM((N_HEADS, MB_TILE, MB_TILE), F32),
                        pltpu.VMEM((N_HEADS, D_HEAD, MB_TILE), F32),
                        pltpu.VMEM((MB_TILE, GROUP_W), BF16)],
        compiler_params=pltpu.CompilerParams(
            dimension_semantics=("parallel", "arbitrary"), vmem_limit_bytes=VMEM_LIMIT),
        name="moba",
    )(q, k_tiles, vt_tiles, k_mean, k_absmax, x, ya, w_out)


def _rope_tables(seq):
    half = ROPE_DIM // 2
    inv_freq = np.float64(ROPE_THETA) ** (-np.arange(half, dtype=np.float64) * 2.0 / ROPE_DIM)
    ang = np.arange(seq, dtype=np.float64)[:, None] * inv_freq[None, :]
    cos, sin = np.cos(ang).astype(np.float32), np.sin(ang).astype(np.float32)
    rest = LANES - ROPE_DIM
    cos_t = np.concatenate([cos, cos, np.ones((seq, rest), np.float32)], axis=1)
    sin_t = np.concatenate([-sin, sin, np.zeros((seq, rest), np.float32)], axis=1)
    return jnp.asarray(cos_t), jnp.asarray(sin_t)


def kernel(x, ffn_norm, ffn_w_in, ffn_w_out, mix_norm, ab_w_in, ab_w_out, hgrn_lb_logits, hgrn_out_norm,
           conv_w, conv_b, cd_w_in, cd_w_out, mlstm_gate_bias, mlstm_out_norm, final_norm):
    bsz, seq, d = x.shape
    depth = ffn_norm.shape[0]
    t = bsz * seq
    xt = x.reshape(t, d)
    cos_t, sin_t = _rope_tables(seq)

    for layer in range(depth):
        xt = _ffn(xt, ffn_norm[layer, 0], ffn_w_in, ffn_w_out, layer, 0, final_norm, False)
        x3 = xt.reshape(bsz, seq, d)
        if layer % 2 == 0:
            e = layer // 2
            x3 = _mix_ab(x3, mix_norm[layer], ab_w_in[e].astype(BF16), hgrn_lb_logits, hgrn_out_norm[e],
                         conv_w[e], conv_b[e], ab_w_out[e].astype(BF16), layer)
        else:
            o = layer // 2
            w = cd_w_in[o]
            n_ml = 4 * GROUP_W
            pad = jnp.zeros((d, LANES - N_HEADS), BF16)
            w_ml = w[:, :n_ml].astype(BF16)
            w_mb = w[:, n_ml + 2 * N_HEADS:].astype(BF16)
            w_gate = jnp.concatenate([w[:, n_ml:n_ml + N_HEADS].astype(BF16), pad,
                                      w[:, n_ml + N_HEADS:n_ml + 2 * N_HEADS].astype(BF16), pad], axis=1)
            bias_rows = jnp.pad(mlstm_gate_bias[o], ((0, 0), (0, LANES - N_HEADS)))
            ya, q_r, k_tiles, vt_tiles, k_mean, k_absmax = _mix_cd(
                x3, mix_norm[layer], w_ml, w_mb, w_gate, bias_rows, mlstm_out_norm[o], cos_t, sin_t)
            n_blk = seq // MB_BLOCK
            x3 = _moba(q_r, k_tiles, vt_tiles,
                       k_mean.reshape(bsz, n_blk, GROUP_W), k_absmax.reshape(bsz, n_blk, GROUP_W),
                       x3, ya, cd_w_out[o].astype(BF16))
        xt = x3.reshape(t, d)
        xt = _ffn(xt, ffn_norm[layer, 1], ffn_w_in, ffn_w_out, layer, 1, final_norm, layer == depth - 1)
    return xt.reshape(bsz, seq, d)
```

```python
import functools

import jax
import jax.numpy as jnp
import numpy as np
from jax import lax
from jax.experimental import pallas as pl
from jax.experimental.pallas import tpu as pltpu

F32 = jnp.float32
BF16 = jnp.bfloat16

D_MODEL = 1024
D_FF = 2816
GROUP_W = 512
N_HEADS = 4
D_HEAD = 128
RMS_EPS = 1e-6
HG_CHUNK = 32
ML_CHUNK = 64
MB_BLOCK = 256
MB_TOPK = 3
CONV_W = 3
ROPE_THETA = 500000.0
ROPE_DIM = D_HEAD // 4

LANES = 128
VMEM_LIMIT = 48 * 1024 * 1024

FFN_TM = 1024
FFN_SUB = 512
FFN_CH = 256
PROJ_CH = 256
SEQ_T = 256
MB_TILE = 2 * MB_BLOCK
MB_BOUND_WINDOW = 80.0
LOG2E = 1.4426950408889634

AB_COLS = 7 * GROUP_W
CD_COLS = 7 * GROUP_W + 2 * LANES


def _rms(x, gain):
    return x * lax.rsqrt(jnp.mean(x * x, axis=-1, keepdims=True) + RMS_EPS) * gain


def _dot(a, b):
    return jnp.dot(a, b, preferred_element_type=F32)


def _dot_nt(a, b):
    return lax.dot_general(a, b, (((1,), (1,)), ((), ())), preferred_element_type=F32)


def _dot_mask(mask01, x):
    hi = x.astype(BF16)
    lo = (x - hi.astype(F32)).astype(BF16)
    return _dot(mask01, hi) + _dot(mask01, lo)


def _chunk_last(x, chunk):
    rows = x.shape[0]
    return jnp.concatenate([jnp.broadcast_to(x[r0 + chunk - 1:r0 + chunk, :], (chunk, x.shape[1]))
                            for r0 in range(0, rows, chunk)], axis=0)


def _chunk_masks(n, chunk):
    r = lax.broadcasted_iota(jnp.int32, (n, n), 0)
    c = lax.broadcasted_iota(jnp.int32, (n, n), 1)
    same = (r // chunk) == (c // chunk)
    return jnp.where(same, jnp.where(c <= r, 1, 0), 0) > 0, same


def _expand_mask(n, chunk, width):
    n_c = n // chunk
    r = lax.broadcasted_iota(jnp.int32, (n, n_c * width), 0)
    c = lax.broadcasted_iota(jnp.int32, (n, n_c * width), 1)
    return (r // chunk) == (c // width)


def _ffn_body(*refs, final_norm, cast_next):
    if cast_next:
        x_ref, g_ref, wg_ref, wu_ref, wo_ref, fg_ref, nin_ref, nout_ref, o_ref, nin16_ref, nout16_ref, act_ref = refs
        nin16_ref[...] = nin_ref[...].astype(BF16)
        nout16_ref[...] = nout_ref[...].astype(BF16)
    else:
        x_ref, g_ref, wg_ref, wu_ref, wo_ref, fg_ref, o_ref, act_ref = refs

    for r0 in range(0, FFN_TM, FFN_SUB):
        rows = pl.ds(r0, FFN_SUB)
        h = _rms(x_ref[rows, :], g_ref[...]).astype(BF16)
        for c0 in range(0, D_FF, FFN_CH):
            gate = _dot(h, wg_ref[:, c0:c0 + FFN_CH])
            up = _dot(h, wu_ref[:, c0:c0 + FFN_CH])
            act_ref[rows, c0:c0 + FFN_CH] = (gate * jax.nn.sigmoid(gate) * up).astype(BF16)
        y = x_ref[rows, :] + 0.5 * _dot(act_ref[rows, :], wo_ref[...])
        if final_norm:
            y = _rms(y, fg_ref[...])
        o_ref[rows, :] = y


def _ffn(x, gain, w_in16, w_out16, final_gain, final_norm, cast_next=None):
    t = x.shape[0]
    n_steps = t // FFN_TM
    once = pl.Buffered(1)
    in_specs = [
        pl.BlockSpec((FFN_TM, D_MODEL), lambda i: (i, 0)),
        pl.BlockSpec((1, D_MODEL), lambda i: (0, 0)),
        pl.BlockSpec((D_MODEL, D_FF), lambda i: (0, 0), pipeline_mode=once),
        pl.BlockSpec((D_MODEL, D_FF), lambda i: (0, 1), pipeline_mode=once),
        pl.BlockSpec((D_FF, D_MODEL), lambda i: (0, 0), pipeline_mode=once),
        pl.BlockSpec((1, D_MODEL), lambda i: (0, 0)),
    ]
    out_shape = [jax.ShapeDtypeStruct((t, D_MODEL), F32)]
    out_specs = [pl.BlockSpec((FFN_TM, D_MODEL), lambda i: (i, 0))]
    args = [x, gain.reshape(1, D_MODEL), w_in16, w_in16, w_out16, final_gain.reshape(1, D_MODEL)]
    if cast_next is not None:
        nxt_in, nxt_out, layer, which = cast_next
        r_in, r_out = D_MODEL // n_steps, D_FF // n_steps
        in_specs += [pl.BlockSpec((None, None, r_in, 2 * D_FF), lambda i: (layer, which, i, 0)),
                     pl.BlockSpec((None, None, r_out, D_MODEL), lambda i: (layer, which, i, 0))]
        out_shape += [jax.ShapeDtypeStruct((D_MODEL, 2 * D_FF), BF16), jax.ShapeDtypeStruct((D_FF, D_MODEL), BF16)]
        out_specs += [pl.BlockSpec((r_in, 2 * D_FF), lambda i: (i, 0)),
                      pl.BlockSpec((r_out, D_MODEL), lambda i: (i, 0))]
        args += [nxt_in, nxt_out]
    return pl.pallas_call(
        functools.partial(_ffn_body, final_norm=final_norm, cast_next=cast_next is not None),
        out_shape=out_shape,
        grid=(n_steps,),
        in_specs=in_specs,
        out_specs=out_specs,
        scratch_shapes=[pltpu.VMEM((FFN_TM, D_FF), BF16)],
        compiler_params=pltpu.CompilerParams(
            dimension_semantics=("parallel",), vmem_limit_bytes=VMEM_LIMIT),
        name="ffn",
    )(*args)


class _ChunkedProjection:
    def __init__(self, x_ref, g_ref, w_refs, p_ref):
        self.h = _rms(x_ref[...], g_ref[...]).astype(BF16)
        self.p_ref = p_ref
        self.todo = [(w_ref, c0) for w_ref in w_refs for c0 in range(0, w_ref.shape[1], PROJ_CH)]
        self.done = 0

    def emit(self, n_chunks):
        end = len(self.todo) if n_chunks is None else min(len(self.todo), self.done + n_chunks)
        for i in range(self.done, end):
            w_ref, c0 = self.todo[i]
            self.p_ref[:, i * PROJ_CH:(i + 1) * PROJ_CH] = _dot(self.h, w_ref[:, c0:c0 + PROJ_CH])
        self.done = end


def _project_in(x_ref, g_ref, w_refs, p_ref):
    _ChunkedProjection(x_ref, g_ref, w_refs, p_ref).emit(None)


def _with_pipelined_projection(x_ref, xn_ref, g_ref, w_refs, pa_ref, pb_ref, tile_fn):
    s_idx = pl.program_id(1)

    @pl.when(s_idx == 0)
    def _():
        _project_in(x_ref, g_ref, w_refs, pa_ref)

    def branch(parity, p_cur, p_nxt):
        @pl.when(s_idx % 2 == parity)
        def _():
            ahead = _ChunkedProjection(xn_ref, g_ref, w_refs, p_nxt)
            tile_fn(p_cur, ahead)
            ahead.emit(None)

    branch(0, pa_ref, pb_ref)
    branch(1, pb_ref, pa_ref)


def _next_tile(n_s):
    return lambda b, s: (b, jnp.minimum(s + 1, n_s - 1), 0)


def _mix_ab_body(x_ref, xn_ref, g_ref, w_ref, lbl_ref, hgn_ref, cw_ref, cb_ref, wo_ref, o_ref,
                 pa_ref, pb_ref, y_ref, st_ref, zb_ref, *, layer):
    tile = functools.partial(_mix_ab_tile, x_ref=x_ref, lbl_ref=lbl_ref, hgn_ref=hgn_ref, cw_ref=cw_ref,
                             cb_ref=cb_ref, wo_ref=wo_ref, o_ref=o_ref, y_ref=y_ref, st_ref=st_ref,
                             zb_ref=zb_ref, layer=layer)
    _with_pipelined_projection(x_ref, xn_ref, g_ref, (w_ref,), pa_ref, pb_ref, tile)


def _mix_ab_tile(p_ref, ahead, *, x_ref, lbl_ref, hgn_ref, cw_ref, cb_ref, wo_ref, o_ref, y_ref, st_ref, zb_ref, layer):
    s_idx = pl.program_id(1)
    n_c = SEQ_T // HG_CHUNK

    @pl.when(s_idx == 0)
    def _():
        st_ref[...] = jnp.zeros_like(st_ref)
        zb_ref[0:8, :] = jnp.zeros((8, GROUP_W), F32)

    lg = lbl_ref[...]
    ex = jnp.exp(lg - jnp.max(lg, axis=0, keepdims=True))
    sm = ex / jnp.sum(ex, axis=0, keepdims=True)
    lb = jnp.sum(sm[0:layer + 1, :], axis=0, keepdims=True)

    tril, _ = _chunk_masks(SEQ_T, HG_CHUNK)
    tril01 = jnp.where(tril, 1.0, 0.0).astype(BF16)
    emask = _expand_mask(SEQ_T, HG_CHUNK, D_HEAD)

    f = lb + (1.0 - lb) * jax.nn.sigmoid(p_ref[:, GROUP_W:2 * GROUP_W])
    logf = jnp.log(f)
    b_all = _dot_mask(tril01, logf)
    e_all = _chunk_last(b_all, HG_CHUNK)
    ahead.emit(2)

    for h in range(N_HEADS):
        ahead.emit(1)
        lo, hi = h * D_HEAD, (h + 1) * D_HEAD
        q = p_ref[:, lo:hi]
        v = p_ref[:, 2 * GROUP_W + lo:2 * GROUP_W + hi]
        g = p_ref[:, 3 * GROUP_W + lo:3 * GROUP_W + hi]
        b = b_all[:, lo:hi]
        e = e_all[:, lo:hi]
        kk = 1.0 - f[:, lo:hi]
        q_dec = (q * jax.nn.sigmoid(q) * jnp.exp(b)).astype(BF16)
        k_dec = (kk * jnp.exp(-b)).astype(BF16)
        k_end = (kk * jnp.exp(e - b)).astype(BF16)
        v_t = v.T.astype(BF16)

        attn = jnp.where(tril, _dot_nt(q_dec, k_dec), 0.0)
        ahead.emit(1)
        o_t = _dot_nt(v_t, attn.astype(BF16))

        k_exp = jnp.where(emask, jnp.tile(k_end, (1, n_c)), jnp.zeros((), BF16))
        d_all = _dot(v_t, k_exp)
        ahead.emit(1)

        st = st_ref[h]
        prev = []
        for c in range(n_c):
            prev.append(st)
            decay = jnp.exp(e[c * HG_CHUNK:c * HG_CHUNK + 1, :])
            st = decay * st + d_all[:, c * D_HEAD:(c + 1) * D_HEAD]
        st_ref[h] = st
        s_prev = jnp.concatenate(prev, axis=1).astype(BF16)
        q_exp = jnp.where(emask, jnp.tile(q_dec, (1, n_c)), jnp.zeros((), BF16))
        o = (o_t + _dot_nt(s_prev, q_exp)).T

        o = _rms(o, hgn_ref[:, lo:hi]) * (g * jax.nn.sigmoid(g))
        y_ref[:, lo:hi] = o.astype(BF16)

    z = p_ref[:, 5 * GROUP_W:6 * GROUP_W] * p_ref[:, 6 * GROUP_W:7 * GROUP_W]
    zb_ref[8:SEQ_T + 8, :] = z
    z1 = zb_ref[7:SEQ_T + 7, :]
    z2 = zb_ref[6:SEQ_T + 6, :]
    y = cb_ref[...] + cw_ref[0:1, :] * z2 + cw_ref[1:2, :] * z1 + cw_ref[2:3, :] * z
    y_ref[:, GROUP_W:] = (p_ref[:, 4 * GROUP_W:5 * GROUP_W] * y).astype(BF16)
    zb_ref[0:8, :] = zb_ref[SEQ_T:SEQ_T + 8, :]

    o_ref[...] = x_ref[...] + _dot(y_ref[...], wo_ref[...])


def _mix_ab(x, gain, w_in, lb_logits, hg_norm, conv_w, conv_b, w_out, layer):
    bsz, seq = x.shape[0], x.shape[1]
    n_l = lb_logits.shape[0]
    row = lambda b, s: (0, 0)
    once = pl.Buffered(1)
    return pl.pallas_call(
        functools.partial(_mix_ab_body, layer=layer),
        out_shape=jax.ShapeDtypeStruct((bsz, seq, D_MODEL), F32),
        grid=(bsz, seq // SEQ_T),
        in_specs=[
            pl.BlockSpec((None, SEQ_T, D_MODEL), lambda b, s: (b, s, 0)),
            pl.BlockSpec((None, SEQ_T, D_MODEL), _next_tile(seq // SEQ_T)),
            pl.BlockSpec((1, D_MODEL), row),
            pl.BlockSpec((D_MODEL, AB_COLS), row, pipeline_mode=once),
            pl.BlockSpec((n_l, GROUP_W), row),
            pl.BlockSpec((1, GROUP_W), row),
            pl.BlockSpec((CONV_W, GROUP_W), row),
            pl.BlockSpec((1, GROUP_W), row),
            pl.BlockSpec((D_MODEL, D_MODEL), row, pipeline_mode=once),
        ],
        out_specs=pl.BlockSpec((None, SEQ_T, D_MODEL), lambda b, s: (b, s, 0)),
        scratch_shapes=[pltpu.VMEM((SEQ_T, AB_COLS), F32), pltpu.VMEM((SEQ_T, AB_COLS), F32),
                        pltpu.VMEM((SEQ_T, D_MODEL), BF16),
                        pltpu.VMEM((N_HEADS, D_HEAD, D_HEAD), F32), pltpu.VMEM((SEQ_T + 8, GROUP_W), F32)],
        compiler_params=pltpu.CompilerParams(
            dimension_semantics=("arbitrary", "arbitrary"), vmem_limit_bytes=VMEM_LIMIT),
        name="mix_ab",
    )(x, x, gain.reshape(1, D_MODEL), w_in, lb_logits, hg_norm.reshape(1, GROUP_W), conv_w,
      conv_b.reshape(1, GROUP_W), w_out)


def _mix_cd_body(x_ref, xn_ref, g_ref, w_ml_ref, w_mb_ref, w_gate_ref, bias_ref, mln_ref, cos_ref, sin_ref,
                 o_ref, qo_ref, ko_ref, vt_ref, km_ref, ka_ref, pa_ref, pb_ref, c_ref, n_ref, m_ref):
    tile = functools.partial(_mix_cd_tile, bias_ref=bias_ref, mln_ref=mln_ref, cos_ref=cos_ref, sin_ref=sin_ref,
                             o_ref=o_ref, qo_ref=qo_ref, ko_ref=ko_ref, vt_ref=vt_ref, km_ref=km_ref,
                             ka_ref=ka_ref, c_ref=c_ref, n_ref=n_ref, m_ref=m_ref)
    _with_pipelined_projection(x_ref, xn_ref, g_ref, (w_ml_ref, w_mb_ref, w_gate_ref), pa_ref, pb_ref, tile)


def _mix_cd_tile(p_ref, ahead, *, bias_ref, mln_ref, cos_ref, sin_ref, o_ref, qo_ref, ko_ref, vt_ref, km_ref, ka_ref,
                 c_ref, n_ref, m_ref):
    s_idx = pl.program_id(1)
    n_c = SEQ_T // ML_CHUNK
    gi_ref = p_ref.at[:, 7 * GROUP_W:7 * GROUP_W + LANES]
    gf_ref = p_ref.at[:, 7 * GROUP_W + LANES:7 * GROUP_W + 2 * LANES]

    @pl.when(s_idx == 0)
    def _():
        c_ref[...] = jnp.zeros_like(c_ref)
        n_ref[...] = jnp.zeros_like(n_ref)
        m_ref[...] = jnp.zeros_like(m_ref)

    cos_t, sin_t = cos_ref[...], sin_ref[...]
    for h in range(N_HEADS):
        ahead.emit(1)
        lo, hi = h * D_HEAD, (h + 1) * D_HEAD
        q = _rope(p_ref[:, 4 * GROUP_W + lo:4 * GROUP_W + hi], cos_t, sin_t) * (D_HEAD ** -0.5 * LOG2E)
        k = _rope(p_ref[:, 5 * GROUP_W + lo:5 * GROUP_W + hi], cos_t, sin_t)
        qo_ref[:, lo:hi] = q.astype(BF16)
        ko_ref[h] = k.astype(BF16)
        vt_ref[h] = p_ref[:, 6 * GROUP_W + lo:6 * GROUP_W + hi].T.astype(BF16)
        km_ref[:, lo:hi] = jnp.mean(k, axis=0, keepdims=True)
        ka_ref[:, lo:hi] = jnp.max(jnp.abs(k), axis=0, keepdims=True)

    tril, same = _chunk_masks(SEQ_T, ML_CHUNK)
    tril01 = jnp.where(tril, 1.0, 0.0).astype(BF16)
    same01 = jnp.where(same, 1.0, 0.0).astype(BF16)
    emask = _expand_mask(SEQ_T, ML_CHUNK, D_HEAD)

    log_i = gi_ref[...] + bias_ref[0:1, :]
    log_f = jax.nn.log_sigmoid(gf_ref[...] + bias_ref[1:2, :])
    b_col = _dot_mask(tril01, log_f)
    e_col = _dot_mask(same01, log_f)
    w_end = e_col - b_col + log_i

    m = m_ref[...]
    m_prev_rows, m_new_rows, a_rows = [], [], []
    for c in range(n_c):
        r0 = c * ML_CHUNK
        be = e_col[r0:r0 + 1, :]
        m_end = jnp.max(w_end[r0:r0 + ML_CHUNK, :], axis=0, keepdims=True)
        m_new = jnp.maximum(be + m, m_end)
        a_rows.append(jnp.exp(be + m - m_new))
        m_prev_rows.append(jnp.broadcast_to(m, (ML_CHUNK, LANES)))
        m_new_rows.append(jnp.broadcast_to(m_new, (ML_CHUNK, LANES)))
        m = m_new
    m_ref[...] = m
    log_inter = b_col + jnp.concatenate(m_prev_rows, axis=0)
    wk_scale = jnp.exp(w_end - jnp.concatenate(m_new_rows, axis=0))
    r_rows = (b_col - log_i).T

    for h in range(N_HEADS):
        ahead.emit(3)
        lo, hi = h * D_HEAD, (h + 1) * D_HEAD
        q = p_ref[:, lo:hi] * (D_HEAD ** -0.5)
        k = p_ref[:, GROUP_W + lo:GROUP_W + hi]
        v = p_ref[:, 2 * GROUP_W + lo:2 * GROUP_W + hi]
        og = p_ref[:, 3 * GROUP_W + lo:3 * GROUP_W + hi]
        q16, k16, v16 = q.astype(BF16), k.astype(BF16), v.astype(BF16)

        d_log = jnp.where(tril, b_col[:, h:h + 1] - r_rows[h:h + 1, :], -jnp.inf)
        linter = log_inter[:, h:h + 1]
        m_t = jnp.maximum(linter, jnp.max(d_log, axis=-1, keepdims=True))
        w = jnp.exp(d_log - m_t) * _dot_nt(q16, k16)
        a_in = jnp.exp(linter - m_t)
        num = _dot(w.astype(BF16), v16)
        den = jnp.sum(w, axis=-1, keepdims=True)

        wk = wk_scale[:, h:h + 1] * k
        v_exp = jnp.where(emask, jnp.tile(v16, (1, n_c)), jnp.zeros((), BF16))
        dc_all = _dot(wk.T.astype(BF16), v_exp)

        c_mat = c_ref[h]
        n_vec = n_ref[h]
        c_prev, n_prev = [], []
        for c in range(n_c):
            r0 = c * ML_CHUNK
            c_prev.append(c_mat)
            n_prev.append(jnp.broadcast_to(n_vec, (ML_CHUNK, D_HEAD)))
            a = a_rows[c][:, h:h + 1]
            c_mat = a * c_mat + dc_all[:, c * D_HEAD:(c + 1) * D_HEAD]
            n_vec = a * n_vec + jnp.sum(wk[r0:r0 + ML_CHUNK, :], axis=0, keepdims=True)
        c_ref[h] = c_mat
        n_ref[h] = n_vec

        q_exp = jnp.where(emask, jnp.tile(q16, (1, n_c)), jnp.zeros((), BF16))
        q_c = _dot(q_exp, jnp.concatenate(c_prev, axis=0).astype(BF16))
        q_n = jnp.sum(q * jnp.concatenate(n_prev, axis=0), axis=-1, keepdims=True)
        num = num + a_in * q_c
        den = den + a_in * q_n
        hh = num / jnp.maximum(jnp.abs(den), jnp.exp(-m_t))
        o_ref[:, lo:hi] = (_rms(hh, mln_ref[:, lo:hi]) * jax.nn.sigmoid(og)).astype(BF16)


def _mix_cd(x, gain, w_ml, w_mb, w_gate, bias_rows, ml_norm, cos_t, sin_t):
    bsz, seq = x.shape[0], x.shape[1]
    n_blk, n_tile = seq // MB_BLOCK, seq // MB_TILE
    assert SEQ_T == MB_BLOCK and MB_TILE == 2 * MB_BLOCK
    row = lambda b, s: (0, 0)
    stat = pl.BlockSpec((None, None, 1, GROUP_W), lambda b, s: (b, s, 0, 0))
    return pl.pallas_call(
        _mix_cd_body,
        out_shape=[
            jax.ShapeDtypeStruct((bsz, seq, GROUP_W), BF16),
            jax.ShapeDtypeStruct((bsz, seq, GROUP_W), BF16),
            jax.ShapeDtypeStruct((bsz, N_HEADS, n_tile, MB_TILE, D_HEAD), BF16),
            jax.ShapeDtypeStruct((bsz, N_HEADS, n_tile, D_HEAD, MB_TILE), BF16),
            jax.ShapeDtypeStruct((bsz, n_blk, 1, GROUP_W), F32),
            jax.ShapeDtypeStruct((bsz, n_blk, 1, GROUP_W), F32),
        ],
        grid=(bsz, seq // SEQ_T),
        in_specs=[
            pl.BlockSpec((None, SEQ_T, D_MODEL), lambda b, s: (b, s, 0)),
            pl.BlockSpec((None, SEQ_T, D_MODEL), _next_tile(seq // SEQ_T)),
            pl.BlockSpec((1, D_MODEL), row),
            pl.BlockSpec(w_ml.shape, row, pipeline_mode=pl.Buffered(1)),
            pl.BlockSpec(w_mb.shape, row, pipeline_mode=pl.Buffered(1)),
            pl.BlockSpec(w_gate.shape, row, pipeline_mode=pl.Buffered(1)),
            pl.BlockSpec((2, LANES), row),
            pl.BlockSpec((1, GROUP_W), row),
            pl.BlockSpec((SEQ_T, LANES), lambda b, s: (s, 0)),
            pl.BlockSpec((SEQ_T, LANES), lambda b, s: (s, 0)),
        ],
        out_specs=[
            pl.BlockSpec((None, SEQ_T, GROUP_W), lambda b, s: (b, s, 0)),
            pl.BlockSpec((None, SEQ_T, GROUP_W), lambda b, s: (b, s, 0)),
            pl.BlockSpec((None, N_HEADS, None, MB_BLOCK, D_HEAD), lambda b, s: (b, 0, s // 2, s % 2, 0)),
            pl.BlockSpec((None, N_HEADS, None, D_HEAD, MB_BLOCK), lambda b, s: (b, 0, s // 2, 0, s % 2)),
            stat, stat,
        ],
        scratch_shapes=[pltpu.VMEM((SEQ_T, CD_COLS), F32), pltpu.VMEM((SEQ_T, CD_COLS), F32),
                        pltpu.VMEM((N_HEADS, D_HEAD, D_HEAD), F32),
                        pltpu.VMEM((N_HEADS, 1, D_HEAD), F32),
                        pltpu.VMEM((1, LANES), F32)],
        compiler_params=pltpu.CompilerParams(
            dimension_semantics=("arbitrary", "arbitrary"), vmem_limit_bytes=VMEM_LIMIT),
        name="mix_cd",
    )(x, x, gain.reshape(1, D_MODEL), w_ml, w_mb, w_gate, bias_rows, ml_norm.reshape(1, GROUP_W), cos_t, sin_t)


def _rope(x, cos_t, sin_t):
    lane = lax.broadcasted_iota(jnp.int32, x.shape, 1)
    half = ROPE_DIM // 2
    swapped = jnp.where(lane < half, pltpu.roll(x, LANES - half, axis=1), pltpu.roll(x, half, axis=1))
    return x * cos_t + swapped * sin_t


def _moba_body(q_ref, k_ref, vt_ref, km_ref, ka_ref, x_ref, ya_ref, wo_ref, o_ref,
               sel_ref, sd_ref, acc_ref, yb_ref, *, n_blk):
    t_own = pl.program_id(1)
    heads = [(h, h * D_HEAD, (h + 1) * D_HEAD) for h in range(N_HEADS)]
    bk = MB_BLOCK
    o_ref[...] = x_ref[...] + _dot(ya_ref[...], wo_ref[:GROUP_W, :])

    blk = lax.broadcasted_iota(jnp.int32, (n_blk, MB_TILE), 0)
    lane = lax.broadcasted_iota(jnp.int32, (n_blk, MB_TILE), 1)
    own = 2 * t_own + lane // bk
    bound_past = []
    for h, lo, hi in heads:
        q = q_ref[:, lo:hi]
        km = km_ref[:, lo:hi]
        km_hi = km.astype(BF16)
        km_lo = (km - km_hi.astype(F32)).astype(BF16)
        gate = _dot_nt(km_hi, q) + _dot_nt(km_lo, q)
        gate = jnp.where(blk < own, gate, -jnp.inf)
        sel = jnp.zeros(gate.shape, F32)
        for _ in range(MB_TOPK):
            mx = jnp.max(gate, axis=0, keepdims=True)
            idx = jnp.min(jnp.where(gate == mx, blk, n_blk), axis=0, keepdims=True)
            pick = blk == jnp.where(mx > -jnp.inf, idx, -1)
            sel = jnp.where(pick, 1.0, sel)
            gate = jnp.where(pick, -jnp.inf, gate)
        sel_ref[h] = sel
        bound = _dot_nt(ka_ref[:, lo:hi].astype(BF16), jnp.abs(q)) * (1.0 + 2.0 ** -6)
        bound_past.append(jnp.max(jnp.where(sel > 0.0, bound, -jnp.inf), axis=0, keepdims=True))

    def live_rows(h, t):
        return sel_ref[h, pl.ds(2 * t, 1), :] > 0.0, sel_ref[h, pl.ds(2 * t + 1, 1), :] > 0.0

    kpos = lax.broadcasted_iota(jnp.int32, (MB_TILE, MB_TILE), 0)
    qpos = lax.broadcasted_iota(jnp.int32, (MB_TILE, MB_TILE), 1)
    kb, qb = kpos // bk, qpos // bk
    causal = jnp.where(kb == qb, jnp.where(kpos <= qpos, 1.0, 0.0), 0.0)
    cross = jnp.where(kb < qb, 1.0, 0.0)

    def visible(h):
        return (causal + cross * sel_ref[h, pl.ds(2 * t_own, 1), :]) > 0.0

    m_diag = []
    for h, lo, hi in heads:
        s = _dot_nt(k_ref[h, t_own], q_ref[:, lo:hi])
        sd_ref[h] = s
        m_diag.append(jnp.max(jnp.where(visible(h), s, -jnp.inf), axis=0, keepdims=True))

    def exact_past_max():
        def tile_max(t, ms):
            out = []
            for h, lo, hi in heads:
                s = _dot_nt(k_ref[h, t], q_ref[:, lo:hi])
                live_a, live_b = live_rows(h, t)
                m_a = jnp.where(live_a, jnp.max(s[:bk], axis=0, keepdims=True), -jnp.inf)
                m_b = jnp.where(live_b, jnp.max(s[bk:], axis=0, keepdims=True), -jnp.inf)
                out.append(jnp.maximum(ms[h], jnp.maximum(m_a, m_b)))
            return tuple(out)

        return lax.fori_loop(0, t_own, tile_max, tuple(jnp.full((1, MB_TILE), -jnp.inf, F32) for _ in heads))

    slack = functools.reduce(jnp.maximum, [jnp.max(bound_past[h] - m_diag[h]) for h, _, _ in heads])
    m_past = lax.cond(slack > MB_BOUND_WINDOW, exact_past_max, lambda: tuple(bound_past))

    m_row, l0 = [], []
    for h, lo, hi in heads:
        m_h = jnp.maximum(m_diag[h], m_past[h])
        p = jnp.where(visible(h), jnp.exp2(sd_ref[h] - m_h), 0.0)
        m_row.append(m_h)
        l0.append(jnp.sum(p, axis=0, keepdims=True))
        acc_ref[h] = _dot(vt_ref[h, t_own], p.astype(BF16))

    def accumulate(tiles, ls):
        work = [(t, h, lo, hi) for t in tiles for h, lo, hi in heads]
        score = lambda t, h, lo, hi: _dot_nt(k_ref[h, t], q_ref[:, lo:hi])
        ls = list(ls)
        scores = score(*work[0])
        for i, (t, h, lo, hi) in enumerate(work):
            s = scores
            if i + 1 < len(work):
                scores = score(*work[i + 1])
            e = jnp.exp2(s - m_row[h])
            live_a, live_b = live_rows(h, t)
            p = jnp.concatenate([jnp.where(live_a, e[:bk], 0.0), jnp.where(live_b, e[bk:], 0.0)], axis=0)
            ls[h] = ls[h] + jnp.sum(p, axis=0, keepdims=True)
            acc_ref[h] += _dot(vt_ref[h, t], p.astype(BF16))
        return tuple(ls)

    l_f = lax.fori_loop(0, t_own // 2, lambda i, ls: accumulate([2 * i, 2 * i + 1], ls), tuple(l0))
    l_f = lax.cond(t_own % 2 == 1, lambda ls: accumulate([t_own - 1], ls), lambda ls: ls, l_f)
    for h, lo, hi in heads:
        yb_ref[:, lo:hi] = (acc_ref[h] / l_f[h]).T.astype(BF16)

    o_ref[...] += _dot(yb_ref[...], wo_ref[GROUP_W:, :])


def _moba(q, k_tiles, vt_tiles, k_mean, k_absmax, x, ya, w_out):
    bsz, seq = q.shape[0], q.shape[1]
    n_blk, n_tile = seq // MB_BLOCK, seq // MB_TILE
    once = pl.Buffered(1)
    tile = lambda b, i: (b, i, 0)
    return pl.pallas_call(
        functools.partial(_moba_body, n_blk=n_blk),
        out_shape=jax.ShapeDtypeStruct((bsz, seq, D_MODEL), F32),
        grid=(bsz, n_tile),
        in_specs=[
            pl.BlockSpec((None, MB_TILE, GROUP_W), tile),
            pl.BlockSpec((None, N_HEADS, n_tile, MB_TILE, D_HEAD), lambda b, i: (b, 0, 0, 0, 0),
                         pipeline_mode=once),
            pl.BlockSpec((None, N_HEADS, n_tile, D_HEAD, MB_TILE), lambda b, i: (b, 0, 0, 0, 0),
                         pipeline_mode=once),
            pl.BlockSpec((None, n_blk, GROUP_W), lambda b, i: (b, 0, 0)),
            pl.BlockSpec((None, n_blk, GROUP_W), lambda b, i: (b, 0, 0)),
            pl.BlockSpec((None, MB_TILE, D_MODEL), tile),
            pl.BlockSpec((None, MB_TILE, GROUP_W), tile),
            pl.BlockSpec((D_MODEL, D_MODEL), lambda b, i: (0, 0), pipeline_mode=once),
        ],
        out_specs=pl.BlockSpec((None, MB_TILE, D_MODEL), tile),
        scratch_shapes=[pltpu.VMEM((N_HEADS, n_blk, MB_TILE), F32),
                        pltpu.VMEM((N_HEADS, MB_TILE, MB_TILE), F32),
                        pltpu.VMEM((N_HEADS, D_HEAD, MB_TILE), F32),
                        pltpu.VMEM((MB_TILE, GROUP_W), BF16)],
        compiler_params=pltpu.CompilerParams(
            dimension_semantics=("parallel", "arbitrary"), vmem_limit_bytes=VMEM_LIMIT),
        name="moba",
    )(q, k_tiles, vt_tiles, k_mean, k_absmax, x, ya, w_out)


def _rope_tables(seq):
    half = ROPE_DIM // 2
    inv_freq = np.float64(ROPE_THETA) ** (-np.arange(half, dtype=np.float64) * 2.0 / ROPE_DIM)
    ang = np.arange(seq, dtype=np.float64)[:, None] * inv_freq[None, :]
    cos, sin = np.cos(ang).astype(np.float32), np.sin(ang).astype(np.float32)
    rest = LANES - ROPE_DIM
    cos_t = np.concatenate([cos, cos, np.ones((seq, rest), np.float32)], axis=1)
    sin_t = np.concatenate([-sin, sin, np.zeros((seq, rest), np.float32)], axis=1)
    return jnp.asarray(cos_t), jnp.asarray(sin_t)


def kernel(x, ffn_norm, ffn_w_in, ffn_w_out, mix_norm, ab_w_in, ab_w_out, hgrn_lb_logits, hgrn_out_norm,
           conv_w, conv_b, cd_w_in, cd_w_out, mlstm_gate_bias, mlstm_out_norm, final_norm):
    bsz, seq, d = x.shape
    depth = ffn_norm.shape[0]
    t = bsz * seq
    xt = x.reshape(t, d)
    cos_t, sin_t = _rope_tables(seq)

    ffn_order = [(layer, which) for layer in range(depth) for which in (0, 1)]
    w_in16, w_out16 = ffn_w_in[0, 0].astype(BF16), ffn_w_out[0, 0].astype(BF16)

    def run_ffn(xt, w_in16, w_out16, layer, which):
        k = ffn_order.index((layer, which))
        last = k + 1 == len(ffn_order)
        nxt = None if last else (ffn_w_in, ffn_w_out) + ffn_order[k + 1]
        out = _ffn(xt, ffn_norm[layer, which], w_in16, w_out16, final_norm, last, cast_next=nxt)
        return (out[0], None, None) if last else tuple(out)

    for layer in range(depth):
        xt, w_in16, w_out16 = run_ffn(xt, w_in16, w_out16, layer, 0)
        x3 = xt.reshape(bsz, seq, d)
        if layer % 2 == 0:
            e = layer // 2
            x3 = _mix_ab(x3, mix_norm[layer], ab_w_in[e].astype(BF16), hgrn_lb_logits, hgrn_out_norm[e],
                         conv_w[e], conv_b[e], ab_w_out[e].astype(BF16), layer)
        else:
            o = layer // 2
            w = cd_w_in[o]
            n_ml = 4 * GROUP_W
            pad = jnp.zeros((d, LANES - N_HEADS), BF16)
            w_ml = w[:, :n_ml].astype(BF16)
            w_mb = w[:, n_ml + 2 * N_HEADS:].astype(BF16)
            w_gate = jnp.concatenate([w[:, n_ml:n_ml + N_HEADS].astype(BF16), pad,
                                      w[:, n_ml + N_HEADS:n_ml + 2 * N_HEADS].astype(BF16), pad], axis=1)
            bias_rows = jnp.pad(mlstm_gate_bias[o], ((0, 0), (0, LANES - N_HEADS)))
            ya, q_r, k_tiles, vt_tiles, k_mean, k_absmax = _mix_cd(
                x3, mix_norm[layer], w_ml, w_mb, w_gate, bias_rows, mlstm_out_norm[o], cos_t, sin_t)
            n_blk = seq // MB_BLOCK
            x3 = _moba(q_r, k_tiles, vt_tiles,
                       k_mean.reshape(bsz, n_blk, GROUP_W), k_absmax.reshape(bsz, n_blk, GROUP_W),
                       x3, ya, cd_w_out[o].astype(BF16))
        xt = x3.reshape(t, d)
        xt, w_in16, w_out16 = run_ffn(xt, w_in16, w_out16, layer, 1)
    return xt.reshape(bsz, seq, d)
```

```python
import functools

import jax
import jax.numpy as jnp
import numpy as np
from jax import lax
from jax.experimental import pallas as pl
from jax.experimental.pallas import tpu as pltpu

F32 = jnp.float32
BF16 = jnp.bfloat16

D_MODEL = 1024
D_FF = 2816
GROUP_W = 512
N_HEADS = 4
D_HEAD = 128
RMS_EPS = 1e-6
HG_CHUNK = 32
ML_CHUNK = 64
MB_BLOCK = 256
MB_TOPK = 3
CONV_W = 3
ROPE_THETA = 500000.0
ROPE_DIM = D_HEAD // 4

LANES = 128
VMEM_LIMIT = 48 * 1024 * 1024

FFN_TM = 1024
FFN_SUB = 512
FFN_CH = 256
PROJ_CH = 256
SEQ_T = 256
MB_TILE = 2 * MB_BLOCK
MB_BOUND_WINDOW = 80.0
LOG2E = 1.4426950408889634

AB_COLS = 7 * GROUP_W
CD_COLS = 7 * GROUP_W + 2 * LANES


def _rms(x, gain):
    return x * lax.rsqrt(jnp.mean(x * x, axis=-1, keepdims=True) + RMS_EPS) * gain


def _dot(a, b):
    return jnp.dot(a, b, preferred_element_type=F32)


def _dot_nt(a, b):
    return lax.dot_general(a, b, (((1,), (1,)), ((), ())), preferred_element_type=F32)


def _dot_mask(mask01, x):
    hi = x.astype(BF16)
    lo = (x - hi.astype(F32)).astype(BF16)
    return _dot(mask01, hi) + _dot(mask01, lo)


def _chunk_last(x, chunk):
    rows = x.shape[0]
    return jnp.concatenate([jnp.broadcast_to(x[r0 + chunk - 1:r0 + chunk, :], (chunk, x.shape[1]))
                            for r0 in range(0, rows, chunk)], axis=0)


def _chunk_masks(n, chunk):
    r = lax.broadcasted_iota(jnp.int32, (n, n), 0)
    c = lax.broadcasted_iota(jnp.int32, (n, n), 1)
    same = (r // chunk) == (c // chunk)
    return jnp.where(same, jnp.where(c <= r, 1, 0), 0) > 0, same


def _expand_mask(n, chunk, width):
    n_c = n // chunk
    r = lax.broadcasted_iota(jnp.int32, (n, n_c * width), 0)
    c = lax.broadcasted_iota(jnp.int32, (n, n_c * width), 1)
    return (r // chunk) == (c // width)


def _split_cd_w_in(chunk, ml_ref, mb_ref, gate_ref):
    n_ml = 4 * GROUP_W
    ml_ref[...] = chunk[:, :n_ml].astype(BF16)
    mb_ref[...] = chunk[:, n_ml + 2 * N_HEADS:].astype(BF16)
    win = chunk[:, n_ml:n_ml + LANES]
    lane = lax.broadcasted_iota(jnp.int32, win.shape, 1)
    gate_ref[:, :LANES] = jnp.where(lane < N_HEADS, win, 0.0).astype(BF16)
    gate_ref[:, LANES:] = jnp.where(lane < N_HEADS, pltpu.roll(win, LANES - N_HEADS, axis=1), 0.0).astype(BF16)


def _ffn_body(*refs, final_norm, side_kinds):
    n_in = len(side_kinds)
    n_out = sum(3 if kind == "cd_w_in" else 1 for kind in side_kinds)
    x_ref, g_ref, wg_ref, wu_ref, wo_ref, fg_ref = refs[:6]
    side_in, o_ref = refs[6:6 + n_in], refs[6 + n_in]
    side_out, act_ref = list(refs[7 + n_in:7 + n_in + n_out]), refs[7 + n_in + n_out]

    for kind, src_ref in zip(side_kinds, side_in):
        if kind == "cd_w_in":
            _split_cd_w_in(src_ref[...], side_out.pop(0), side_out.pop(0), side_out.pop(0))
        else:
            side_out.pop(0)[...] = src_ref[...].astype(BF16)

    for r0 in range(0, FFN_TM, FFN_SUB):
        rows = pl.ds(r0, FFN_SUB)
        h = _rms(x_ref[rows, :], g_ref[...]).astype(BF16)
        for c0 in range(0, D_FF, FFN_CH):
            gate = _dot(h, wg_ref[:, c0:c0 + FFN_CH])
            up = _dot(h, wu_ref[:, c0:c0 + FFN_CH])
            act_ref[rows, c0:c0 + FFN_CH] = (gate * jax.nn.sigmoid(gate) * up).astype(BF16)
        y = x_ref[rows, :] + 0.5 * _dot(act_ref[rows, :], wo_ref[...])
        if final_norm:
            y = _rms(y, fg_ref[...])
        o_ref[rows, :] = y


def _ffn(x, gain, w_in16, w_out16, final_gain, final_norm, side_casts=()):
    t = x.shape[0]
    n_steps = t // FFN_TM
    once = pl.Buffered(1)
    in_specs = [
        pl.BlockSpec((FFN_TM, D_MODEL), lambda i: (i, 0)),
        pl.BlockSpec((1, D_MODEL), lambda i: (0, 0)),
        pl.BlockSpec((D_MODEL, D_FF), lambda i: (0, 0), pipeline_mode=once),
        pl.BlockSpec((D_MODEL, D_FF), lambda i: (0, 1), pipeline_mode=once),
        pl.BlockSpec((D_FF, D_MODEL), lambda i: (0, 0), pipeline_mode=once),
        pl.BlockSpec((1, D_MODEL), lambda i: (0, 0)),
    ]
    out_shape = [jax.ShapeDtypeStruct((t, D_MODEL), F32)]
    out_specs = [pl.BlockSpec((FFN_TM, D_MODEL), lambda i: (i, 0))]
    args = [x, gain.reshape(1, D_MODEL), w_in16, w_in16, w_out16, final_gain.reshape(1, D_MODEL)]
    for kind, src, lead in side_casts:
        rows, cols = src.shape[-2:]
        r = rows // n_steps
        in_specs.append(pl.BlockSpec((None,) * len(lead) + (r, cols), lambda i, lead=lead: (*lead, i, 0)))
        args.append(src)
        widths = (4 * GROUP_W, 3 * GROUP_W, 2 * LANES) if kind == "cd_w_in" else (cols,)
        for w in widths:
            out_shape.append(jax.ShapeDtypeStruct((rows, w), BF16))
            out_specs.append(pl.BlockSpec((r, w), lambda i: (i, 0)))
    return pl.pallas_call(
        functools.partial(_ffn_body, final_norm=final_norm, side_kinds=tuple(k for k, _, _ in side_casts)),
        out_shape=out_shape,
        grid=(n_steps,),
        in_specs=in_specs,
        out_specs=out_specs,
        scratch_shapes=[pltpu.VMEM((FFN_TM, D_FF), BF16)],
        compiler_params=pltpu.CompilerParams(
            dimension_semantics=("parallel",), vmem_limit_bytes=VMEM_LIMIT),
        name="ffn",
    )(*args)


class _ChunkedProjection:
    def __init__(self, x_ref, g_ref, w_refs, p_ref):
        self.h = _rms(x_ref[...], g_ref[...]).astype(BF16)
        self.p_ref = p_ref
        self.todo = [(w_ref, c0) for w_ref in w_refs for c0 in range(0, w_ref.shape[1], PROJ_CH)]
        self.done = 0

    def emit(self, n_chunks):
        end = len(self.todo) if n_chunks is None else min(len(self.todo), self.done + n_chunks)
        for i in range(self.done, end):
            w_ref, c0 = self.todo[i]
            self.p_ref[:, i * PROJ_CH:(i + 1) * PROJ_CH] = _dot(self.h, w_ref[:, c0:c0 + PROJ_CH])
        self.done = end


def _project_in(x_ref, g_ref, w_refs, p_ref):
    _ChunkedProjection(x_ref, g_ref, w_refs, p_ref).emit(None)


def _with_pipelined_projection(x_ref, xn_ref, g_ref, w_refs, pa_ref, pb_ref, tile_fn):
    s_idx = pl.program_id(1)

    @pl.when(s_idx == 0)
    def _():
        _project_in(x_ref, g_ref, w_refs, pa_ref)

    def branch(parity, p_cur, p_nxt):
        @pl.when(s_idx % 2 == parity)
        def _():
            ahead = _ChunkedProjection(xn_ref, g_ref, w_refs, p_nxt)
            tile_fn(p_cur, ahead)
            ahead.emit(None)

    branch(0, pa_ref, pb_ref)
    branch(1, pb_ref, pa_ref)


def _next_tile(n_s):
    return lambda b, s: (b, jnp.minimum(s + 1, n_s - 1), 0)


def _mix_ab_body(x_ref, xn_ref, g_ref, w_ref, lbl_ref, hgn_ref, cw_ref, cb_ref, wo_ref, o_ref,
                 pa_ref, pb_ref, y_ref, st_ref, zb_ref, *, layer):
    tile = functools.partial(_mix_ab_tile, x_ref=x_ref, lbl_ref=lbl_ref, hgn_ref=hgn_ref, cw_ref=cw_ref,
                             cb_ref=cb_ref, wo_ref=wo_ref, o_ref=o_ref, y_ref=y_ref, st_ref=st_ref,
                             zb_ref=zb_ref, layer=layer)
    _with_pipelined_projection(x_ref, xn_ref, g_ref, (w_ref,), pa_ref, pb_ref, tile)


def _mix_ab_tile(p_ref, ahead, *, x_ref, lbl_ref, hgn_ref, cw_ref, cb_ref, wo_ref, o_ref, y_ref, st_ref, zb_ref, layer):
    s_idx = pl.program_id(1)
    n_c = SEQ_T // HG_CHUNK

    @pl.when(s_idx == 0)
    def _():
        st_ref[...] = jnp.zeros_like(st_ref)
        zb_ref[0:8, :] = jnp.zeros((8, GROUP_W), F32)

    lg = lbl_ref[...]
    ex = jnp.exp(lg - jnp.max(lg, axis=0, keepdims=True))
    sm = ex / jnp.sum(ex, axis=0, keepdims=True)
    lb = jnp.sum(sm[0:layer + 1, :], axis=0, keepdims=True)

    tril, _ = _chunk_masks(SEQ_T, HG_CHUNK)
    tril01 = jnp.where(tril, 1.0, 0.0).astype(BF16)
    emask = _expand_mask(SEQ_T, HG_CHUNK, D_HEAD)

    f = lb + (1.0 - lb) * jax.nn.sigmoid(p_ref[:, GROUP_W:2 * GROUP_W])
    logf = jnp.log(f)
    b_all = _dot_mask(tril01, logf)
    e_all = _chunk_last(b_all, HG_CHUNK)
    ahead.emit(2)

    for h in range(N_HEADS):
        ahead.emit(1)
        lo, hi = h * D_HEAD, (h + 1) * D_HEAD
        q = p_ref[:, lo:hi]
        v = p_ref[:, 2 * GROUP_W + lo:2 * GROUP_W + hi]
        g = p_ref[:, 3 * GROUP_W + lo:3 * GROUP_W + hi]
        b = b_all[:, lo:hi]
        e = e_all[:, lo:hi]
        kk = 1.0 - f[:, lo:hi]
        q_dec = (q * jax.nn.sigmoid(q) * jnp.exp(b)).astype(BF16)
        k_dec = (kk * jnp.exp(-b)).astype(BF16)
        k_end = (kk * jnp.exp(e - b)).astype(BF16)
        v_t = v.T.astype(BF16)

        attn = jnp.where(tril, _dot_nt(q_dec, k_dec), 0.0)
        ahead.emit(1)
        o_t = _dot_nt(v_t, attn.astype(BF16))

        k_exp = jnp.where(emask, jnp.tile(k_end, (1, n_c)), jnp.zeros((), BF16))
        d_all = _dot(v_t, k_exp)
        ahead.emit(1)

        st = st_ref[h]
        prev = []
        for c in range(n_c):
            prev.append(st)
            decay = jnp.exp(e[c * HG_CHUNK:c * HG_CHUNK + 1, :])
            st = decay * st + d_all[:, c * D_HEAD:(c + 1) * D_HEAD]
        st_ref[h] = st
        s_prev = jnp.concatenate(prev, axis=1).astype(BF16)
        q_exp = jnp.where(emask, jnp.tile(q_dec, (1, n_c)), jnp.zeros((), BF16))
        o = (o_t + _dot_nt(s_prev, q_exp)).T

        o = _rms(o, hgn_ref[:, lo:hi]) * (g * jax.nn.sigmoid(g))
        y_ref[:, lo:hi] = o.astype(BF16)

    z = p_ref[:, 5 * GROUP_W:6 * GROUP_W] * p_ref[:, 6 * GROUP_W:7 * GROUP_W]
    zb_ref[8:SEQ_T + 8, :] = z
    z1 = zb_ref[7:SEQ_T + 7, :]
    z2 = zb_ref[6:SEQ_T + 6, :]
    y = cb_ref[...] + cw_ref[0:1, :] * z2 + cw_ref[1:2, :] * z1 + cw_ref[2:3, :] * z
    y_ref[:, GROUP_W:] = (p_ref[:, 4 * GROUP_W:5 * GROUP_W] * y).astype(BF16)
    zb_ref[0:8, :] = zb_ref[SEQ_T:SEQ_T + 8, :]

    o_ref[...] = x_ref[...] + _dot(y_ref[...], wo_ref[...])


def _mix_ab(x, gain, w_in, lb_logits, hg_norm, conv_w, conv_b, w_out, layer):
    bsz, seq = x.shape[0], x.shape[1]
    n_l = lb_logits.shape[0]
    row = lambda b, s: (0, 0)
    once = pl.Buffered(1)
    return pl.pallas_call(
        functools.partial(_mix_ab_body, layer=layer),
        out_shape=jax.ShapeDtypeStruct((bsz, seq, D_MODEL), F32),
        grid=(bsz, seq // SEQ_T),
        in_specs=[
            pl.BlockSpec((None, SEQ_T, D_MODEL), lambda b, s: (b, s, 0)),
            pl.BlockSpec((None, SEQ_T, D_MODEL), _next_tile(seq // SEQ_T)),
            pl.BlockSpec((1, D_MODEL), row),
            pl.BlockSpec((D_MODEL, AB_COLS), row, pipeline_mode=once),
            pl.BlockSpec((n_l, GROUP_W), row),
            pl.BlockSpec((1, GROUP_W), row),
            pl.BlockSpec((CONV_W, GROUP_W), row),
            pl.BlockSpec((1, GROUP_W), row),
            pl.BlockSpec((D_MODEL, D_MODEL), row, pipeline_mode=once),
        ],
        out_specs=pl.BlockSpec((None, SEQ_T, D_MODEL), lambda b, s: (b, s, 0)),
        scratch_shapes=[pltpu.VMEM((SEQ_T, AB_COLS), F32), pltpu.VMEM((SEQ_T, AB_COLS), F32),
                        pltpu.VMEM((SEQ_T, D_MODEL), BF16),
                        pltpu.VMEM((N_HEADS, D_HEAD, D_HEAD), F32), pltpu.VMEM((SEQ_T + 8, GROUP_W), F32)],
        compiler_params=pltpu.CompilerParams(
            dimension_semantics=("arbitrary", "arbitrary"), vmem_limit_bytes=VMEM_LIMIT),
        name="mix_ab",
    )(x, x, gain.reshape(1, D_MODEL), w_in, lb_logits, hg_norm.reshape(1, GROUP_W), conv_w,
      conv_b.reshape(1, GROUP_W), w_out)


def _mix_cd_body(x_ref, xn_ref, g_ref, w_ml_ref, w_mb_ref, w_gate_ref, bias_ref, mln_ref, cos_ref, sin_ref,
                 o_ref, qo_ref, ko_ref, vt_ref, km_ref, ka_ref, pa_ref, pb_ref, c_ref, n_ref, m_ref):
    tile = functools.partial(_mix_cd_tile, bias_ref=bias_ref, mln_ref=mln_ref, cos_ref=cos_ref, sin_ref=sin_ref,
                             o_ref=o_ref, qo_ref=qo_ref, ko_ref=ko_ref, vt_ref=vt_ref, km_ref=km_ref,
                             ka_ref=ka_ref, c_ref=c_ref, n_ref=n_ref, m_ref=m_ref)
    _with_pipelined_projection(x_ref, xn_ref, g_ref, (w_ml_ref, w_mb_ref, w_gate_ref), pa_ref, pb_ref, tile)


def _mix_cd_tile(p_ref, ahead, *, bias_ref, mln_ref, cos_ref, sin_ref, o_ref, qo_ref, ko_ref, vt_ref, km_ref, ka_ref,
                 c_ref, n_ref, m_ref):
    s_idx = pl.program_id(1)
    n_c = SEQ_T // ML_CHUNK
    gi_ref = p_ref.at[:, 7 * GROUP_W:7 * GROUP_W + LANES]
    gf_ref = p_ref.at[:, 7 * GROUP_W + LANES:7 * GROUP_W + 2 * LANES]

    @pl.when(s_idx == 0)
    def _():
        c_ref[...] = jnp.zeros_like(c_ref)
        n_ref[...] = jnp.zeros_like(n_ref)
        m_ref[...] = jnp.zeros_like(m_ref)

    cos_t, sin_t = cos_ref[...], sin_ref[...]
    for h in range(N_HEADS):
        ahead.emit(1)
        lo, hi = h * D_HEAD, (h + 1) * D_HEAD
        q = _rope(p_ref[:, 4 * GROUP_W + lo:4 * GROUP_W + hi], cos_t, sin_t) * (D_HEAD ** -0.5 * LOG2E)
        k = _rope(p_ref[:, 5 * GROUP_W + lo:5 * GROUP_W + hi], cos_t, sin_t)
        qo_ref[:, lo:hi] = q.astype(BF16)
        ko_ref[h] = k.astype(BF16)
        vt_ref[h] = p_ref[:, 6 * GROUP_W + lo:6 * GROUP_W + hi].T.astype(BF16)
        km_ref[:, lo:hi] = jnp.mean(k, axis=0, keepdims=True)
        ka_ref[:, lo:hi] = jnp.max(jnp.abs(k), axis=0, keepdims=True)

    tril, same = _chunk_masks(SEQ_T, ML_CHUNK)
    tril01 = jnp.where(tril, 1.0, 0.0).astype(BF16)
    same01 = jnp.where(same, 1.0, 0.0).astype(BF16)
    emask = _expand_mask(SEQ_T, ML_CHUNK, D_HEAD)

    log_i = gi_ref[...] + bias_ref[0:1, :]
    log_f = jax.nn.log_sigmoid(gf_ref[...] + bias_ref[1:2, :])
    b_col = _dot_mask(tril01, log_f)
    e_col = _dot_mask(same01, log_f)
    w_end = e_col - b_col + log_i

    m = m_ref[...]
    m_prev_rows, m_new_rows, a_rows = [], [], []
    for c in range(n_c):
        r0 = c * ML_CHUNK
        be = e_col[r0:r0 + 1, :]
        m_end = jnp.max(w_end[r0:r0 + ML_CHUNK, :], axis=0, keepdims=True)
        m_new = jnp.maximum(be + m, m_end)
        a_rows.append(jnp.exp(be + m - m_new))
        m_prev_rows.append(jnp.broadcast_to(m, (ML_CHUNK, LANES)))
        m_new_rows.append(jnp.broadcast_to(m_new, (ML_CHUNK, LANES)))
        m = m_new
    m_ref[...] = m
    log_inter = b_col + jnp.concatenate(m_prev_rows, axis=0)
    wk_scale = jnp.exp(w_end - jnp.concatenate(m_new_rows, axis=0))
    r_rows = (b_col - log_i).T

    for h in range(N_HEADS):
        ahead.emit(3)
        lo, hi = h * D_HEAD, (h + 1) * D_HEAD
        q = p_ref[:, lo:hi] * (D_HEAD ** -0.5)
        k = p_ref[:, GROUP_W + lo:GROUP_W + hi]
        v = p_ref[:, 2 * GROUP_W + lo:2 * GROUP_W + hi]
        og = p_ref[:, 3 * GROUP_W + lo:3 * GROUP_W + hi]
        q16, k16, v16 = q.astype(BF16), k.astype(BF16), v.astype(BF16)

        d_log = jnp.where(tril, b_col[:, h:h + 1] - r_rows[h:h + 1, :], -jnp.inf)
        linter = log_inter[:, h:h + 1]
        m_t = jnp.maximum(linter, jnp.max(d_log, axis=-1, keepdims=True))
        w = jnp.exp(d_log - m_t) * _dot_nt(q16, k16)
        a_in = jnp.exp(linter - m_t)
        num = _dot(w.astype(BF16), v16)
        den = jnp.sum(w, axis=-1, keepdims=True)

        wk = wk_scale[:, h:h + 1] * k
        v_exp = jnp.where(emask, jnp.tile(v16, (1, n_c)), jnp.zeros((), BF16))
        dc_all = _dot(wk.T.astype(BF16), v_exp)

        c_mat = c_ref[h]
        n_vec = n_ref[h]
        c_prev, n_prev = [], []
        for c in range(n_c):
            r0 = c * ML_CHUNK
            c_prev.append(c_mat)
            n_prev.append(jnp.broadcast_to(n_vec, (ML_CHUNK, D_HEAD)))
            a = a_rows[c][:, h:h + 1]
            c_mat = a * c_mat + dc_all[:, c * D_HEAD:(c + 1) * D_HEAD]
            n_vec = a * n_vec + jnp.sum(wk[r0:r0 + ML_CHUNK, :], axis=0, keepdims=True)
        c_ref[h] = c_mat
        n_ref[h] = n_vec

        q_exp = jnp.where(emask, jnp.tile(q16, (1, n_c)), jnp.zeros((), BF16))
        q_c = _dot(q_exp, jnp.concatenate(c_prev, axis=0).astype(BF16))
        q_n = jnp.sum(q * jnp.concatenate(n_prev, axis=0), axis=-1, keepdims=True)
        num = num + a_in * q_c
        den = den + a_in * q_n
        hh = num / jnp.maximum(jnp.abs(den), jnp.exp(-m_t))
        o_ref[:, lo:hi] = (_rms(hh, mln_ref[:, lo:hi]) * jax.nn.sigmoid(og)).astype(BF16)


def _mix_cd(x, gain, w_ml, w_mb, w_gate, bias_rows, ml_norm, cos_t, sin_t):
    bsz, seq = x.shape[0], x.shape[1]
    n_blk, n_tile = seq // MB_BLOCK, seq // MB_TILE
    assert SEQ_T == MB_BLOCK and MB_TILE == 2 * MB_BLOCK
    row = lambda b, s: (0, 0)
    stat = pl.BlockSpec((None, None, 1, GROUP_W), lambda b, s: (b, s, 0, 0))
    return pl.pallas_call(
        _mix_cd_body,
        out_shape=[
            jax.ShapeDtypeStruct((bsz, seq, GROUP_W), BF16),
            jax.ShapeDtypeStruct((bsz, seq, GROUP_W), BF16),
            jax.ShapeDtypeStruct((bsz, N_HEADS, n_tile, MB_TILE, D_HEAD), BF16),
            jax.ShapeDtypeStruct((bsz, N_HEADS, n_tile, D_HEAD, MB_TILE), BF16),
            jax.ShapeDtypeStruct((bsz, n_blk, 1, GROUP_W), F32),
            jax.ShapeDtypeStruct((bsz, n_blk, 1, GROUP_W), F32),
        ],
        grid=(bsz, seq // SEQ_T),
        in_specs=[
            pl.BlockSpec((None, SEQ_T, D_MODEL), lambda b, s: (b, s, 0)),
            pl.BlockSpec((None, SEQ_T, D_MODEL), _next_tile(seq // SEQ_T)),
            pl.BlockSpec((1, D_MODEL), row),
            pl.BlockSpec(w_ml.shape, row, pipeline_mode=pl.Buffered(1)),
            pl.BlockSpec(w_mb.shape, row, pipeline_mode=pl.Buffered(1)),
            pl.BlockSpec(w_gate.shape, row, pipeline_mode=pl.Buffered(1)),
            pl.BlockSpec((2, LANES), row),
            pl.BlockSpec((1, GROUP_W), row),
            pl.BlockSpec((SEQ_T, LANES), lambda b, s: (s, 0)),
            pl.BlockSpec((SEQ_T, LANES), lambda b, s: (s, 0)),
        ],
        out_specs=[
            pl.BlockSpec((None, SEQ_T, GROUP_W), lambda b, s: (b, s, 0)),
            pl.BlockSpec((None, SEQ_T, GROUP_W), lambda b, s: (b, s, 0)),
            pl.BlockSpec((None, N_HEADS, None, MB_BLOCK, D_HEAD), lambda b, s: (b, 0, s // 2, s % 2, 0)),
            pl.BlockSpec((None, N_HEADS, None, D_HEAD, MB_BLOCK), lambda b, s: (b, 0, s // 2, 0, s % 2)),
            stat, stat,
        ],
        scratch_shapes=[pltpu.VMEM((SEQ_T, CD_COLS), F32), pltpu.VMEM((SEQ_T, CD_COLS), F32),
                        pltpu.VMEM((N_HEADS, D_HEAD, D_HEAD), F32),
                        pltpu.VMEM((N_HEADS, 1, D_HEAD), F32),
                        pltpu.VMEM((1, LANES), F32)],
        compiler_params=pltpu.CompilerParams(
            dimension_semantics=("arbitrary", "arbitrary"), vmem_limit_bytes=VMEM_LIMIT),
        name="mix_cd",
    )(x, x, gain.reshape(1, D_MODEL), w_ml, w_mb, w_gate, bias_rows, ml_norm.reshape(1, GROUP_W), cos_t, sin_t)


def _rope(x, cos_t, sin_t):
    lane = lax.broadcasted_iota(jnp.int32, x.shape, 1)
    half = ROPE_DIM // 2
    swapped = jnp.where(lane < half, pltpu.roll(x, LANES - half, axis=1), pltpu.roll(x, half, axis=1))
    return x * cos_t + swapped * sin_t


def _moba_body(q_ref, k_ref, vt_ref, km_ref, ka_ref, x_ref, ya_ref, wo_ref, o_ref,
               sel_ref, sd_ref, acc_ref, yb_ref, *, n_blk):
    t_own = pl.program_id(1)
    heads = [(h, h * D_HEAD, (h + 1) * D_HEAD) for h in range(N_HEADS)]
    bk = MB_BLOCK
    o_ref[...] = x_ref[...] + _dot(ya_ref[...], wo_ref[:GROUP_W, :])

    blk = lax.broadcasted_iota(jnp.int32, (n_blk, MB_TILE), 0)
    lane = lax.broadcasted_iota(jnp.int32, (n_blk, MB_TILE), 1)
    own = 2 * t_own + lane // bk
    bound_past = []
    for h, lo, hi in heads:
        q = q_ref[:, lo:hi]
        km = km_ref[:, lo:hi]
        km_hi = km.astype(BF16)
        km_lo = (km - km_hi.astype(F32)).astype(BF16)
        gate = _dot_nt(km_hi, q) + _dot_nt(km_lo, q)
        gate = jnp.where(blk < own, gate, -jnp.inf)
        sel = jnp.zeros(gate.shape, F32)
        for _ in range(MB_TOPK):
            mx = jnp.max(gate, axis=0, keepdims=True)
            idx = jnp.min(jnp.where(gate == mx, blk, n_blk), axis=0, keepdims=True)
            pick = blk == jnp.where(mx > -jnp.inf, idx, -1)
            sel = jnp.where(pick, 1.0, sel)
            gate = jnp.where(pick, -jnp.inf, gate)
        sel_ref[h] = sel
        bound = _dot_nt(ka_ref[:, lo:hi].astype(BF16), jnp.abs(q)) * (1.0 + 2.0 ** -6)
        bound_past.append(jnp.max(jnp.where(sel > 0.0, bound, -jnp.inf), axis=0, keepdims=True))

    def live_rows(h, t):
        return sel_ref[h, pl.ds(2 * t, 1), :] > 0.0, sel_ref[h, pl.ds(2 * t + 1, 1), :] > 0.0

    kpos = lax.broadcasted_iota(jnp.int32, (MB_TILE, MB_TILE), 0)
    qpos = lax.broadcasted_iota(jnp.int32, (MB_TILE, MB_TILE), 1)
    kb, qb = kpos // bk, qpos // bk
    causal = jnp.where(kb == qb, jnp.where(kpos <= qpos, 1.0, 0.0), 0.0)
    cross = jnp.where(kb < qb, 1.0, 0.0)

    def visible(h):
        return (causal + cross * sel_ref[h, pl.ds(2 * t_own, 1), :]) > 0.0

    m_diag = []
    for h, lo, hi in heads:
        s = _dot_nt(k_ref[h, t_own], q_ref[:, lo:hi])
        sd_ref[h] = s
        m_diag.append(jnp.max(jnp.where(visible(h), s, -jnp.inf), axis=0, keepdims=True))

    def exact_past_max():
        def tile_max(t, ms):
            out = []
            for h, lo, hi in heads:
                s = _dot_nt(k_ref[h, t], q_ref[:, lo:hi])
                live_a, live_b = live_rows(h, t)
                m_a = jnp.where(live_a, jnp.max(s[:bk], axis=0, keepdims=True), -jnp.inf)
                m_b = jnp.where(live_b, jnp.max(s[bk:], axis=0, keepdims=True), -jnp.inf)
                out.append(jnp.maximum(ms[h], jnp.maximum(m_a, m_b)))
            return tuple(out)

        return lax.fori_loop(0, t_own, tile_max, tuple(jnp.full((1, MB_TILE), -jnp.inf, F32) for _ in heads))

    slack = functools.reduce(jnp.maximum, [jnp.max(bound_past[h] - m_diag[h]) for h, _, _ in heads])
    m_past = lax.cond(slack > MB_BOUND_WINDOW, exact_past_max, lambda: tuple(bound_past))

    m_row, l0 = [], []
    for h, lo, hi in heads:
        m_h = jnp.maximum(m_diag[h], m_past[h])
        p = jnp.where(visible(h), jnp.exp2(sd_ref[h] - m_h), 0.0)
        m_row.append(m_h)
        l0.append(jnp.sum(p, axis=0, keepdims=True))
        acc_ref[h] = _dot(vt_ref[h, t_own], p.astype(BF16))

    def accumulate(tiles, ls):
        work = [(t, h, lo, hi) for t in tiles for h, lo, hi in heads]
        score = lambda t, h, lo, hi: _dot_nt(k_ref[h, t], q_ref[:, lo:hi])
        ls = list(ls)
        scores = score(*work[0])
        for i, (t, h, lo, hi) in enumerate(work):
            s = scores
            if i + 1 < len(work):
                scores = score(*work[i + 1])
            e = jnp.exp2(s - m_row[h])
            live_a, live_b = live_rows(h, t)
            p = jnp.concatenate([jnp.where(live_a, e[:bk], 0.0), jnp.where(live_b, e[bk:], 0.0)], axis=0)
            ls[h] = ls[h] + jnp.sum(p, axis=0, keepdims=True)
            acc_ref[h] += _dot(vt_ref[h, t], p.astype(BF16))
        return tuple(ls)

    l_f = lax.fori_loop(0, t_own // 2, lambda i, ls: accumulate([2 * i, 2 * i + 1], ls), tuple(l0))
    l_f = lax.cond(t_own % 2 == 1, lambda ls: accumulate([t_own - 1], ls), lambda ls: ls, l_f)
    for h, lo, hi in heads:
        yb_ref[:, lo:hi] = (acc_ref[h] / l_f[h]).T.astype(BF16)

    o_ref[...] += _dot(yb_ref[...], wo_ref[GROUP_W:, :])


def _moba(q, k_tiles, vt_tiles, k_mean, k_absmax, x, ya, w_out):
    bsz, seq = q.shape[0], q.shape[1]
    n_blk, n_tile = seq // MB_BLOCK, seq // MB_TILE
    once = pl.Buffered(1)
    tile = lambda b, i: (b, i, 0)
    return pl.pallas_call(
        functools.partial(_moba_body, n_blk=n_blk),
        out_shape=jax.ShapeDtypeStruct((bsz, seq, D_MODEL), F32),
        grid=(bsz, n_tile),
        in_specs=[
            pl.BlockSpec((None, MB_TILE, GROUP_W), tile),
            pl.BlockSpec((None, N_HEADS, n_tile, MB_TILE, D_HEAD), lambda b, i: (b, 0, 0, 0, 0),
                         pipeline_mode=once),
            pl.BlockSpec((None, N_HEADS, n_tile, D_HEAD, MB_TILE), lambda b, i: (b, 0, 0, 0, 0),
                         pipeline_mode=once),
            pl.BlockSpec((None, n_blk, GROUP_W), lambda b, i: (b, 0, 0)),
            pl.BlockSpec((None, n_blk, GROUP_W), lambda b, i: (b, 0, 0)),
            pl.BlockSpec((None, MB_TILE, D_MODEL), tile),
            pl.BlockSpec((None, MB_TILE, GROUP_W), tile),
            pl.BlockSpec((D_MODEL, D_MODEL), lambda b, i: (0, 0), pipeline_mode=once),
        ],
        out_specs=pl.BlockSpec((None, MB_TILE, D_MODEL), tile),
        scratch_shapes=[pltpu.VMEM((N_HEADS, n_blk, MB_TILE), F32),
                        pltpu.VMEM((N_HEADS, MB_TILE, MB_TILE), F32),
                        pltpu.VMEM((N_HEADS, D_HEAD, MB_TILE), F32),
                        pltpu.VMEM((MB_TILE, GROUP_W), BF16)],
        compiler_params=pltpu.CompilerParams(
            dimension_semantics=("parallel", "arbitrary"), vmem_limit_bytes=VMEM_LIMIT),
        name="moba",
    )(q, k_tiles, vt_tiles, k_mean, k_absmax, x, ya, w_out)


def _rope_tables(seq):
    half = ROPE_DIM // 2
    inv_freq = np.float64(ROPE_THETA) ** (-np.arange(half, dtype=np.float64) * 2.0 / ROPE_DIM)
    ang = np.arange(seq, dtype=np.float64)[:, None] * inv_freq[None, :]
    cos, sin = np.cos(ang).astype(np.float32), np.sin(ang).astype(np.float32)
    rest = LANES - ROPE_DIM
    cos_t = np.concatenate([cos, cos, np.ones((seq, rest), np.float32)], axis=1)
    sin_t = np.concatenate([-sin, sin, np.zeros((seq, rest), np.float32)], axis=1)
    return jnp.asarray(cos_t), jnp.asarray(sin_t)


def kernel(x, ffn_norm, ffn_w_in, ffn_w_out, mix_norm, ab_w_in, ab_w_out, hgrn_lb_logits, hgrn_out_norm,
           conv_w, conv_b, cd_w_in, cd_w_out, mlstm_gate_bias, mlstm_out_norm, final_norm):
    bsz, seq, d = x.shape
    depth = ffn_norm.shape[0]
    t = bsz * seq
    xt = x.reshape(t, d)
    cos_t, sin_t = _rope_tables(seq)

    ffn_order = [(layer, which) for layer in range(depth) for which in (0, 1)]
    ready = {("ffn", 0, 0): (ffn_w_in[0, 0].astype(BF16), ffn_w_out[0, 0].astype(BF16))}

    def run_ffn(xt, layer, which):
        k = ffn_order.index((layer, which))
        last = k + 1 == len(ffn_order)
        jobs, keys = [], []
        if not last:
            nxt = ffn_order[k + 1]
            jobs += [("plain", ffn_w_in, nxt), ("plain", ffn_w_out, nxt)]
            keys.append((("ffn",) + nxt, 2))
        if which == 0 and layer % 2 == 0:
            jobs += [("plain", ab_w_in, (layer // 2,)), ("plain", ab_w_out, (layer // 2,))]
            keys.append((("ab", layer), 2))
        if which == 0 and layer % 2 == 1:
            jobs += [("cd_w_in", cd_w_in, (layer // 2,)), ("plain", cd_w_out, (layer // 2,))]
            keys.append((("cd", layer), 4))
        w_in16, w_out16 = ready.pop(("ffn", layer, which))
        out = _ffn(xt, ffn_norm[layer, which], w_in16, w_out16, final_norm, last, side_casts=jobs)
        extra = list(out[1:])
        for key, n in keys:
            ready[key], extra = tuple(extra[:n]), extra[n:]
        return out[0]

    for layer in range(depth):
        xt = run_ffn(xt, layer, 0)
        x3 = xt.reshape(bsz, seq, d)
        if layer % 2 == 0:
            e = layer // 2
            w_in16, w_out16 = ready.pop(("ab", layer))
            x3 = _mix_ab(x3, mix_norm[layer], w_in16, hgrn_lb_logits, hgrn_out_norm[e], conv_w[e], conv_b[e],
                         w_out16, layer)
        else:
            o = layer // 2
            w_ml, w_mb, w_gate, w_out16 = ready.pop(("cd", layer))
            bias_rows = jnp.pad(mlstm_gate_bias[o], ((0, 0), (0, LANES - N_HEADS)))
            ya, q_r, k_tiles, vt_tiles, k_mean, k_absmax = _mix_cd(
                x3, mix_norm[layer], w_ml, w_mb, w_gate, bias_rows, mlstm_out_norm[o], cos_t, sin_t)
            n_blk = seq // MB_BLOCK
            x3 = _moba(q_r, k_tiles, vt_tiles,
                       k_mean.reshape(bsz, n_blk, GROUP_W), k_absmax.reshape(bsz, n_blk, GROUP_W),
                       x3, ya, w_out16)
        xt = run_ffn(x3.reshape(t, d), layer, 1)
    return xt.reshape(bsz, seq, d)
```

```python
import functools

import jax
import jax.numpy as jnp
import numpy as np
from jax import lax
from jax.experimental import pallas as pl
from jax.experimental.pallas import tpu as pltpu

F32 = jnp.float32
BF16 = jnp.bfloat16

D_MODEL = 1024
D_FF = 2816
GROUP_W = 512
N_HEADS = 4
D_HEAD = 128
RMS_EPS = 1e-6
HG_CHUNK = 32
ML_CHUNK = 64
MB_BLOCK = 256
MB_TOPK = 3
CONV_W = 3
ROPE_THETA = 500000.0
ROPE_DIM = D_HEAD // 4

LANES = 128
VMEM_LIMIT = 48 * 1024 * 1024

FFN_TM = 1024
FFN_SUB = 256
FFN_CH = 256
PROJ_CH = 256
SEQ_T = 256
SEQ_STEP = 4 * SEQ_T
MB_TILE = 2 * MB_BLOCK
MB_BOUND_WINDOW = 80.0
LOG2E = 1.4426950408889634

AB_COLS = 7 * GROUP_W
CD_COLS = 7 * GROUP_W + 2 * LANES


def _rms(x, gain):
    return x * lax.rsqrt(jnp.mean(x * x, axis=-1, keepdims=True) + RMS_EPS) * gain


def _dot(a, b):
    return jnp.dot(a, b, preferred_element_type=F32)


def _dot_nt(a, b):
    return lax.dot_general(a, b, (((1,), (1,)), ((), ())), preferred_element_type=F32)


def _dot_mask(mask01, x):
    hi = x.astype(BF16)
    lo = (x - hi.astype(F32)).astype(BF16)
    return _dot(mask01, hi) + _dot(mask01, lo)


def _chunk_last(x, chunk):
    rows = x.shape[0]
    return jnp.concatenate([jnp.broadcast_to(x[r0 + chunk - 1:r0 + chunk, :], (chunk, x.shape[1]))
                            for r0 in range(0, rows, chunk)], axis=0)


def _chunk_masks(n, chunk):
    r = lax.broadcasted_iota(jnp.int32, (n, n), 0)
    c = lax.broadcasted_iota(jnp.int32, (n, n), 1)
    same = (r // chunk) == (c // chunk)
    return jnp.where(same, jnp.where(c <= r, 1, 0), 0) > 0, same


def _expand_mask(n, chunk, width):
    n_c = n // chunk
    r = lax.broadcasted_iota(jnp.int32, (n, n_c * width), 0)
    c = lax.broadcasted_iota(jnp.int32, (n, n_c * width), 1)
    return (r // chunk) == (c // width)


def _split_cd_w_in(chunk, ml_ref, mb_ref, gate_ref):
    n_ml = 4 * GROUP_W
    ml_ref[...] = chunk[:, :n_ml].astype(BF16)
    mb_ref[...] = chunk[:, n_ml + 2 * N_HEADS:].astype(BF16)
    win = chunk[:, n_ml:n_ml + LANES]
    lane = lax.broadcasted_iota(jnp.int32, win.shape, 1)
    gate_ref[:, :LANES] = jnp.where(lane < N_HEADS, win, 0.0).astype(BF16)
    gate_ref[:, LANES:] = jnp.where(lane < N_HEADS, pltpu.roll(win, LANES - N_HEADS, axis=1), 0.0).astype(BF16)


def _ffn_body(*refs, final_norm, side_kinds):
    n_in = len(side_kinds)
    n_out = sum(3 if kind == "cd_w_in" else 1 for kind in side_kinds)
    x_ref, g_ref, wg_ref, wu_ref, wo_ref, fg_ref = refs[:6]
    side_in, o_ref = refs[6:6 + n_in], refs[6 + n_in]
    side_out, act_ref = list(refs[7 + n_in:7 + n_in + n_out]), refs[7 + n_in + n_out]

    for kind, src_ref in zip(side_kinds, side_in):
        if kind == "cd_w_in":
            _split_cd_w_in(src_ref[...], side_out.pop(0), side_out.pop(0), side_out.pop(0))
        else:
            side_out.pop(0)[...] = src_ref[...].astype(BF16)

    for r0 in range(0, FFN_TM, FFN_SUB):
        rows = pl.ds(r0, FFN_SUB)
        h = _rms(x_ref[rows, :], g_ref[...]).astype(BF16)
        for c0 in range(0, D_FF, FFN_CH):
            gate = _dot(h, wg_ref[:, c0:c0 + FFN_CH])
            up = _dot(h, wu_ref[:, c0:c0 + FFN_CH])
            act_ref[rows, c0:c0 + FFN_CH] = (gate * jax.nn.sigmoid(gate) * up).astype(BF16)
        y = x_ref[rows, :] + 0.5 * _dot(act_ref[rows, :], wo_ref[...])
        if final_norm:
            y = _rms(y, fg_ref[...])
        o_ref[rows, :] = y


def _ffn(x, gain, w_in16, w_out16, final_gain, final_norm, side_casts=()):
    t = x.shape[0]
    n_steps = t // FFN_TM
    once = pl.Buffered(1)
    in_specs = [
        pl.BlockSpec((FFN_TM, D_MODEL), lambda i: (i, 0)),
        pl.BlockSpec((1, D_MODEL), lambda i: (0, 0)),
        pl.BlockSpec((D_MODEL, D_FF), lambda i: (0, 0), pipeline_mode=once),
        pl.BlockSpec((D_MODEL, D_FF), lambda i: (0, 1), pipeline_mode=once),
        pl.BlockSpec((D_FF, D_MODEL), lambda i: (0, 0), pipeline_mode=once),
        pl.BlockSpec((1, D_MODEL), lambda i: (0, 0)),
    ]
    out_shape = [jax.ShapeDtypeStruct((t, D_MODEL), F32)]
    out_specs = [pl.BlockSpec((FFN_TM, D_MODEL), lambda i: (i, 0))]
    args = [x, gain.reshape(1, D_MODEL), w_in16, w_in16, w_out16, final_gain.reshape(1, D_MODEL)]
    for kind, src, lead in side_casts:
        rows, cols = src.shape[-2:]
        r = rows // n_steps
        in_specs.append(pl.BlockSpec((None,) * len(lead) + (r, cols), lambda i, lead=lead: (*lead, i, 0)))
        args.append(src)
        widths = (4 * GROUP_W, 3 * GROUP_W, 2 * LANES) if kind == "cd_w_in" else (cols,)
        for w in widths:
            out_shape.append(jax.ShapeDtypeStruct((rows, w), BF16))
            out_specs.append(pl.BlockSpec((r, w), lambda i: (i, 0)))
    return pl.pallas_call(
        functools.partial(_ffn_body, final_norm=final_norm, side_kinds=tuple(k for k, _, _ in side_casts)),
        out_shape=out_shape,
        grid=(n_steps,),
        in_specs=in_specs,
        out_specs=out_specs,
        scratch_shapes=[pltpu.VMEM((FFN_TM, D_FF), BF16)],
        compiler_params=pltpu.CompilerParams(
            dimension_semantics=("parallel",), vmem_limit_bytes=VMEM_LIMIT),
        name="ffn",
    )(*args)


class _ChunkedProjection:
    def __init__(self, x_ref, g_ref, w_refs, p_ref):
        self.h = _rms(x_ref[...], g_ref[...]).astype(BF16)
        self.p_ref = p_ref
        self.todo = [(w_ref, c0) for w_ref in w_refs for c0 in range(0, w_ref.shape[1], PROJ_CH)]
        self.done = 0

    def emit(self, n_chunks):
        end = len(self.todo) if n_chunks is None else min(len(self.todo), self.done + n_chunks)
        for i in range(self.done, end):
            w_ref, c0 = self.todo[i]
            self.p_ref[:, i * PROJ_CH:(i + 1) * PROJ_CH] = _dot(self.h, w_ref[:, c0:c0 + PROJ_CH])
        self.done = end


def _project_in(x_ref, g_ref, w_refs, p_ref):
    _ChunkedProjection(x_ref, g_ref, w_refs, p_ref).emit(None)


def _tiles_with_lookahead(x_ref, xn_ref, g_ref, w_refs, pa_ref, pb_ref, tile_fn):
    n_tiles = x_ref.shape[0] // SEQ_T
    assert n_tiles % 2 == 0
    rows = lambda k: pl.ds(k * SEQ_T, SEQ_T)
    bufs = (pa_ref, pb_ref)

    @pl.when(pl.program_id(1) == 0)
    def _():
        _project_in(x_ref.at[rows(0)], g_ref, w_refs, pa_ref)

    for k in range(n_tiles):
        nxt = x_ref.at[rows(k + 1)] if k + 1 < n_tiles else xn_ref.at[rows(0)]
        ahead = _ChunkedProjection(nxt, g_ref, w_refs, bufs[(k + 1) % 2])
        tile_fn(k, bufs[k % 2], ahead)
        ahead.emit(None)


def _one_tile_with_lookahead(x_ref, xn_ref, g_ref, w_refs, pa_ref, pb_ref, tile_fn):
    s_idx = pl.program_id(1)

    @pl.when(s_idx == 0)
    def _():
        _project_in(x_ref, g_ref, w_refs, pa_ref)

    def branch(parity, p_cur, p_nxt):
        @pl.when(s_idx % 2 == parity)
        def _():
            ahead = _ChunkedProjection(xn_ref, g_ref, w_refs, p_nxt)
            tile_fn(p_cur, ahead)
            ahead.emit(None)

    branch(0, pa_ref, pb_ref)
    branch(1, pb_ref, pa_ref)


def _next_tile(n_s):
    return lambda b, s: (b, jnp.minimum(s + 1, n_s - 1), 0)


def _mix_ab_body(x_ref, xn_ref, g_ref, w_ref, lbl_ref, hgn_ref, cw_ref, cb_ref, wo_ref, o_ref,
                 pa_ref, pb_ref, y_ref, st_ref, zb_ref, *, layer):
    def tile(half, p_ref, ahead):
        rows = pl.ds(half * SEQ_T, SEQ_T)
        _mix_ab_tile(p_ref, ahead, x_ref=x_ref.at[rows], lbl_ref=lbl_ref, hgn_ref=hgn_ref, cw_ref=cw_ref,
                     cb_ref=cb_ref, wo_ref=wo_ref, o_ref=o_ref.at[rows], y_ref=y_ref, st_ref=st_ref,
                     zb_ref=zb_ref, layer=layer, starts_sequence=half == 0)

    _tiles_with_lookahead(x_ref, xn_ref, g_ref, (w_ref,), pa_ref, pb_ref, tile)


def _mix_ab_tile(p_ref, ahead, *, x_ref, lbl_ref, hgn_ref, cw_ref, cb_ref, wo_ref, o_ref, y_ref, st_ref, zb_ref,
                 layer, starts_sequence):
    n_c = SEQ_T // HG_CHUNK

    if starts_sequence:
        @pl.when(pl.program_id(1) == 0)
        def _():
            st_ref[...] = jnp.zeros_like(st_ref)
            zb_ref[0:8, :] = jnp.zeros((8, GROUP_W), F32)

    lg = lbl_ref[...]
    ex = jnp.exp(lg - jnp.max(lg, axis=0, keepdims=True))
    sm = ex / jnp.sum(ex, axis=0, keepdims=True)
    lb = jnp.sum(sm[0:layer + 1, :], axis=0, keepdims=True)

    tril, _ = _chunk_masks(SEQ_T, HG_CHUNK)
    tril01 = jnp.where(tril, 1.0, 0.0).astype(BF16)
    emask = _expand_mask(SEQ_T, HG_CHUNK, D_HEAD)

    f = lb + (1.0 - lb) * jax.nn.sigmoid(p_ref[:, GROUP_W:2 * GROUP_W])
    logf = jnp.log(f)
    b_all = _dot_mask(tril01, logf)
    e_all = _chunk_last(b_all, HG_CHUNK)
    ahead.emit(2)

    for h in range(N_HEADS):
        ahead.emit(1)
        lo, hi = h * D_HEAD, (h + 1) * D_HEAD
        q = p_ref[:, lo:hi]
        v = p_ref[:, 2 * GROUP_W + lo:2 * GROUP_W + hi]
        g = p_ref[:, 3 * GROUP_W + lo:3 * GROUP_W + hi]
        b = b_all[:, lo:hi]
        e = e_all[:, lo:hi]
        kk = 1.0 - f[:, lo:hi]
        q_dec = (q * jax.nn.sigmoid(q) * jnp.exp(b)).astype(BF16)
        k_dec = (kk * jnp.exp(-b)).astype(BF16)
        k_end = (kk * jnp.exp(e - b)).astype(BF16)
        v_t = v.T.astype(BF16)

        attn = jnp.where(tril, _dot_nt(q_dec, k_dec), 0.0)
        ahead.emit(1)
        o_t = _dot_nt(v_t, attn.astype(BF16))

        k_exp = jnp.where(emask, jnp.tile(k_end, (1, n_c)), jnp.zeros((), BF16))
        d_all = _dot(v_t, k_exp)
        ahead.emit(1)

        st = st_ref[h]
        prev = []
        for c in range(n_c):
            prev.append(st)
            decay = jnp.exp(e[c * HG_CHUNK:c * HG_CHUNK + 1, :])
            st = decay * st + d_all[:, c * D_HEAD:(c + 1) * D_HEAD]
        st_ref[h] = st
        s_prev = jnp.concatenate(prev, axis=1).astype(BF16)
        q_exp = jnp.where(emask, jnp.tile(q_dec, (1, n_c)), jnp.zeros((), BF16))
        o = (o_t + _dot_nt(s_prev, q_exp)).T

        o = _rms(o, hgn_ref[:, lo:hi]) * (g * jax.nn.sigmoid(g))
        y_ref[:, lo:hi] = o.astype(BF16)

    z = p_ref[:, 5 * GROUP_W:6 * GROUP_W] * p_ref[:, 6 * GROUP_W:7 * GROUP_W]
    zb_ref[8:SEQ_T + 8, :] = z
    z1 = zb_ref[7:SEQ_T + 7, :]
    z2 = zb_ref[6:SEQ_T + 6, :]
    y = cb_ref[...] + cw_ref[0:1, :] * z2 + cw_ref[1:2, :] * z1 + cw_ref[2:3, :] * z
    y_ref[:, GROUP_W:] = (p_ref[:, 4 * GROUP_W:5 * GROUP_W] * y).astype(BF16)
    zb_ref[0:8, :] = zb_ref[SEQ_T:SEQ_T + 8, :]

    o_ref[...] = x_ref[...] + _dot(y_ref[...], wo_ref[...])


def _mix_ab(x, gain, w_in, lb_logits, hg_norm, conv_w, conv_b, w_out, layer):
    bsz, seq = x.shape[0], x.shape[1]
    n_l = lb_logits.shape[0]
    row = lambda b, s: (0, 0)
    once = pl.Buffered(1)
    return pl.pallas_call(
        functools.partial(_mix_ab_body, layer=layer),
        out_shape=jax.ShapeDtypeStruct((bsz, seq, D_MODEL), F32),
        grid=(bsz, seq // SEQ_STEP),
        in_specs=[
            pl.BlockSpec((None, SEQ_STEP, D_MODEL), lambda b, s: (b, s, 0)),
            pl.BlockSpec((None, SEQ_STEP, D_MODEL), _next_tile(seq // SEQ_STEP)),
            pl.BlockSpec((1, D_MODEL), row),
            pl.BlockSpec((D_MODEL, AB_COLS), row, pipeline_mode=once),
            pl.BlockSpec((n_l, GROUP_W), row),
            pl.BlockSpec((1, GROUP_W), row),
            pl.BlockSpec((CONV_W, GROUP_W), row),
            pl.BlockSpec((1, GROUP_W), row),
            pl.BlockSpec((D_MODEL, D_MODEL), row, pipeline_mode=once),
        ],
        out_specs=pl.BlockSpec((None, SEQ_STEP, D_MODEL), lambda b, s: (b, s, 0)),
        scratch_shapes=[pltpu.VMEM((SEQ_T, AB_COLS), F32), pltpu.VMEM((SEQ_T, AB_COLS), F32),
                        pltpu.VMEM((SEQ_T, D_MODEL), BF16),
                        pltpu.VMEM((N_HEADS, D_HEAD, D_HEAD), F32), pltpu.VMEM((SEQ_T + 8, GROUP_W), F32)],
        compiler_params=pltpu.CompilerParams(
            dimension_semantics=("arbitrary", "arbitrary"), vmem_limit_bytes=VMEM_LIMIT),
        name="mix_ab",
    )(x, x, gain.reshape(1, D_MODEL), w_in, lb_logits, hg_norm.reshape(1, GROUP_W), conv_w,
      conv_b.reshape(1, GROUP_W), w_out)


def _mix_cd_body(x_ref, xn_ref, g_ref, w_ml_ref, w_mb_ref, w_gate_ref, bias_ref, mln_ref, cos_ref, sin_ref,
                 o_ref, qo_ref, ko_ref, vt_ref, km_ref, ka_ref, pa_ref, pb_ref, c_ref, n_ref, m_ref):
    tile = functools.partial(_mix_cd_tile, bias_ref=bias_ref, mln_ref=mln_ref, cos_ref=cos_ref, sin_ref=sin_ref,
                             o_ref=o_ref, qo_ref=qo_ref, ko_ref=ko_ref, vt_ref=vt_ref, km_ref=km_ref,
                             ka_ref=ka_ref, c_ref=c_ref, n_ref=n_ref, m_ref=m_ref, starts_sequence=True)
    _one_tile_with_lookahead(x_ref, xn_ref, g_ref, (w_ml_ref, w_mb_ref, w_gate_ref), pa_ref, pb_ref, tile)


def _mix_cd_tile(p_ref, ahead, *, bias_ref, mln_ref, cos_ref, sin_ref, o_ref, qo_ref, ko_ref, vt_ref, km_ref, ka_ref,
                 c_ref, n_ref, m_ref, starts_sequence):
    n_c = SEQ_T // ML_CHUNK
    gi_ref = p_ref.at[:, 7 * GROUP_W:7 * GROUP_W + LANES]
    gf_ref = p_ref.at[:, 7 * GROUP_W + LANES:7 * GROUP_W + 2 * LANES]

    if starts_sequence:
        @pl.when(pl.program_id(1) == 0)
        def _():
            c_ref[...] = jnp.zeros_like(c_ref)
            n_ref[...] = jnp.zeros_like(n_ref)
            m_ref[...] = jnp.zeros_like(m_ref)

    cos_t, sin_t = cos_ref[...], sin_ref[...]
    for h in range(N_HEADS):
        ahead.emit(1)
        lo, hi = h * D_HEAD, (h + 1) * D_HEAD
        q = _rope(p_ref[:, 4 * GROUP_W + lo:4 * GROUP_W + hi], cos_t, sin_t) * (D_HEAD ** -0.5 * LOG2E)
        k = _rope(p_ref[:, 5 * GROUP_W + lo:5 * GROUP_W + hi], cos_t, sin_t)
        qo_ref[:, lo:hi] = q.astype(BF16)
        ko_ref[h] = k.astype(BF16)
        vt_ref[h] = p_ref[:, 6 * GROUP_W + lo:6 * GROUP_W + hi].T.astype(BF16)
        km_ref[:, lo:hi] = jnp.mean(k, axis=0, keepdims=True)
        ka_ref[:, lo:hi] = jnp.max(jnp.abs(k), axis=0, keepdims=True)

    tril, same = _chunk_masks(SEQ_T, ML_CHUNK)
    tril01 = jnp.where(tril, 1.0, 0.0).astype(BF16)
    same01 = jnp.where(same, 1.0, 0.0).astype(BF16)
    emask = _expand_mask(SEQ_T, ML_CHUNK, D_HEAD)

    log_i = gi_ref[...] + bias_ref[0:1, :]
    log_f = jax.nn.log_sigmoid(gf_ref[...] + bias_ref[1:2, :])
    b_col = _dot_mask(tril01, log_f)
    e_col = _dot_mask(same01, log_f)
    w_end = e_col - b_col + log_i

    m = m_ref[...]
    m_prev_rows, m_new_rows, a_rows = [], [], []
    for c in range(n_c):
        r0 = c * ML_CHUNK
        be = e_col[r0:r0 + 1, :]
        m_end = jnp.max(w_end[r0:r0 + ML_CHUNK, :], axis=0, keepdims=True)
        m_new = jnp.maximum(be + m, m_end)
        a_rows.append(jnp.exp(be + m - m_new))
        m_prev_rows.append(jnp.broadcast_to(m, (ML_CHUNK, LANES)))
        m_new_rows.append(jnp.broadcast_to(m_new, (ML_CHUNK, LANES)))
        m = m_new
    m_ref[...] = m
    log_inter = b_col + jnp.concatenate(m_prev_rows, axis=0)
    wk_scale = jnp.exp(w_end - jnp.concatenate(m_new_rows, axis=0))
    r_rows = (b_col - log_i).T

    for h in range(N_HEADS):
        ahead.emit(3)
        lo, hi = h * D_HEAD, (h + 1) * D_HEAD
        q = p_ref[:, lo:hi] * (D_HEAD ** -0.5)
        k = p_ref[:, GROUP_W + lo:GROUP_W + hi]
        v = p_ref[:, 2 * GROUP_W + lo:2 * GROUP_W + hi]
        og = p_ref[:, 3 * GROUP_W + lo:3 * GROUP_W + hi]
        q16, k16, v16 = q.astype(BF16), k.astype(BF16), v.astype(BF16)

        d_log = jnp.where(tril, b_col[:, h:h + 1] - r_rows[h:h + 1, :], -jnp.inf)
        linter = log_inter[:, h:h + 1]
        m_t = jnp.maximum(linter, jnp.max(d_log, axis=-1, keepdims=True))
        w = jnp.exp(d_log - m_t) * _dot_nt(q16, k16)
        a_in = jnp.exp(linter - m_t)
        num = _dot(w.astype(BF16), v16)
        den = jnp.sum(w, axis=-1, keepdims=True)

        wk = wk_scale[:, h:h + 1] * k
        v_exp = jnp.where(emask, jnp.tile(v16, (1, n_c)), jnp.zeros((), BF16))
        dc_all = _dot(wk.T.astype(BF16), v_exp)

        c_mat = c_ref[h]
        n_vec = n_ref[h]
        c_prev, n_prev = [], []
        for c in range(n_c):
            r0 = c * ML_CHUNK
            c_prev.append(c_mat)
            n_prev.append(jnp.broadcast_to(n_vec, (ML_CHUNK, D_HEAD)))
            a = a_rows[c][:, h:h + 1]
            c_mat = a * c_mat + dc_all[:, c * D_HEAD:(c + 1) * D_HEAD]
            n_vec = a * n_vec + jnp.sum(wk[r0:r0 + ML_CHUNK, :], axis=0, keepdims=True)
        c_ref[h] = c_mat
        n_ref[h] = n_vec

        q_exp = jnp.where(emask, jnp.tile(q16, (1, n_c)), jnp.zeros((), BF16))
        q_c = _dot(q_exp, jnp.concatenate(c_prev, axis=0).astype(BF16))
        q_n = jnp.sum(q * jnp.concatenate(n_prev, axis=0), axis=-1, keepdims=True)
        num = num + a_in * q_c
        den = den + a_in * q_n
        hh = num / jnp.maximum(jnp.abs(den), jnp.exp(-m_t))
        o_ref[:, lo:hi] = (_rms(hh, mln_ref[:, lo:hi]) * jax.nn.sigmoid(og)).astype(BF16)


def _mix_cd(x, gain, w_ml, w_mb, w_gate, bias_rows, ml_norm, cos_t, sin_t):
    bsz, seq = x.shape[0], x.shape[1]
    n_blk, n_tile = seq // MB_BLOCK, seq // MB_TILE
    assert SEQ_T == MB_BLOCK and MB_TILE == 2 * MB_BLOCK
    row = lambda b, s: (0, 0)
    stat = pl.BlockSpec((None, None, 1, GROUP_W), lambda b, s: (b, s, 0, 0))
    return pl.pallas_call(
        _mix_cd_body,
        out_shape=[
            jax.ShapeDtypeStruct((bsz, seq, GROUP_W), BF16),
            jax.ShapeDtypeStruct((bsz, seq, GROUP_W), BF16),
            jax.ShapeDtypeStruct((bsz, N_HEADS, n_tile, MB_TILE, D_HEAD), BF16),
            jax.ShapeDtypeStruct((bsz, N_HEADS, n_tile, D_HEAD, MB_TILE), BF16),
            jax.ShapeDtypeStruct((bsz, n_blk, 1, GROUP_W), F32),
            jax.ShapeDtypeStruct((bsz, n_blk, 1, GROUP_W), F32),
        ],
        grid=(bsz, seq // SEQ_T),
        in_specs=[
            pl.BlockSpec((None, SEQ_T, D_MODEL), lambda b, s: (b, s, 0)),
            pl.BlockSpec((None, SEQ_T, D_MODEL), _next_tile(seq // SEQ_T)),
            pl.BlockSpec((1, D_MODEL), row),
            pl.BlockSpec(w_ml.shape, row, pipeline_mode=pl.Buffered(1)),
            pl.BlockSpec(w_mb.shape, row, pipeline_mode=pl.Buffered(1)),
            pl.BlockSpec(w_gate.shape, row, pipeline_mode=pl.Buffered(1)),
            pl.BlockSpec((2, LANES), row),
            pl.BlockSpec((1, GROUP_W), row),
            pl.BlockSpec((SEQ_T, LANES), lambda b, s: (s, 0)),
            pl.BlockSpec((SEQ_T, LANES), lambda b, s: (s, 0)),
        ],
        out_specs=[
            pl.BlockSpec((None, SEQ_T, GROUP_W), lambda b, s: (b, s, 0)),
            pl.BlockSpec((None, SEQ_T, GROUP_W), lambda b, s: (b, s, 0)),
            pl.BlockSpec((None, N_HEADS, None, MB_BLOCK, D_HEAD), lambda b, s: (b, 0, s // 2, s % 2, 0)),
            pl.BlockSpec((None, N_HEADS, None, D_HEAD, MB_BLOCK), lambda b, s: (b, 0, s // 2, 0, s % 2)),
            stat, stat,
        ],
        scratch_shapes=[pltpu.VMEM((SEQ_T, CD_COLS), F32), pltpu.VMEM((SEQ_T, CD_COLS), F32),
                        pltpu.VMEM((N_HEADS, D_HEAD, D_HEAD), F32),
                        pltpu.VMEM((N_HEADS, 1, D_HEAD), F32),
                        pltpu.VMEM((1, LANES), F32)],
        compiler_params=pltpu.CompilerParams(
            dimension_semantics=("arbitrary", "arbitrary"), vmem_limit_bytes=VMEM_LIMIT),
        name="mix_cd",
    )(x, x, gain.reshape(1, D_MODEL), w_ml, w_mb, w_gate, bias_rows, ml_norm.reshape(1, GROUP_W), cos_t, sin_t)


def _rope(x, cos_t, sin_t):
    lane = lax.broadcasted_iota(jnp.int32, x.shape, 1)
    half = ROPE_DIM // 2
    swapped = jnp.where(lane < half, pltpu.roll(x, LANES - half, axis=1), pltpu.roll(x, half, axis=1))
    return x * cos_t + swapped * sin_t


def _moba_body(q_ref, k_ref, vt_ref, km_ref, ka_ref, x_ref, ya_ref, wo_ref, o_ref,
               sel_ref, sd_ref, acc_ref, yb_ref, *, n_blk):
    t_own = pl.program_id(1)
    heads = [(h, h * D_HEAD, (h + 1) * D_HEAD) for h in range(N_HEADS)]
    bk = MB_BLOCK
    o_ref[...] = x_ref[...] + _dot(ya_ref[...], wo_ref[:GROUP_W, :])

    blk = lax.broadcasted_iota(jnp.int32, (n_blk, MB_TILE), 0)
    lane = lax.broadcasted_iota(jnp.int32, (n_blk, MB_TILE), 1)
    own = 2 * t_own + lane // bk
    bound_past = []
    for h, lo, hi in heads:
        q = q_ref[:, lo:hi]
        km = km_ref[:, lo:hi]
        km_hi = km.astype(BF16)
        km_lo = (km - km_hi.astype(F32)).astype(BF16)
        gate = _dot_nt(km_hi, q) + _dot_nt(km_lo, q)
        gate = jnp.where(blk < own, gate, -jnp.inf)
        sel = jnp.zeros(gate.shape, F32)
        for _ in range(MB_TOPK):
            mx = jnp.max(gate, axis=0, keepdims=True)
            idx = jnp.min(jnp.where(gate == mx, blk, n_blk), axis=0, keepdims=True)
            pick = blk == jnp.where(mx > -jnp.inf, idx, -1)
            sel = jnp.where(pick, 1.0, sel)
            gate = jnp.where(pick, -jnp.inf, gate)
        sel_ref[h] = sel
        bound = _dot_nt(ka_ref[:, lo:hi].astype(BF16), jnp.abs(q)) * (1.0 + 2.0 ** -6)
        bound_past.append(jnp.max(jnp.where(sel > 0.0, bound, -jnp.inf), axis=0, keepdims=True))

    def live_rows(h, t):
        return sel_ref[h, pl.ds(2 * t, 1), :] > 0.0, sel_ref[h, pl.ds(2 * t + 1, 1), :] > 0.0

    kpos = lax.broadcasted_iota(jnp.int32, (MB_TILE, MB_TILE), 0)
    qpos = lax.broadcasted_iota(jnp.int32, (MB_TILE, MB_TILE), 1)
    kb, qb = kpos // bk, qpos // bk
    causal = jnp.where(kb == qb, jnp.where(kpos <= qpos, 1.0, 0.0), 0.0)
    cross = jnp.where(kb < qb, 1.0, 0.0)

    def visible(h):
        return (causal + cross * sel_ref[h, pl.ds(2 * t_own, 1), :]) > 0.0

    m_diag = []
    for h, lo, hi in heads:
        s = _dot_nt(k_ref[h, t_own], q_ref[:, lo:hi])
        sd_ref[h] = s
        m_diag.append(jnp.max(jnp.where(visible(h), s, -jnp.inf), axis=0, keepdims=True))

    def exact_past_max():
        def tile_max(t, ms):
            out = []
            for h, lo, hi in heads:
                s = _dot_nt(k_ref[h, t], q_ref[:, lo:hi])
                live_a, live_b = live_rows(h, t)
                m_a = jnp.where(live_a, jnp.max(s[:bk], axis=0, keepdims=True), -jnp.inf)
                m_b = jnp.where(live_b, jnp.max(s[bk:], axis=0, keepdims=True), -jnp.inf)
                out.append(jnp.maximum(ms[h], jnp.maximum(m_a, m_b)))
            return tuple(out)

        return lax.fori_loop(0, t_own, tile_max, tuple(jnp.full((1, MB_TILE), -jnp.inf, F32) for _ in heads))

    slack = functools.reduce(jnp.maximum, [jnp.max(bound_past[h] - m_diag[h]) for h, _, _ in heads])
    m_past = lax.cond(slack > MB_BOUND_WINDOW, exact_past_max, lambda: tuple(bound_past))

    m_row, l0 = [], []
    for h, lo, hi in heads:
        m_h = jnp.maximum(m_diag[h], m_past[h])
        p = jnp.where(visible(h), jnp.exp2(sd_ref[h] - m_h), 0.0)
        m_row.append(m_h)
        l0.append(jnp.sum(p, axis=0, keepdims=True))
        acc_ref[h] = _dot(vt_ref[h, t_own], p.astype(BF16))

    def accumulate(tiles, ls):
        work = [(t, h, lo, hi) for t in tiles for h, lo, hi in heads]
        score = lambda t, h, lo, hi: _dot_nt(k_ref[h, t], q_ref[:, lo:hi])
        ls = list(ls)
        scores = score(*work[0])
        for i, (t, h, lo, hi) in enumerate(work):
            s = scores
            if i + 1 < len(work):
                scores = score(*work[i + 1])
            e = jnp.exp2(s - m_row[h])
            live_a, live_b = live_rows(h, t)
            p = jnp.concatenate([jnp.where(live_a, e[:bk], 0.0), jnp.where(live_b, e[bk:], 0.0)], axis=0)
            ls[h] = ls[h] + jnp.sum(p, axis=0, keepdims=True)
            acc_ref[h] += _dot(vt_ref[h, t], p.astype(BF16))
        return tuple(ls)

    l_f = lax.fori_loop(0, t_own // 4, lambda i, ls: accumulate([4 * i + r for r in range(4)], ls), tuple(l0))
    done = (t_own // 4) * 4
    l_f = lax.cond(t_own - done >= 2, lambda ls: accumulate([done, done + 1], ls), lambda ls: ls, l_f)
    l_f = lax.cond(t_own % 2 == 1, lambda ls: accumulate([t_own - 1], ls), lambda ls: ls, l_f)
    for h, lo, hi in heads:
        yb_ref[:, lo:hi] = (acc_ref[h] / l_f[h]).T.astype(BF16)

    o_ref[...] += _dot(yb_ref[...], wo_ref[GROUP_W:, :])


def _moba(q, k_tiles, vt_tiles, k_mean, k_absmax, x, ya, w_out):
    bsz, seq = q.shape[0], q.shape[1]
    n_blk, n_tile = seq // MB_BLOCK, seq // MB_TILE
    once = pl.Buffered(1)
    tile = lambda b, i: (b, i, 0)
    return pl.pallas_call(
        functools.partial(_moba_body, n_blk=n_blk),
        out_shape=jax.ShapeDtypeStruct((bsz, seq, D_MODEL), F32),
        grid=(bsz, n_tile),
        in_specs=[
            pl.BlockSpec((None, MB_TILE, GROUP_W), tile),
            pl.BlockSpec((None, N_HEADS, n_tile, MB_TILE, D_HEAD), lambda b, i: (b, 0, 0, 0, 0),
                         pipeline_mode=once),
            pl.BlockSpec((None, N_HEADS, n_tile, D_HEAD, MB_TILE), lambda b, i: (b, 0, 0, 0, 0),
                         pipeline_mode=once),
            pl.BlockSpec((None, n_blk, GROUP_W), lambda b, i: (b, 0, 0)),
            pl.BlockSpec((None, n_blk, GROUP_W), lambda b, i: (b, 0, 0)),
            pl.BlockSpec((None, MB_TILE, D_MODEL), tile),
            pl.BlockSpec((None, MB_TILE, GROUP_W), tile),
            pl.BlockSpec((D_MODEL, D_MODEL), lambda b, i: (0, 0), pipeline_mode=once),
        ],
        out_specs=pl.BlockSpec((None, MB_TILE, D_MODEL), tile),
        scratch_shapes=[pltpu.VMEM((N_HEADS, n_blk, MB_TILE), F32),
                        pltpu.VMEM((N_HEADS, MB_TILE, MB_TILE), F32),
                        pltpu.VMEM((N_HEADS, D_HEAD, MB_TILE), F32),
                        pltpu.VMEM((MB_TILE, GROUP_W), BF16)],
        compiler_params=pltpu.CompilerParams(
            dimension_semantics=("parallel", "arbitrary"), vmem_limit_bytes=VMEM_LIMIT),
        name="moba",
    )(q, k_tiles, vt_tiles, k_mean, k_absmax, x, ya, w_out)


def _rope_tables(seq):
    half = ROPE_DIM // 2
    inv_freq = np.float64(ROPE_THETA) ** (-np.arange(half, dtype=np.float64) * 2.0 / ROPE_DIM)
    ang = np.arange(seq, dtype=np.float64)[:, None] * inv_freq[None, :]
    cos, sin = np.cos(ang).astype(np.float32), np.sin(ang).astype(np.float32)
    rest = LANES - ROPE_DIM
    cos_t = np.concatenate([cos, cos, np.ones((seq, rest), np.float32)], axis=1)
    sin_t = np.concatenate([-sin, sin, np.zeros((seq, rest), np.float32)], axis=1)
    return jnp.asarray(cos_t), jnp.asarray(sin_t)


def kernel(x, ffn_norm, ffn_w_in, ffn_w_out, mix_norm, ab_w_in, ab_w_out, hgrn_lb_logits, hgrn_out_norm,
           conv_w, conv_b, cd_w_in, cd_w_out, mlstm_gate_bias, mlstm_out_norm, final_norm):
    bsz, seq, d = x.shape
    depth = ffn_norm.shape[0]
    t = bsz * seq
    xt = x.reshape(t, d)
    cos_t, sin_t = _rope_tables(seq)

    ffn_order = [(layer, which) for layer in range(depth) for which in (0, 1)]
    ready = {("ffn", 0, 0): (ffn_w_in[0, 0].astype(BF16), ffn_w_out[0, 0].astype(BF16))}

    def run_ffn(xt, layer, which):
        k = ffn_order.index((layer, which))
        last = k + 1 == len(ffn_order)
        jobs, keys = [], []
        if not last:
            nxt = ffn_order[k + 1]
            jobs += [("plain", ffn_w_in, nxt), ("plain", ffn_w_out, nxt)]
            keys.append((("ffn",) + nxt, 2))
        if which == 0 and layer % 2 == 0:
            jobs += [("plain", ab_w_in, (layer // 2,)), ("plain", ab_w_out, (layer // 2,))]
            keys.append((("ab", layer), 2))
        if which == 0 and layer % 2 == 1:
            jobs += [("cd_w_in", cd_w_in, (layer // 2,)), ("plain", cd_w_out, (layer // 2,))]
            keys.append((("cd", layer), 4))
        w_in16, w_out16 = ready.pop(("ffn", layer, which))
        out = _ffn(xt, ffn_norm[layer, which], w_in16, w_out16, final_norm, last, side_casts=jobs)
        extra = list(out[1:])
        for key, n in keys:
            ready[key], extra = tuple(extra[:n]), extra[n:]
        return out[0]

    for layer in range(depth):
        xt = run_ffn(xt, layer, 0)
        x3 = xt.reshape(bsz, seq, d)
        if layer % 2 == 0:
            e = layer // 2
            w_in16, w_out16 = ready.pop(("ab", layer))
            x3 = _mix_ab(x3, mix_norm[layer], w_in16, hgrn_lb_logits, hgrn_out_norm[e], conv_w[e], conv_b[e],
                         w_out16, layer)
        else:
            o = layer // 2
            w_ml, w_mb, w_gate, w_out16 = ready.pop(("cd", layer))
            bias_rows = jnp.pad(mlstm_gate_bias[o], ((0, 0), (0, LANES - N_HEADS)))
            ya, q_r, k_tiles, vt_tiles, k_mean, k_absmax = _mix_cd(
                x3, mix_norm[layer], w_ml, w_mb, w_gate, bias_rows, mlstm_out_norm[o], cos_t, sin_t)
            n_blk = seq // MB_BLOCK
            x3 = _moba(q_r, k_tiles, vt_tiles,
                       k_mean.reshape(bsz, n_blk, GROUP_W), k_absmax.reshape(bsz, n_blk, GROUP_W),
                       x3, ya, w_out16)
        xt = run_ffn(x3.reshape(t, d), layer, 1)
    return xt.reshape(bsz, seq, d)
```

```python
import functools

import jax
import jax.numpy as jnp
import numpy as np
from jax import lax
from jax.experimental import pallas as pl
from jax.experimental.pallas import tpu as pltpu

F32 = jnp.float32
BF16 = jnp.bfloat16

D_MODEL = 1024
D_FF = 2816
GROUP_W = 512
N_HEADS = 4
D_HEAD = 128
RMS_EPS = 1e-6
HG_CHUNK = 32
ML_CHUNK = 64
MB_BLOCK = 256
MB_TOPK = 3
CONV_W = 3
ROPE_THETA = 500000.0
ROPE_DIM = D_HEAD // 4

LANES = 128
VMEM_LIMIT = 48 * 1024 * 1024

FFN_TM = 1024
FFN_SUB = 256
FFN_CH = 256
PROJ_CH = 256
SEQ_T = 256
SEQ_STEP = 4 * SEQ_T
MB_TILE = 2 * MB_BLOCK
MB_BOUND_WINDOW = 80.0
LOG2E = 1.4426950408889634

AB_COLS = 7 * GROUP_W
CD_COLS = 7 * GROUP_W + 2 * LANES


def _rms(x, gain):
    return x * lax.rsqrt(jnp.mean(x * x, axis=-1, keepdims=True) + RMS_EPS) * gain


def _dot(a, b):
    return jnp.dot(a, b, preferred_element_type=F32)


def _dot_nt(a, b):
    return lax.dot_general(a, b, (((1,), (1,)), ((), ())), preferred_element_type=F32)


def _dot_mask(mask01, x):
    hi = x.astype(BF16)
    lo = (x - hi.astype(F32)).astype(BF16)
    return _dot(mask01, hi) + _dot(mask01, lo)


def _chunk_last(x, chunk):
    rows = x.shape[0]
    return jnp.concatenate([jnp.broadcast_to(x[r0 + chunk - 1:r0 + chunk, :], (chunk, x.shape[1]))
                            for r0 in range(0, rows, chunk)], axis=0)


def _chunk_masks(n, chunk):
    r = lax.broadcasted_iota(jnp.int32, (n, n), 0)
    c = lax.broadcasted_iota(jnp.int32, (n, n), 1)
    same = (r // chunk) == (c // chunk)
    return jnp.where(same, jnp.where(c <= r, 1, 0), 0) > 0, same


def _expand_mask(n, chunk, width):
    n_c = n // chunk
    r = lax.broadcasted_iota(jnp.int32, (n, n_c * width), 0)
    c = lax.broadcasted_iota(jnp.int32, (n, n_c * width), 1)
    return (r // chunk) == (c // width)


def _split_cd_w_in(chunk, ml_ref, mb_ref, gate_ref):
    n_ml = 4 * GROUP_W
    ml_ref[...] = chunk[:, :n_ml].astype(BF16)
    mb_ref[...] = chunk[:, n_ml + 2 * N_HEADS:].astype(BF16)
    win = chunk[:, n_ml:n_ml + LANES]
    lane = lax.broadcasted_iota(jnp.int32, win.shape, 1)
    gate_ref[:, :LANES] = jnp.where(lane < N_HEADS, win, 0.0).astype(BF16)
    gate_ref[:, LANES:] = jnp.where(lane < N_HEADS, pltpu.roll(win, LANES - N_HEADS, axis=1), 0.0).astype(BF16)


def _ffn_body(*refs, final_norm, side_kinds):
    n_in = len(side_kinds)
    n_out = sum(3 if kind == "cd_w_in" else 1 for kind in side_kinds)
    x_ref, g_ref, wg_ref, wu_ref, wo_ref, fg_ref = refs[:6]
    side_in, o_ref = refs[6:6 + n_in], refs[6 + n_in]
    side_out, act_ref = list(refs[7 + n_in:7 + n_in + n_out]), refs[7 + n_in + n_out]

    for kind, src_ref in zip(side_kinds, side_in):
        if kind == "cd_w_in":
            _split_cd_w_in(src_ref[...], side_out.pop(0), side_out.pop(0), side_out.pop(0))
        else:
            side_out.pop(0)[...] = src_ref[...].astype(BF16)

    for r0 in range(0, FFN_TM, FFN_SUB):
        rows = pl.ds(r0, FFN_SUB)
        h = _rms(x_ref[rows, :], g_ref[...]).astype(BF16)
        for c0 in range(0, D_FF, FFN_CH):
            gate = _dot(h, wg_ref[:, c0:c0 + FFN_CH])
            up = _dot(h, wu_ref[:, c0:c0 + FFN_CH])
            act_ref[rows, c0:c0 + FFN_CH] = (gate * jax.nn.sigmoid(gate) * up).astype(BF16)
        y = x_ref[rows, :] + 0.5 * _dot(act_ref[rows, :], wo_ref[...])
        if final_norm:
            y = _rms(y, fg_ref[...])
        o_ref[rows, :] = y


def _ffn(x, gain, w_in16, w_out16, final_gain, final_norm, side_casts=()):
    t = x.shape[0]
    n_steps = t // FFN_TM
    once = pl.Buffered(1)
    in_specs = [
        pl.BlockSpec((FFN_TM, D_MODEL), lambda i: (i, 0)),
        pl.BlockSpec((1, D_MODEL), lambda i: (0, 0)),
        pl.BlockSpec((D_MODEL, D_FF), lambda i: (0, 0), pipeline_mode=once),
        pl.BlockSpec((D_MODEL, D_FF), lambda i: (0, 1), pipeline_mode=once),
        pl.BlockSpec((D_FF, D_MODEL), lambda i: (0, 0), pipeline_mode=once),
        pl.BlockSpec((1, D_MODEL), lambda i: (0, 0)),
    ]
    out_shape = [jax.ShapeDtypeStruct((t, D_MODEL), F32)]
    out_specs = [pl.BlockSpec((FFN_TM, D_MODEL), lambda i: (i, 0))]
    args = [x, gain.reshape(1, D_MODEL), w_in16, w_in16, w_out16, final_gain.reshape(1, D_MODEL)]
    for kind, src, lead in side_casts:
        rows, cols = src.shape[-2:]
        r = rows // n_steps
        in_specs.append(pl.BlockSpec((None,) * len(lead) + (r, cols), lambda i, lead=lead: (*lead, i, 0)))
        args.append(src)
        widths = (4 * GROUP_W, 3 * GROUP_W, 2 * LANES) if kind == "cd_w_in" else (cols,)
        for w in widths:
            out_shape.append(jax.ShapeDtypeStruct((rows, w), BF16))
            out_specs.append(pl.BlockSpec((r, w), lambda i: (i, 0)))
    return pl.pallas_call(
        functools.partial(_ffn_body, final_norm=final_norm, side_kinds=tuple(k for k, _, _ in side_casts)),
        out_shape=out_shape,
        grid=(n_steps,),
        in_specs=in_specs,
        out_specs=out_specs,
        scratch_shapes=[pltpu.VMEM((FFN_TM, D_FF), BF16)],
        compiler_params=pltpu.CompilerParams(
            dimension_semantics=("parallel",), vmem_limit_bytes=VMEM_LIMIT),
        name="ffn",
    )(*args)


class _ChunkedProjection:
    def __init__(self, x_ref, g_ref, w_refs, p_ref):
        self.h = _rms(x_ref[...], g_ref[...]).astype(BF16)
        self.p_ref = p_ref
        self.todo = [(w_ref, c0) for w_ref in w_refs for c0 in range(0, w_ref.shape[1], PROJ_CH)]
        self.done = 0

    def emit(self, n_chunks):
        end = len(self.todo) if n_chunks is None else min(len(self.todo), self.done + n_chunks)
        for i in range(self.done, end):
            w_ref, c0 = self.todo[i]
            self.p_ref[:, i * PROJ_CH:(i + 1) * PROJ_CH] = _dot(self.h, w_ref[:, c0:c0 + PROJ_CH])
        self.done = end


def _project_in(x_ref, g_ref, w_refs, p_ref):
    _ChunkedProjection(x_ref, g_ref, w_refs, p_ref).emit(None)


def _tiles_with_lookahead(x_ref, xn_ref, g_ref, w_refs, pa_ref, pb_ref, tile_fn):
    n_tiles = x_ref.shape[0] // SEQ_T
    assert n_tiles % 2 == 0
    rows = lambda k: pl.ds(k * SEQ_T, SEQ_T)
    bufs = (pa_ref, pb_ref)

    @pl.when(pl.program_id(1) == 0)
    def _():
        _project_in(x_ref.at[rows(0)], g_ref, w_refs, pa_ref)

    for k in range(n_tiles):
        nxt = x_ref.at[rows(k + 1)] if k + 1 < n_tiles else xn_ref.at[rows(0)]
        ahead = _ChunkedProjection(nxt, g_ref, w_refs, bufs[(k + 1) % 2])
        tile_fn(k, bufs[k % 2], ahead)
        ahead.emit(None)


def _one_tile_with_lookahead(x_ref, xn_ref, g_ref, w_refs, pa_ref, pb_ref, tile_fn):
    s_idx = pl.program_id(1)

    @pl.when(s_idx == 0)
    def _():
        _project_in(x_ref, g_ref, w_refs, pa_ref)

    def branch(parity, p_cur, p_nxt):
        @pl.when(s_idx % 2 == parity)
        def _():
            ahead = _ChunkedProjection(xn_ref, g_ref, w_refs, p_nxt)
            tile_fn(p_cur, ahead)
            ahead.emit(None)

    branch(0, pa_ref, pb_ref)
    branch(1, pb_ref, pa_ref)


def _next_tile(n_s):
    return lambda b, s: (b, jnp.minimum(s + 1, n_s - 1), 0)


def _mix_ab_body(x_ref, xn_ref, g_ref, w_ref, lbl_ref, hgn_ref, cw_ref, cb_ref, wo_ref, o_ref,
                 pa_ref, pb_ref, y_ref, st_ref, zb_ref, *, layer):
    def tile(half, p_ref, ahead):
        rows = pl.ds(half * SEQ_T, SEQ_T)
        _mix_ab_tile(p_ref, ahead, x_ref=x_ref.at[rows], lbl_ref=lbl_ref, hgn_ref=hgn_ref, cw_ref=cw_ref,
                     cb_ref=cb_ref, wo_ref=wo_ref, o_ref=o_ref.at[rows], y_ref=y_ref, st_ref=st_ref,
                     zb_ref=zb_ref, layer=layer, starts_sequence=half == 0)

    _tiles_with_lookahead(x_ref, xn_ref, g_ref, (w_ref,), pa_ref, pb_ref, tile)


def _mix_ab_tile(p_ref, ahead, *, x_ref, lbl_ref, hgn_ref, cw_ref, cb_ref, wo_ref, o_ref, y_ref, st_ref, zb_ref,
                 layer, starts_sequence):
    n_c = SEQ_T // HG_CHUNK

    if starts_sequence:
        @pl.when(pl.program_id(1) == 0)
        def _():
            st_ref[...] = jnp.zeros_like(st_ref)
            zb_ref[0:8, :] = jnp.zeros((8, GROUP_W), F32)

    lg = lbl_ref[...]
    ex = jnp.exp(lg - jnp.max(lg, axis=0, keepdims=True))
    sm = ex / jnp.sum(ex, axis=0, keepdims=True)
    lb = jnp.sum(sm[0:layer + 1, :], axis=0, keepdims=True)

    tril, _ = _chunk_masks(SEQ_T, HG_CHUNK)
    tril01 = jnp.where(tril, 1.0, 0.0).astype(BF16)
    emask = _expand_mask(SEQ_T, HG_CHUNK, D_HEAD)

    f = lb + (1.0 - lb) * jax.nn.sigmoid(p_ref[:, GROUP_W:2 * GROUP_W])
    logf = jnp.log(f)
    b_all = _dot_mask(tril01, logf)
    e_all = _chunk_last(b_all, HG_CHUNK)
    ahead.emit(2)

    for h in range(N_HEADS):
        ahead.emit(1)
        lo, hi = h * D_HEAD, (h + 1) * D_HEAD
        q = p_ref[:, lo:hi]
        v = p_ref[:, 2 * GROUP_W + lo:2 * GROUP_W + hi]
        g = p_ref[:, 3 * GROUP_W + lo:3 * GROUP_W + hi]
        b = b_all[:, lo:hi]
        e = e_all[:, lo:hi]
        kk = 1.0 - f[:, lo:hi]
        q_dec = (q * jax.nn.sigmoid(q) * jnp.exp(b)).astype(BF16)
        k_dec = (kk * jnp.exp(-b)).astype(BF16)
        k_end = (kk * jnp.exp(e - b)).astype(BF16)
        v_t = v.T.astype(BF16)

        attn = jnp.where(tril, _dot_nt(q_dec, k_dec), 0.0)
        ahead.emit(1)
        o_t = _dot_nt(v_t, attn.astype(BF16))

        k_exp = jnp.where(emask, jnp.tile(k_end, (1, n_c)), jnp.zeros((), BF16))
        d_all = _dot(v_t, k_exp)
        ahead.emit(1)

        st = st_ref[h]
        prev = []
        for c in range(n_c):
            prev.append(st)
            decay = jnp.exp(e[c * HG_CHUNK:c * HG_CHUNK + 1, :])
            st = decay * st + d_all[:, c * D_HEAD:(c + 1) * D_HEAD]
        st_ref[h] = st
        s_prev = jnp.concatenate(prev, axis=1).astype(BF16)
        q_exp = jnp.where(emask, jnp.tile(q_dec, (1, n_c)), jnp.zeros((), BF16))
        o = (o_t + _dot_nt(s_prev, q_exp)).T

        o = _rms(o, hgn_ref[:, lo:hi]) * (g * jax.nn.sigmoid(g))
        y_ref[:, lo:hi] = o.astype(BF16)

    z = p_ref[:, 5 * GROUP_W:6 * GROUP_W] * p_ref[:, 6 * GROUP_W:7 * GROUP_W]
    zb_ref[8:SEQ_T + 8, :] = z
    z1 = zb_ref[7:SEQ_T + 7, :]
    z2 = zb_ref[6:SEQ_T + 6, :]
    y = cb_ref[...] + cw_ref[0:1, :] * z2 + cw_ref[1:2, :] * z1 + cw_ref[2:3, :] * z
    y_ref[:, GROUP_W:] = (p_ref[:, 4 * GROUP_W:5 * GROUP_W] * y).astype(BF16)
    zb_ref[0:8, :] = zb_ref[SEQ_T:SEQ_T + 8, :]

    o_ref[...] = x_ref[...] + _dot(y_ref[...], wo_ref[...])


def _mix_ab(x, gain, w_in, lb_logits, hg_norm, conv_w, conv_b, w_out, layer):
    bsz, seq = x.shape[0], x.shape[1]
    n_l = lb_logits.shape[0]
    row = lambda b, s: (0, 0)
    once = pl.Buffered(1)
    return pl.pallas_call(
        functools.partial(_mix_ab_body, layer=layer),
        out_shape=jax.ShapeDtypeStruct((bsz, seq, D_MODEL), F32),
        grid=(bsz, seq // SEQ_STEP),
        in_specs=[
            pl.BlockSpec((None, SEQ_STEP, D_MODEL), lambda b, s: (b, s, 0)),
            pl.BlockSpec((None, SEQ_STEP, D_MODEL), _next_tile(seq // SEQ_STEP)),
            pl.BlockSpec((1, D_MODEL), row),
            pl.BlockSpec((D_MODEL, AB_COLS), row, pipeline_mode=once),
            pl.BlockSpec((n_l, GROUP_W), row),
            pl.BlockSpec((1, GROUP_W), row),
            pl.BlockSpec((CONV_W, GROUP_W), row),
            pl.BlockSpec((1, GROUP_W), row),
            pl.BlockSpec((D_MODEL, D_MODEL), row, pipeline_mode=once),
        ],
        out_specs=pl.BlockSpec((None, SEQ_STEP, D_MODEL), lambda b, s: (b, s, 0)),
        scratch_shapes=[pltpu.VMEM((SEQ_T, AB_COLS), F32), pltpu.VMEM((SEQ_T, AB_COLS), F32),
                        pltpu.VMEM((SEQ_T, D_MODEL), BF16),
                        pltpu.VMEM((N_HEADS, D_HEAD, D_HEAD), F32), pltpu.VMEM((SEQ_T + 8, GROUP_W), F32)],
        compiler_params=pltpu.CompilerParams(
            dimension_semantics=("arbitrary", "arbitrary"), vmem_limit_bytes=VMEM_LIMIT),
        name="mix_ab",
    )(x, x, gain.reshape(1, D_MODEL), w_in, lb_logits, hg_norm.reshape(1, GROUP_W), conv_w,
      conv_b.reshape(1, GROUP_W), w_out)


def _mix_cd_body(x_ref, xn_ref, g_ref, w_ml_ref, w_mb_ref, w_gate_ref, bias_ref, mln_ref, cos_ref, sin_ref,
                 o_ref, qo_ref, ko_ref, vt_ref, km_ref, ka_ref, pa_ref, pb_ref, c_ref, n_ref, m_ref):
    tile = functools.partial(_mix_cd_tile, bias_ref=bias_ref, mln_ref=mln_ref, cos_ref=cos_ref, sin_ref=sin_ref,
                             o_ref=o_ref, qo_ref=qo_ref, ko_ref=ko_ref, vt_ref=vt_ref, km_ref=km_ref,
                             ka_ref=ka_ref, c_ref=c_ref, n_ref=n_ref, m_ref=m_ref, starts_sequence=True)
    _one_tile_with_lookahead(x_ref, xn_ref, g_ref, (w_ml_ref, w_mb_ref, w_gate_ref), pa_ref, pb_ref, tile)


def _mix_cd_tile(p_ref, ahead, *, bias_ref, mln_ref, cos_ref, sin_ref, o_ref, qo_ref, ko_ref, vt_ref, km_ref, ka_ref,
                 c_ref, n_ref, m_ref, starts_sequence):
    n_c = SEQ_T // ML_CHUNK
    gi_ref = p_ref.at[:, 7 * GROUP_W:7 * GROUP_W + LANES]
    gf_ref = p_ref.at[:, 7 * GROUP_W + LANES:7 * GROUP_W + 2 * LANES]

    if starts_sequence:
        @pl.when(pl.program_id(1) == 0)
        def _():
            c_ref[...] = jnp.zeros_like(c_ref)
            n_ref[...] = jnp.zeros_like(n_ref)
            m_ref[...] = jnp.zeros_like(m_ref)

    cos_t, sin_t = cos_ref[...], sin_ref[...]
    for h in range(N_HEADS):
        ahead.emit(1)
        lo, hi = h * D_HEAD, (h + 1) * D_HEAD
        q = _rope(p_ref[:, 4 * GROUP_W + lo:4 * GROUP_W + hi], cos_t, sin_t) * (D_HEAD ** -0.5 * LOG2E)
        k = _rope(p_ref[:, 5 * GROUP_W + lo:5 * GROUP_W + hi], cos_t, sin_t)
        qo_ref[:, lo:hi] = q.astype(BF16)
        ko_ref[h] = k.astype(BF16)
        vt_ref[h] = p_ref[:, 6 * GROUP_W + lo:6 * GROUP_W + hi].T.astype(BF16)
        km_ref[:, lo:hi] = jnp.mean(k, axis=0, keepdims=True)
        ka_ref[:, lo:hi] = jnp.max(jnp.abs(k), axis=0, keepdims=True)

    tril, same = _chunk_masks(SEQ_T, ML_CHUNK)
    tril01 = jnp.where(tril, 1.0, 0.0).astype(BF16)
    same01 = jnp.where(same, 1.0, 0.0).astype(BF16)
    emask = _expand_mask(SEQ_T, ML_CHUNK, D_HEAD)

    log_i = gi_ref[...] + bias_ref[0:1, :]
    log_f = jax.nn.log_sigmoid(gf_ref[...] + bias_ref[1:2, :])
    b_col = _dot_mask(tril01, log_f)
    e_col = _dot_mask(same01, log_f)
    w_end = e_col - b_col + log_i

    m = m_ref[...]
    m_prev_rows, m_new_rows, a_rows = [], [], []
    for c in range(n_c):
        r0 = c * ML_CHUNK
        be = e_col[r0:r0 + 1, :]
        m_end = jnp.max(w_end[r0:r0 + ML_CHUNK, :], axis=0, keepdims=True)
        m_new = jnp.maximum(be + m, m_end)
        a_rows.append(jnp.exp(be + m - m_new))
        m_prev_rows.append(jnp.broadcast_to(m, (ML_CHUNK, LANES)))
        m_new_rows.append(jnp.broadcast_to(m_new, (ML_CHUNK, LANES)))
        m = m_new
    m_ref[...] = m
    log_inter = b_col + jnp.concatenate(m_prev_rows, axis=0)
    wk_scale = jnp.exp(w_end - jnp.concatenate(m_new_rows, axis=0))
    r_rows = (b_col - log_i).T

    for h in range(N_HEADS):
        ahead.emit(3)
        lo, hi = h * D_HEAD, (h + 1) * D_HEAD
        q = p_ref[:, lo:hi] * (D_HEAD ** -0.5)
        k = p_ref[:, GROUP_W + lo:GROUP_W + hi]
        v = p_ref[:, 2 * GROUP_W + lo:2 * GROUP_W + hi]
        og = p_ref[:, 3 * GROUP_W + lo:3 * GROUP_W + hi]
        q16, k16, v16 = q.astype(BF16), k.astype(BF16), v.astype(BF16)

        d_log = jnp.where(tril, b_col[:, h:h + 1] - r_rows[h:h + 1, :], -jnp.inf)
        linter = log_inter[:, h:h + 1]
        m_t = jnp.maximum(linter, jnp.max(d_log, axis=-1, keepdims=True))
        w = jnp.exp(d_log - m_t) * _dot_nt(q16, k16)
        a_in = jnp.exp(linter - m_t)
        num = _dot(w.astype(BF16), v16)
        den = jnp.sum(w, axis=-1, keepdims=True)

        wk = wk_scale[:, h:h + 1] * k
        v_exp = jnp.where(emask, jnp.tile(v16, (1, n_c)), jnp.zeros((), BF16))
        dc_all = _dot(wk.T.astype(BF16), v_exp)

        c_mat = c_ref[h]
        n_vec = n_ref[h]
        c_prev, n_prev = [], []
        for c in range(n_c):
            r0 = c * ML_CHUNK
            c_prev.append(c_mat)
            n_prev.append(jnp.broadcast_to(n_vec, (ML_CHUNK, D_HEAD)))
            a = a_rows[c][:, h:h + 1]
            c_mat = a * c_mat + dc_all[:, c * D_HEAD:(c + 1) * D_HEAD]
            n_vec = a * n_vec + jnp.sum(wk[r0:r0 + ML_CHUNK, :], axis=0, keepdims=True)
        c_ref[h] = c_mat
        n_ref[h] = n_vec

        q_exp = jnp.where(emask, jnp.tile(q16, (1, n_c)), jnp.zeros((), BF16))
        q_c = _dot(q_exp, jnp.concatenate(c_prev, axis=0).astype(BF16))
        q_n = jnp.sum(q * jnp.concatenate(n_prev, axis=0), axis=-1, keepdims=True)
        num = num + a_in * q_c
        den = den + a_in * q_n
        hh = num / jnp.maximum(jnp.abs(den), jnp.exp(-m_t))
        o_ref[:, lo:hi] = (_rms(hh, mln_ref[:, lo:hi]) * jax.nn.sigmoid(og)).astype(BF16)


def _mix_cd(x, gain, w_ml, w_mb, w_gate, bias_rows, ml_norm, cos_t, sin_t):
    bsz, seq = x.shape[0], x.shape[1]
    n_blk, n_tile = seq // MB_BLOCK, seq // MB_TILE
    assert SEQ_T == MB_BLOCK and MB_TILE == 2 * MB_BLOCK
    row = lambda b, s: (0, 0)
    stat = pl.BlockSpec((None, None, 1, GROUP_W), lambda b, s: (b, s, 0, 0))
    return pl.pallas_call(
        _mix_cd_body,
        out_shape=[
            jax.ShapeDtypeStruct((bsz, seq, GROUP_W), BF16),
            jax.ShapeDtypeStruct((bsz, seq, GROUP_W), BF16),
            jax.ShapeDtypeStruct((bsz, N_HEADS, n_tile, MB_TILE, D_HEAD), BF16),
            jax.ShapeDtypeStruct((bsz, N_HEADS, n_tile, D_HEAD, MB_TILE), BF16),
            jax.ShapeDtypeStruct((bsz, n_blk, 1, GROUP_W), F32),
            jax.ShapeDtypeStruct((bsz, n_blk, 1, GROUP_W), F32),
        ],
        grid=(bsz, seq // SEQ_T),
        in_specs=[
            pl.BlockSpec((None, SEQ_T, D_MODEL), lambda b, s: (b, s, 0)),
            pl.BlockSpec((None, SEQ_T, D_MODEL), _next_tile(seq // SEQ_T)),
            pl.BlockSpec((1, D_MODEL), row),
            pl.BlockSpec(w_ml.shape, row, pipeline_mode=pl.Buffered(1)),
            pl.BlockSpec(w_mb.shape, row, pipeline_mode=pl.Buffered(1)),
            pl.BlockSpec(w_gate.shape, row, pipeline_mode=pl.Buffered(1)),
            pl.BlockSpec((2, LANES), row),
            pl.BlockSpec((1, GROUP_W), row),
            pl.BlockSpec((SEQ_T, LANES), lambda b, s: (s, 0)),
            pl.BlockSpec((SEQ_T, LANES), lambda b, s: (s, 0)),
        ],
        out_specs=[
            pl.BlockSpec((None, SEQ_T, GROUP_W), lambda b, s: (b, s, 0)),
            pl.BlockSpec((None, SEQ_T, GROUP_W), lambda b, s: (b, s, 0)),
            pl.BlockSpec((None, N_HEADS, None, MB_BLOCK, D_HEAD), lambda b, s: (b, 0, s // 2, s % 2, 0)),
            pl.BlockSpec((None, N_HEADS, None, D_HEAD, MB_BLOCK), lambda b, s: (b, 0, s // 2, 0, s % 2)),
            stat, stat,
        ],
        scratch_shapes=[pltpu.VMEM((SEQ_T, CD_COLS), F32), pltpu.VMEM((SEQ_T, CD_COLS), F32),
                        pltpu.VMEM((N_HEADS, D_HEAD, D_HEAD), F32),
                        pltpu.VMEM((N_HEADS, 1, D_HEAD), F32),
                        pltpu.VMEM((1, LANES), F32)],
        compiler_params=pltpu.CompilerParams(
            dimension_semantics=("arbitrary", "arbitrary"), vmem_limit_bytes=VMEM_LIMIT),
        name="mix_cd",
    )(x, x, gain.reshape(1, D_MODEL), w_ml, w_mb, w_gate, bias_rows, ml_norm.reshape(1, GROUP_W), cos_t, sin_t)


def _rope(x, cos_t, sin_t):
    lane = lax.broadcasted_iota(jnp.int32, x.shape, 1)
    half = ROPE_DIM // 2
    swapped = jnp.where(lane < half, pltpu.roll(x, LANES - half, axis=1), pltpu.roll(x, half, axis=1))
    return x * cos_t + swapped * sin_t


def _moba_body(q_ref, k_ref, vt_ref, km_ref, ka_ref, x_ref, ya_ref, wo_ref, o_ref,
               sel_ref, sd_ref, acc_ref, yb_ref, *, n_blk):
    t_own = pl.program_id(1)
    heads = [(h, h * D_HEAD, (h + 1) * D_HEAD) for h in range(N_HEADS)]
    bk = MB_BLOCK
    o_ref[...] = x_ref[...] + _dot(ya_ref[...], wo_ref[:GROUP_W, :])

    blk = lax.broadcasted_iota(jnp.int32, (n_blk, MB_TILE), 0)
    lane = lax.broadcasted_iota(jnp.int32, (n_blk, MB_TILE), 1)
    own = 2 * t_own + lane // bk
    bound_past = []
    for h, lo, hi in heads:
        q = q_ref[:, lo:hi]
        km = km_ref[:, lo:hi]
        km_hi = km.astype(BF16)
        km_lo = (km - km_hi.astype(F32)).astype(BF16)
        gate = _dot_nt(km_hi, q) + _dot_nt(km_lo, q)
        gate = jnp.where(blk < own, gate, -jnp.inf)
        sel = jnp.zeros(gate.shape, F32)
        for _ in range(MB_TOPK):
            mx = jnp.max(gate, axis=0, keepdims=True)
            idx = jnp.min(jnp.where(gate == mx, blk, n_blk), axis=0, keepdims=True)
            pick = blk == jnp.where(mx > -jnp.inf, idx, -1)
            sel = jnp.where(pick, 1.0, sel)
            gate = jnp.where(pick, -jnp.inf, gate)
        sel_ref[h] = sel
        bound = _dot_nt(ka_ref[:, lo:hi].astype(BF16), jnp.abs(q)) * (1.0 + 2.0 ** -6)
        bound_past.append(jnp.max(jnp.where(sel > 0.0, bound, -jnp.inf), axis=0, keepdims=True))

    def live_rows(h, t):
        return sel_ref[h, pl.ds(2 * t, 1), :] > 0.0, sel_ref[h, pl.ds(2 * t + 1, 1), :] > 0.0

    kpos = lax.broadcasted_iota(jnp.int32, (MB_TILE, MB_TILE), 0)
    qpos = lax.broadcasted_iota(jnp.int32, (MB_TILE, MB_TILE), 1)
    kb, qb = kpos // bk, qpos // bk
    causal = jnp.where(kb == qb, jnp.where(kpos <= qpos, 1.0, 0.0), 0.0)
    cross = jnp.where(kb < qb, 1.0, 0.0)

    def visible(h):
        return (causal + cross * sel_ref[h, pl.ds(2 * t_own, 1), :]) > 0.0

    m_diag = []
    for h, lo, hi in heads:
        s = _dot_nt(k_ref[h, t_own], q_ref[:, lo:hi])
        sd_ref[h] = s
        m_diag.append(jnp.max(jnp.where(visible(h), s, -jnp.inf), axis=0, keepdims=True))

    def exact_past_max():
        def tile_max(t, ms):
            out = []
            for h, lo, hi in heads:
                s = _dot_nt(k_ref[h, t], q_ref[:, lo:hi])
                live_a, live_b = live_rows(h, t)
                m_a = jnp.where(live_a, jnp.max(s[:bk], axis=0, keepdims=True), -jnp.inf)
                m_b = jnp.where(live_b, jnp.max(s[bk:], axis=0, keepdims=True), -jnp.inf)
                out.append(jnp.maximum(ms[h], jnp.maximum(m_a, m_b)))
            return tuple(out)

        return lax.fori_loop(0, t_own, tile_max, tuple(jnp.full((1, MB_TILE), -jnp.inf, F32) for _ in heads))

    slack = functools.reduce(jnp.maximum, [jnp.max(bound_past[h] - m_diag[h]) for h, _, _ in heads])
    m_past = lax.cond(slack > MB_BOUND_WINDOW, exact_past_max, lambda: tuple(bound_past))

    m_row, l0 = [], []
    for h, lo, hi in heads:
        m_h = jnp.maximum(m_diag[h], m_past[h])
        p = jnp.where(visible(h), jnp.exp2(sd_ref[h] - m_h), 0.0)
        m_row.append(m_h)
        l0.append(jnp.sum(p, axis=0, keepdims=True))
        acc_ref[h] = _dot(vt_ref[h, t_own], p.astype(BF16))

    def accumulate(tiles, ls):
        work = [(t, h, lo, hi) for t in tiles for h, lo, hi in heads]
        score = lambda t, h, lo, hi: _dot_nt(k_ref[h, t], q_ref[:, lo:hi])
        ls = list(ls)
        scores = score(*work[0])
        for i, (t, h, lo, hi) in enumerate(work):
            s = scores
            if i + 1 < len(work):
                scores = score(*work[i + 1])
            e = jnp.exp2(s - m_row[h])
            live_a, live_b = live_rows(h, t)
            p = jnp.concatenate([jnp.where(live_a, e[:bk], 0.0), jnp.where(live_b, e[bk:], 0.0)], axis=0)
            ls[h] = ls[h] + jnp.sum(p, axis=0, keepdims=True)
            acc_ref[h] += _dot(vt_ref[h, t], p.astype(BF16))
        return tuple(ls)

    l_f = lax.fori_loop(0, t_own // 4, lambda i, ls: accumulate([4 * i + r for r in range(4)], ls), tuple(l0))
    done = (t_own // 4) * 4
    l_f = lax.cond(t_own - done >= 2, lambda ls: accumulate([done, done + 1], ls), lambda ls: ls, l_f)
    l_f = lax.cond(t_own % 2 == 1, lambda ls: accumulate([t_own - 1], ls), lambda ls: ls, l_f)
    for h, lo, hi in heads:
        yb_ref[:, lo:hi] = (acc_ref[h] / l_f[h]).T.astype(BF16)

    o_ref[...] += _dot(yb_ref[...], wo_ref[GROUP_W:, :])


def _moba(q, k_tiles, vt_tiles, k_mean, k_absmax, x, ya, w_out):
    bsz, seq = q.shape[0], q.shape[1]
    n_blk, n_tile = seq // MB_BLOCK, seq // MB_TILE
    once = pl.Buffered(1)
    tile = lambda b, i: (b, i, 0)
    return pl.pallas_call(
        functools.partial(_moba_body, n_blk=n_blk),
        out_shape=jax.ShapeDtypeStruct((bsz, seq, D_MODEL), F32),
        grid=(bsz, n_tile),
        in_specs=[
            pl.BlockSpec((None, MB_TILE, GROUP_W), tile),
            pl.BlockSpec((None, N_HEADS, n_tile, MB_TILE, D_HEAD), lambda b, i: (b, 0, 0, 0, 0),
                         pipeline_mode=once),
            pl.BlockSpec((None, N_HEADS, n_tile, D_HEAD, MB_TILE), lambda b, i: (b, 0, 0, 0, 0),
                         pipeline_mode=once),
            pl.BlockSpec((None, n_blk, GROUP_W), lambda b, i: (b, 0, 0)),
            pl.BlockSpec((None, n_blk, GROUP_W), lambda b, i: (b, 0, 0)),
            pl.BlockSpec((None, MB_TILE, D_MODEL), tile),
            pl.BlockSpec((None, MB_TILE, GROUP_W), tile),
            pl.BlockSpec((D_MODEL, D_MODEL), lambda b, i: (0, 0), pipeline_mode=once),
        ],
        out_specs=pl.BlockSpec((None, MB_TILE, D_MODEL), tile),
        scratch_shapes=[pltpu.VMEM((N_HEADS, n_blk, MB_TILE), F32),
                        pltpu.VMEM((N_HEADS, MB_TILE, MB_TILE), F32),
                        pltpu.VMEM((N_HEADS, D_HEAD, MB_TILE), F32),
                        pltpu.VMEM((MB_TILE, GROUP_W), BF16)],
        compiler_params=pltpu.CompilerParams(
            dimension_semantics=("parallel", "arbitrary"), vmem_limit_bytes=VMEM_LIMIT),
        name="moba",
    )(q, k_tiles, vt_tiles, k_mean, k_absmax, x, ya, w_out)


def _rope_tables(seq):
    half = ROPE_DIM // 2
    inv_freq = np.float64(ROPE_THETA) ** (-np.arange(half, dtype=np.float64) * 2.0 / ROPE_DIM)
    ang = np.arange(seq, dtype=np.float64)[:, None] * inv_freq[None, :]
    cos, sin = np.cos(ang).astype(np.float32), np.sin(ang).astype(np.float32)
    rest = LANES - ROPE_DIM
    cos_t = np.concatenate([cos, cos, np.ones((seq, rest), np.float32)], axis=1)
    sin_t = np.concatenate([-sin, sin, np.zeros((seq, rest), np.float32)], axis=1)
    return jnp.asarray(cos_t), jnp.asarray(sin_t)


def kernel(x, ffn_norm, ffn_w_in, ffn_w_out, mix_norm, ab_w_in, ab_w_out, hgrn_lb_logits, hgrn_out_norm,
           conv_w, conv_b, cd_w_in, cd_w_out, mlstm_gate_bias, mlstm_out_norm, final_norm):
    bsz, seq, d = x.shape
    depth = ffn_norm.shape[0]
    t = bsz * seq
    xt = x.reshape(t, d)
    cos_t, sin_t = _rope_tables(seq)

    ffn_order = [(layer, which) for layer in range(depth) for which in (0, 1)]
    ready = {("ffn", 0, 0): (ffn_w_in[0, 0].astype(BF16), ffn_w_out[0, 0].astype(BF16))}

    def run_ffn(xt, layer, which):
        k = ffn_order.index((layer, which))
        last = k + 1 == len(ffn_order)
        jobs, keys = [], []
        if not last:
            nxt = ffn_order[k + 1]
            jobs += [("plain", ffn_w_in, nxt), ("plain", ffn_w_out, nxt)]
            keys.append((("ffn",) + nxt, 2))
        if which == 0 and layer % 2 == 0:
            jobs += [("plain", ab_w_in, (layer // 2,)), ("plain", ab_w_out, (layer // 2,))]
            keys.append((("ab", layer), 2))
        if which == 0 and layer % 2 == 1:
            jobs += [("cd_w_in", cd_w_in[layer // 2], ()), ("plain", cd_w_out, (layer // 2,))]
            keys.append((("cd", layer), 4))
        w_in16, w_out16 = ready.pop(("ffn", layer, which))
        out = _ffn(xt, ffn_norm[layer, which], w_in16, w_out16, final_norm, last, side_casts=jobs)
        extra = list(out[1:])
        for key, n in keys:
            ready[key], extra = tuple(extra[:n]), extra[n:]
        return out[0]

    for layer in range(depth):
        xt = run_ffn(xt, layer, 0)
        x3 = xt.reshape(bsz, seq, d)
        if layer % 2 == 0:
            e = layer // 2
            w_in16, w_out16 = ready.pop(("ab", layer))
            x3 = _mix_ab(x3, mix_norm[layer], w_in16, hgrn_lb_logits, hgrn_out_norm[e], conv_w[e], conv_b[e],
                         w_out16, layer)
        else:
            o = layer // 2
            w_ml, w_mb, w_gate, w_out16 = ready.pop(("cd", layer))
            bias_rows = jnp.pad(mlstm_gate_bias[o], ((0, 0), (0, LANES - N_HEADS)))
            ya, q_r, k_tiles, vt_tiles, k_mean, k_absmax = _mix_cd(
                x3, mix_norm[layer], w_ml, w_mb, w_gate, bias_rows, mlstm_out_norm[o], cos_t, sin_t)
            n_blk = seq // MB_BLOCK
            x3 = _moba(q_r, k_tiles, vt_tiles,
                       k_mean.reshape(bsz, n_blk, GROUP_W), k_absmax.reshape(bsz, n_blk, GROUP_W),
                       x3, ya, w_out16)
        xt = run_ffn(x3.reshape(t, d), layer, 1)
    return xt.reshape(bsz, seq, d)
```
